```python
import jax, jax.numpy as jnp
from jax import lax
import numpy as np

D_MODEL = 2048
BATCH = 2
SEQ = 8192
DEPTH = 1
DEC_BATCH = 32
DEC_SEQ = 32
PAST_LEN = 2048

CHUNK = 64
REC_CHUNK = CHUNK
REC_SUB = 16
HG_DK = 128
HG_HEADS = D_MODEL // HG_DK
HG_DV = 128
HG_WIDTH = HG_HEADS * HG_DV
MLP_CHUNK = 128
MLP_GROUPS = 4
MLP_WIDTH = D_MODEL // 2
MLP_GDIM = MLP_WIDTH // MLP_GROUPS
N_GROUPS = 4
E_PER_GROUP = 8
N_EXPERTS = N_GROUPS * E_PER_GROUP
TOP_K = 2
D_EXPERT = D_MODEL // 2
MOE_BLOCK = 128
PLE_DIM = 256
EPS = 1e-6
IN_WIDTHS = (HG_HEADS * HG_DK, HG_HEADS * HG_DK, HG_WIDTH, HG_WIDTH, MLP_WIDTH, MLP_WIDTH, D_MODEL, D_MODEL)
IN_WIDTH = sum(IN_WIDTHS)

kernel_name = 'hgrn2_gmlp_hiermoe_stream_step'


def rmsnorm(x, g):
    xf = x.astype(jnp.float32)
    y = xf * lax.rsqrt(jnp.mean(xf * xf, axis=-1, keepdims=True) + EPS)
    return (y * g.astype(jnp.float32)).astype(x.dtype)


def layernorm(x, g, b):
    xf = x.astype(jnp.float32)
    mu = jnp.mean(xf, axis=-1, keepdims=True)
    xc = xf - mu
    y = xc * lax.rsqrt(jnp.mean(xc * xc, axis=-1, keepdims=True) + EPS)
    return (y * g.astype(jnp.float32) + b.astype(jnp.float32)).astype(x.dtype)


def _hgrn2_chunk(s, inp):
    q, lf, k, v = inp
    nb, nh = q.shape[0], q.shape[1]
    ns = REC_CHUNK // REC_SUB
    b = jnp.cumsum(lf, axis=2)
    b_last = b[:, :, -1:, :]
    o_inter = jnp.einsum('bhtk,bhkv->bhtv', q * jnp.exp(b), s)
    r = (b - lf)[:, :, ::REC_SUB]
    qs = q.reshape(nb, nh, ns, REC_SUB, HG_DK) * jnp.exp(b.reshape(nb, nh, ns, REC_SUB, HG_DK) - r[:, :, :, None, :])
    pos = jnp.arange(REC_CHUNK)
    reach = pos[None, :] < (jnp.arange(ns)[:, None] + 1) * REC_SUB
    ex = jnp.where(reach[None, None, :, :, None], r[:, :, :, None, :] - b[:, :, None, :, :], -jnp.inf)
    ks = k[:, :, None] * jnp.exp(ex)
    a = jnp.einsum('bhjtk,bhjsk->bhjts', qs, ks).reshape(nb, nh, REC_CHUNK, REC_CHUNK)
    a = jnp.where(pos[:, None] >= pos[None, :], a, 0.0)
    o = o_inter + jnp.einsum('bhts,bhsv->bhtv', a, v)
    s_new = jnp.exp(b_last[:, :, 0, :])[..., None] * s + jnp.einsum('bhsk,bhsv->bhkv', k * jnp.exp(b_last - b), v)
    return s_new, o


def hgrn2_recurrence(q, lf, k, v, s0):
    nb, t = q.shape[0], q.shape[1]
    pad = (-t) % REC_CHUNK
    n_c = (t + pad) // REC_CHUNK

    def blocks(a):
        a = jnp.pad(a, ((0, 0), (0, pad), (0, 0), (0, 0)))
        return a.reshape(nb, n_c, REC_CHUNK, HG_HEADS, a.shape[-1]).transpose(1, 0, 3, 2, 4)

    s_fin, o = lax.scan(_hgrn2_chunk, s0, (blocks(q), blocks(lf), blocks(k), blocks(v)))
    o = o.transpose(1, 0, 3, 2, 4).reshape(nb, n_c * REC_CHUNK, HG_HEADS, HG_DV)[:, :t]
    return o, s_fin


def spatial_gating(u, vn, w_s, b_s, start):
    nb, t, _ = vn.shape
    pad_r = (-(start + t)) % MLP_CHUNK
    vp = jnp.pad(vn, ((0, 0), (start, pad_r), (0, 0)))
    n_c = (start + t + pad_r) // MLP_CHUNK
    vp = vp.reshape(nb, n_c, MLP_CHUNK, MLP_GROUPS, MLP_GDIM)
    w = jnp.tril(w_s)
    s = jnp.einsum('gts,bnsgd->bntgd', w, vp) + b_s.T[None, None, :, :, None]
    s = s.reshape(nb, n_c * MLP_CHUNK, MLP_WIDTH)[:, start:start + t]
    return u * s


def grouped_experts(xf, e_flat, w_flat, w_gate, w_up, w_down):
    n, d = xf.shape
    a_n = e_flat.shape[0]
    tok = jnp.repeat(jnp.arange(n, dtype=jnp.int32), TOP_K)
    order = jnp.argsort(e_flat)
    e_s, tok_s, w_s = e_flat[order], tok[order], w_flat[order]
    counts = jnp.zeros((N_EXPERTS,), jnp.int32).at[e_flat].add(1)
    starts = jnp.cumsum(counts) - counts
    padded = (counts + MOE_BLOCK - 1) // MOE_BLOCK * MOE_BLOCK
    pad_end = jnp.cumsum(padded)
    pad_start = pad_end - padded
    dest = pad_start[e_s] + (jnp.arange(a_n, dtype=jnp.int32) - starts[e_s])
    n_blocks = -(-a_n // MOE_BLOCK) + N_EXPERTS
    n_slots = n_blocks * MOE_BLOCK
    slot_tok = jnp.full((n_slots,), n, jnp.int32).at[dest].set(tok_s)
    slot_w = jnp.zeros((n_slots,), xf.dtype).at[dest].set(w_s.astype(xf.dtype))
    block_exp = jnp.minimum(jnp.searchsorted(pad_end, jnp.arange(n_blocks, dtype=jnp.int32) * MOE_BLOCK, side='right'), N_EXPERTS - 1)
    x_pad = jnp.concatenate([xf, jnp.zeros((1, d), xf.dtype)], axis=0)
    xb = x_pad[slot_tok].reshape(n_blocks, MOE_BLOCK, d)

    def expert_block(args):
        xi, ei = args
        hid = jax.nn.silu(xi @ w_gate[ei]) * (xi @ w_up[ei])
        return hid @ w_down[ei]

    yb = lax.map(expert_block, (xb, block_exp))
    y = jax.ops.segment_sum(yb.reshape(n_slots, d) * slot_w[:, None], slot_tok, num_segments=n + 1)
    return y[:n]


def hier_moe(c, w_gr, b_gr, w_er, b_er, w_gate, w_up, w_down):
    nb, t, d = c.shape
    xf = c.reshape(nb * t, d)
    n = xf.shape[0]
    lg = (xf @ w_gr).astype(jnp.float32) + b_gr.astype(jnp.float32)
    pg = jax.nn.softmax(lg, axis=-1)
    _, gidx = lax.top_k(lg, 1)
    le = ((xf @ w_er).astype(jnp.float32) + b_er.astype(jnp.float32)).reshape(n, N_GROUPS, E_PER_GROUP)
    le_g = jnp.take_along_axis(le, gidx[:, :, None], axis=1)[:, 0]
    ev, ei = lax.top_k(le_g, TOP_K)
    gate = jnp.take_along_axis(pg, gidx, axis=1) * jax.nn.softmax(ev, axis=-1)
    expert = gidx * E_PER_GROUP + ei
    y = grouped_experts(xf, expert.reshape(-1).astype(jnp.int32), gate.reshape(-1), w_gate, w_up, w_down)
    return y.reshape(nb, t, d)


def trunk_layer(h, p_i, s0, start, g_mix, w_in, lb, g_head, w_pa, ln_v_g, ln_v_b, w_s, b_s, w_pb, w_o,
                g_ffn, w_gr, b_gr, w_er, b_er, w_gate, w_up, w_down, g_ple, w_pg, w_pp):
    f32 = jnp.float32
    nb, t, _ = h.shape
    a = rmsnorm(h, g_mix)
    z = a @ w_in
    split_points = [int(c) for c in np.cumsum(IN_WIDTHS)[:-1]]
    zq, zf, zi, zo, zu, zv, zga, zgb = jnp.split(z, split_points, axis=-1)
    q = zq.astype(f32).reshape(nb, t, HG_HEADS, HG_DK)
    sf = jax.nn.sigmoid(zf.astype(f32)).reshape(nb, t, HG_HEADS, HG_DK)
    lbh = lb.reshape(HG_HEADS, HG_DK)
    lf = jnp.log(lbh + (1.0 - lbh) * sf)
    k = (1.0 - lbh) * (1.0 - sf)
    v = zi.astype(f32).reshape(nb, t, HG_HEADS, HG_DV)
    o, s_new = hgrn2_recurrence(q, lf, k, v, s0.astype(f32))
    o = o * lax.rsqrt(jnp.mean(o * o, axis=-1, keepdims=True) + EPS) * g_head.astype(f32).reshape(HG_HEADS, HG_DV)
    o = o.reshape(nb, t, HG_WIDTH).astype(h.dtype) * jax.nn.sigmoid(zo)
    y_a = o @ w_pa
    u = jax.nn.gelu(zu)
    vn = layernorm(jax.nn.gelu(zv), ln_v_g, ln_v_b)
    y_b = spatial_gating(u, vn, w_s, b_s, start) @ w_pb
    h = h + (jax.nn.sigmoid(zga) * y_a + jax.nn.sigmoid(zgb) * y_b) @ w_o
    h = h + hier_moe(rmsnorm(h, g_ffn), w_gr, b_gr, w_er, b_er, w_gate, w_up, w_down)
    h = h + jax.nn.sigmoid(rmsnorm(h, g_ple) @ w_pg) * (p_i.astype(h.dtype) @ w_pp)
    n_open = (start + t - 1) % MLP_CHUNK + 1
    keep = min(n_open, t)
    return h, s_new, vn[:, t - keep:]


def setup_inputs(seed: int = 0) -> dict:
    key = jax.random.key(seed)
    ks = jax.random.split(key, 32)
    f32 = jnp.float32

    def nrm(k, shape, scale):
        return jax.random.normal(k, shape, f32) * scale

    def gain(k, shape):
        return 1.0 + 0.01 * jax.random.normal(k, shape, f32)

    return {
        'x_prompt': nrm(ks[0], (BATCH, SEQ, D_MODEL), 1.0),
        'x_sample': nrm(ks[1], (DEC_BATCH, DEC_SEQ, D_MODEL), 1.0),
        'p_prompt': nrm(ks[2], (DEPTH, BATCH, SEQ, PLE_DIM), 1.0),
        'p_sample': nrm(ks[3], (DEPTH, DEC_BATCH, DEC_SEQ, PLE_DIM), 1.0),
        'state_hgrn': nrm(ks[4], (DEPTH, DEC_BATCH, HG_HEADS, HG_DK, HG_DV), 0.5),
        'g_mix': gain(ks[5], (DEPTH, D_MODEL)),
        'w_in': nrm(ks[6], (DEPTH, D_MODEL, IN_WIDTH), D_MODEL ** -0.5),
        'lb_logits': nrm(ks[7], (DEPTH + 1, HG_HEADS * HG_DK), 0.1),
        'g_head': gain(ks[8], (DEPTH, HG_WIDTH)),
        'w_pa': nrm(ks[9], (DEPTH, HG_WIDTH, D_MODEL), HG_WIDTH ** -0.5),
        'ln_v_g': gain(ks[10], (DEPTH, MLP_WIDTH)),
        'ln_v_b': nrm(ks[11], (DEPTH, MLP_WIDTH), 0.01),
        'w_s': nrm(ks[12], (DEPTH, MLP_GROUPS, MLP_CHUNK, MLP_CHUNK), MLP_CHUNK ** -0.5),
        'b_s': gain(ks[13], (DEPTH, MLP_GROUPS, MLP_CHUNK)),
        'w_pb': nrm(ks[14], (DEPTH, MLP_WIDTH, D_MODEL), MLP_WIDTH ** -0.5),
        'w_o': nrm(ks[15], (DEPTH, D_MODEL, D_MODEL), D_MODEL ** -0.5),
        'g_ffn': gain(ks[16], (DEPTH, D_MODEL)),
        'w_gr': nrm(ks[17], (DEPTH, D_MODEL, N_GROUPS), D_MODEL ** -0.5),
        'b_gr': nrm(ks[18], (DEPTH, N_GROUPS), 0.01),
        'w_er': nrm(ks[19], (DEPTH, D_MODEL, N_EXPERTS), D_MODEL ** -0.5),
        'b_er': nrm(ks[20], (DEPTH, N_EXPERTS), 0.01),
        'w_gate': nrm(ks[21], (DEPTH, N_EXPERTS, D_MODEL, D_EXPERT), D_MODEL ** -0.5),
        'w_up': nrm(ks[22], (DEPTH, N_EXPERTS, D_MODEL, D_EXPERT), D_MODEL ** -0.5),
        'w_down': nrm(ks[23], (DEPTH, N_EXPERTS, D_EXPERT, D_MODEL), D_EXPERT ** -0.5),
        'g_ple': gain(ks[24], (DEPTH, D_MODEL)),
        'w_pg': nrm(ks[25], (DEPTH, D_MODEL, D_MODEL), D_MODEL ** -0.5),
        'w_pp': nrm(ks[26], (DEPTH, PLE_DIM, D_MODEL), PLE_DIM ** -0.5),
        'g_final': gain(ks[27], (D_MODEL,)),
    }


def reference(x_prompt, x_sample, p_prompt, p_sample, state_hgrn, g_mix, w_in, lb_logits, g_head, w_pa,
              ln_v_g, ln_v_b, w_s, b_s, w_pb, w_o, g_ffn, w_gr, b_gr, w_er, b_er, w_gate, w_up, w_down,
              g_ple, w_pg, w_pp, g_final):
    lbs = jnp.cumsum(jax.nn.softmax(lb_logits.astype(jnp.float32), axis=0), axis=0)
    hp, hs = x_prompt, x_sample
    sp_list, ss_list, vp_list, vs_list = [], [], [], []
    s0_prompt = jnp.zeros((x_prompt.shape[0], HG_HEADS, HG_DK, HG_DV), jnp.float32)
    sample_start = PAST_LEN % MLP_CHUNK
    for i in range(DEPTH):
        lw = (g_mix[i], w_in[i], lbs[i], g_head[i], w_pa[i], ln_v_g[i], ln_v_b[i], w_s[i], b_s[i], w_pb[i],
              w_o[i], g_ffn[i], w_gr[i], b_gr[i], w_er[i], b_er[i], w_gate[i], w_up[i], w_down[i],
              g_ple[i], w_pg[i], w_pp[i])
        hp, sp, vp = trunk_layer(hp, p_prompt[i], s0_prompt, 0, *lw)
        hs, ss, vs = trunk_layer(hs, p_sample[i], state_hgrn[i], sample_start, *lw)
        sp_list.append(sp.astype(x_prompt.dtype))
        ss_list.append(ss.astype(state_hgrn.dtype))
        vp_list.append(vp)
        vs_list.append(vs)
    y_prompt = rmsnorm(hp, g_final)
    y_sample = rmsnorm(hs, g_final)
    state_hgrn_prompt = jnp.stack(sp_list)
    state_hgrn_sample = jnp.stack(ss_list)
    mlp_v_prompt = jnp.stack(vp_list)
    mlp_v_sample = jnp.stack(vs_list)
    return (y_prompt, y_sample, state_hgrn_prompt, state_hgrn_sample, mlp_v_prompt, mlp_v_sample)
```

```python
import functools
import math

import jax
import jax.numpy as jnp
from jax import lax
from jax.experimental import pallas as pl
from jax.experimental.pallas import tpu as pltpu

F32 = jnp.float32
BF16 = jnp.bfloat16

EPS = 1e-6
HEAD_DIM = 128
REC_CHUNK = 64
REC_HALF = 32
MLP_CHUNK = 128
MLP_GROUPS = 4
N_GROUPS = 4
E_PER_GROUP = 8
N_EXPERTS = N_GROUPS * E_PER_GROUP
TOP_K = 2
MOE_BLOCK = 128
PAST_LEN = 2048
LANES = 128
ROUTER_ROWS = 40
VMEM_LIMIT = 56 * 1024 * 1024


def _pick(n, candidates):
    for c in candidates:
        if n % c == 0:
            return c
    raise ValueError(f"no tile in {candidates} divides {n}")


def _dot(a, b):
    return jnp.dot(a, b, preferred_element_type=F32)


def _dot_nt(a, b, precision=None):
    return lax.dot_general(a, b, (((1,), (1,)), ((), ())), precision=precision, preferred_element_type=F32)


def _dot_tn(a, b):
    return lax.dot_general(a, b, (((0,), (0,)), ((), ())), preferred_element_type=F32)


def _rms(x, g):
    return x * lax.rsqrt(jnp.mean(x * x, axis=-1, keepdims=True) + EPS) * g


def _gelu(x):
    c = math.sqrt(2.0 / math.pi)
    return x * (0.5 * (1.0 + jnp.tanh(c * (x + 0.044715 * (x * x * x)))))


def _in_proj_body(x_ref, g_ref, w_ref, z_ref, xn_ref):
    @pl.when(pl.program_id(1) == 0)
    def _():
        xn_ref[...] = _rms(x_ref[...], g_ref[...]).astype(BF16)

    z_ref[...] = _dot(xn_ref[...], w_ref[...]).astype(z_ref.dtype)


def _in_proj(h, g, w):
    n, d = h.shape
    width = w.shape[1]
    tm = _pick(n, (1024, 512, 256))
    tn = _pick(width, (1024, 512, 256, 128))
    return pl.pallas_call(
        _in_proj_body,
        grid=(n // tm, width // tn),
        in_specs=[pl.BlockSpec((tm, d), lambda i, j: (i, 0)),
                  pl.BlockSpec((1, d), lambda i, j: (0, 0)),
                  pl.BlockSpec((d, tn), lambda i, j: (0, j))],
        out_specs=pl.BlockSpec((tm, tn), lambda i, j: (i, j)),
        out_shape=jax.ShapeDtypeStruct((n, width), BF16),
        scratch_shapes=[pltpu.VMEM((tm, d), BF16)],
        compiler_params=pltpu.CompilerParams(dimension_semantics=("parallel", "arbitrary"),
                                             vmem_limit_bytes=VMEM_LIMIT),
        name="in_proj",
    )(h, g, w)


def _cumsum_rows(x, row):
    d = 1
    while d < x.shape[0]:
        x = x + jnp.where(row >= d, pltpu.roll(x, d, 0), 0.0)
        d *= 2
    return x


def _hgrn_body(q_ref, f_ref, i_ref, zo_ref, s0_ref, lb_ref, gh_ref, og_ref, sout_ref, st_ref,
               *, chunk, n_chunks, heads):
    ti = pl.program_id(2)

    @pl.when(ti == 0)
    def _():
        for h in range(heads):
            st_ref[h] = s0_ref[0, h].T

    half = min(REC_HALF, chunk)
    n_half = chunk // half
    row = lax.broadcasted_iota(jnp.int32, (chunk, HEAD_DIM), 0)
    tri = (lax.broadcasted_iota(jnp.int32, (half, half), 0) >= lax.broadcasted_iota(jnp.int32, (half, half), 1))

    def chunk_step(c, carry):
        r0 = pl.multiple_of(c * chunk, chunk)
        for h in range(heads):
            hs = slice(h * HEAD_DIM, (h + 1) * HEAD_DIM)
            q = q_ref[pl.ds(r0, chunk), hs].astype(F32)
            zf = f_ref[pl.ds(r0, chunk), hs].astype(F32)
            v = i_ref[pl.ds(r0, chunk), hs]
            zo = zo_ref[pl.ds(r0, chunk), hs].astype(F32)
            lb = lb_ref[:, hs]
            sf = jax.nn.sigmoid(zf)
            lf = jnp.log(lb + (1.0 - lb) * sf)
            kk = (1.0 - lb) * (1.0 - sf)
            b = _cumsum_rows(lf, row)
            b_last = b[chunk - 1:chunk, :]
            st = st_ref[h]
            o_inter = _dot_nt((q * jnp.exp(b)).astype(BF16), st.astype(BF16))
            outs = []
            for g in range(n_half):
                rg = slice(g * half, (g + 1) * half)
                mid = b[g * half + half // 2 - 1:g * half + half // 2, :]
                qs = (q[rg] * jnp.exp(b[rg] - mid)).astype(BF16)
                ks = (kk[rg] * jnp.exp(mid - b[rg])).astype(BF16)
                a = jnp.where(tri, _dot_nt(qs, ks), 0.0)
                o_g = o_inter[rg] + _dot(a.astype(BF16), v[rg])
                for gp in range(g):
                    rp = slice(gp * half, (gp + 1) * half)
                    edge = b[g * half - 1:g * half, :]
                    qc = (q[rg] * jnp.exp(b[rg] - edge)).astype(BF16)
                    kc = (kk[rp] * jnp.exp(edge - b[rp])).astype(BF16)
                    o_g = o_g + _dot(_dot_nt(qc, kc).astype(BF16), v[rp])
                outs.append(o_g)
            kdec = (kk * jnp.exp(b_last - b)).astype(BF16)
            st_ref[h] = st * jnp.exp(b_last) + _dot_tn(v, kdec)
            gh = gh_ref[:, hs]
            sig_o = jax.nn.sigmoid(zo)
            for g in range(n_half):
                o_g = outs[g]
                o_n = o_g * lax.rsqrt(jnp.mean(o_g * o_g, axis=-1, keepdims=True) + EPS) * gh
                og_ref[pl.ds(r0 + g * half, half), hs] = (
                    o_n * sig_o[g * half:(g + 1) * half]).astype(og_ref.dtype)
        return carry

    lax.fori_loop(0, n_chunks, chunk_step, 0)

    @pl.when(ti == pl.num_programs(2) - 1)
    def _():
        for h in range(heads):
            sout_ref[0, h] = st_ref[h].T


def _hgrn(z, s0, lb, g_head, *, row_base, batch, seq, d_model, name):
    n_heads = d_model // HEAD_DIM
    heads = min(4, n_heads)
    chunk = min(REC_CHUNK, seq)
    assert seq % chunk == 0 and chunk % min(REC_HALF, chunk) == 0
    rt = _pick(seq, (512, 256, 128, 64, 32))
    rt = max(rt, chunk)
    assert row_base % rt == 0
    tiles = seq // rt
    wblk = heads * HEAD_DIM
    cpb = d_model // wblk

    def zspec(section):
        return pl.BlockSpec((rt, wblk),
                            lambda b, hg, i: (row_base // rt + b * tiles + i, section * cpb + hg))

    return pl.pallas_call(
        functools.partial(_hgrn_body, chunk=chunk, n_chunks=rt // chunk, heads=heads),
        grid=(batch, n_heads // heads, tiles),
        in_specs=[zspec(0), zspec(1), zspec(2), zspec(3),
                  pl.BlockSpec((1, heads, HEAD_DIM, HEAD_DIM), lambda b, hg, i: (b, hg, 0, 0)),
                  pl.BlockSpec((1, wblk), lambda b, hg, i: (0, hg)),
                  pl.BlockSpec((1, wblk), lambda b, hg, i: (0, hg))],
        out_specs=[pl.BlockSpec((rt, wblk), lambda b, hg, i: (b * tiles + i, hg)),
                   pl.BlockSpec((1, heads, HEAD_DIM, HEAD_DIM), lambda b, hg, i: (b, hg, 0, 0))],
        out_shape=[jax.ShapeDtypeStruct((batch * seq, d_model), BF16),
                   jax.ShapeDtypeStruct((batch, n_heads, HEAD_DIM, HEAD_DIM), F32)],
        scratch_shapes=[pltpu.VMEM((heads, HEAD_DIM, HEAD_DIM), F32)],
        compiler_params=pltpu.CompilerParams(dimension_semantics=("parallel", "parallel", "arbitrary"),
                                             vmem_limit_bytes=VMEM_LIMIT),
        name=name,
    )(z, z, z, z, s0, lb, g_head)


def _mix_body(zu_ref, zv_ref, zga_ref, zgb_ref, ogp_ref, ogs_ref, x_ref, lng_ref, lnb_ref,
              wsp_ref, wss_ref, bsp_ref, bss_ref, wpa_ref, wpb_ref, wo_ref,
              h1_ref, vn_ref, sg_ref, *, n_prompt_tiles, dec_seq):
    is_p = pl.program_id(0) < n_prompt_tiles
    tm, width = zu_ref.shape
    gd = width // MLP_GROUPS
    u = _gelu(zu_ref[...].astype(F32))
    gv = _gelu(zv_ref[...].astype(F32))
    xc = gv - jnp.mean(gv, axis=-1, keepdims=True)
    vn = xc * lax.rsqrt(jnp.mean(xc * xc, axis=-1, keepdims=True) + EPS) * lng_ref[...] + lnb_ref[...]
    vn_ref[...] = vn
    vnb = vn.astype(BF16)
    r = lax.broadcasted_iota(jnp.int32, (MLP_CHUNK, MLP_CHUNK), 0)
    c = lax.broadcasted_iota(jnp.int32, (MLP_CHUNK, MLP_CHUNK), 1)
    causal = r >= c
    same_stream = (r // dec_seq) == (c // dec_seq)
    for g in range(MLP_GROUPS):
        w_p = jnp.where(causal, wsp_ref[g], 0.0)
        w_s = jnp.where(causal & same_stream, wss_ref[g], 0.0)
        w = jnp.where(is_p, w_p, w_s).astype(BF16)
        bias = jnp.where(is_p, bsp_ref[g], bss_ref[g])
        for cc in range(tm // MLP_CHUNK):
            rows = slice(cc * MLP_CHUNK, (cc + 1) * MLP_CHUNK)
            cols = slice(g * gd, (g + 1) * gd)
            s = _dot(w, vnb[rows, cols]) + bias
            sg_ref[rows, cols] = (u[rows, cols] * s).astype(BF16)
    y_b = _dot(sg_ref[...], wpb_ref[...])
    og = jnp.where(is_p, ogp_ref[...].astype(F32), ogs_ref[...].astype(F32)).astype(BF16)
    y_a = _dot(og, wpa_ref[...])
    m = (jax.nn.sigmoid(zga_ref[...].astype(F32)) * y_a
         + jax.nn.sigmoid(zgb_ref[...].astype(F32)) * y_b)
    h1_ref[...] = x_ref[...] + _dot(m.astype(BF16), wo_ref[...])


def _mix(z, og_p, og_s, x, ln_g, ln_b, ws_p, ws_s, bs_p, bs_s, w_pa, w_pb, w_o, *, n_prompt, dec_seq):
    n, d = x.shape
    width = w_pb.shape[0]
    tm = 256
    assert n % tm == 0 and n_prompt % tm == 0 and tm % MLP_CHUNK == 0 and MLP_CHUNK % dec_seq == 0
    npt = n_prompt // tm
    nst = og_s.shape[0] // tm
    u_blk = 4 * d // width
    ga_blk = (4 * d + 2 * width) // d
    const2 = lambda i: (0, 0)
    const3 = lambda i: (0, 0, 0)
    return pl.pallas_call(
        functools.partial(_mix_body, n_prompt_tiles=npt, dec_seq=dec_seq),
        grid=(n // tm,),
        in_specs=[pl.BlockSpec((tm, width), lambda i: (i, u_blk)),
                  pl.BlockSpec((tm, width), lambda i: (i, u_blk + 1)),
                  pl.BlockSpec((tm, d), lambda i: (i, ga_blk)),
                  pl.BlockSpec((tm, d), lambda i: (i, ga_blk + 1)),
                  pl.BlockSpec((tm, d), lambda i: (jnp.minimum(i, npt - 1), 0)),
                  pl.BlockSpec((tm, d), lambda i: (jnp.clip(i - npt, 0, nst - 1), 0)),
                  pl.BlockSpec((tm, d), lambda i: (i, 0)),
                  pl.BlockSpec((1, width), const2),
                  pl.BlockSpec((1, width), const2),
                  pl.BlockSpec((MLP_GROUPS, MLP_CHUNK, MLP_CHUNK), const3),
                  pl.BlockSpec((MLP_GROUPS, MLP_CHUNK, MLP_CHUNK), const3),
                  pl.BlockSpec((MLP_GROUPS, MLP_CHUNK, 1), const3),
                  pl.BlockSpec((MLP_GROUPS, MLP_CHUNK, 1), const3),
                  pl.BlockSpec((d, d), const2, pipeline_mode=pl.Buffered(1)),
                  pl.BlockSpec((width, d), const2, pipeline_mode=pl.Buffered(1)),
                  pl.BlockSpec((d, d), const2, pipeline_mode=pl.Buffered(1))],
        out_specs=[pl.BlockSpec((tm, d), lambda i: (i, 0)),
                   pl.BlockSpec((tm, width), lambda i: (i, 0))],
        out_shape=[jax.ShapeDtypeStruct((n, d), F32),
                   jax.ShapeDtypeStruct((n, width), F32)],
        scratch_shapes=[pltpu.VMEM((tm, width), BF16)],
        compiler_params=pltpu.CompilerParams(dimension_semantics=("parallel",),
                                             vmem_limit_bytes=VMEM_LIMIT),
        name="mix",
    )(z, z, z, z, og_p, og_s, x, ln_g, ln_b, ws_p, ws_s, bs_p, bs_s, w_pa, w_pb, w_o)


def _router_body(h_ref, g_ref, wr_ref, br_ref, c_ref, idx_ref, gate_ref, cnt_ref, carry_ref):
    @pl.when(pl.program_id(0) == 0)
    def _():
        carry_ref[...] = jnp.zeros_like(carry_ref)

    tm = h_ref.shape[0]
    c = _rms(h_ref[...], g_ref[...])
    c_ref[...] = c
    lt = _dot_nt(wr_ref[...], c, precision=lax.Precision.HIGHEST) + br_ref[...]
    le = lt[0:N_EXPERTS]
    lg = lt[N_EXPERTS:N_EXPERTS + N_GROUPS]
    gmax = jnp.max(lg, axis=0, keepdims=True)
    p_sel = 1.0 / jnp.sum(jnp.exp(lg - gmax), axis=0, keepdims=True)
    best = lg[0:1]
    gi = jnp.zeros((1, tm), jnp.int32)
    for g in range(1, N_GROUPS):
        better = lg[g:g + 1] > best
        gi = jnp.where(better, g, gi)
        best = jnp.where(better, lg[g:g + 1], best)
    leg = jnp.zeros((E_PER_GROUP, tm), F32)
    for g in range(N_GROUPS):
        leg = jnp.where(gi == g, le[g * E_PER_GROUP:(g + 1) * E_PER_GROUP], leg)
    sub = lax.broadcasted_iota(jnp.int32, (E_PER_GROUP, tm), 0).astype(F32)
    v1 = jnp.max(leg, axis=0, keepdims=True)
    i1 = jnp.min(jnp.where(leg == v1, sub, float(E_PER_GROUP)), axis=0, keepdims=True)
    rest = jnp.where(sub == i1, -jnp.inf, leg)
    v2 = jnp.max(rest, axis=0, keepdims=True)
    i2 = jnp.min(jnp.where(rest == v2, sub, float(E_PER_GROUP)), axis=0, keepdims=True)
    e2 = jnp.exp(v2 - v1)
    den = 1.0 + e2
    gate0 = p_sel * (1.0 / den)
    gate1 = p_sel * (e2 / den)
    ex0 = gi * E_PER_GROUP + i1.astype(jnp.int32)
    ex1 = gi * E_PER_GROUP + i2.astype(jnp.int32)
    eid = lax.broadcasted_iota(jnp.int32, (N_EXPERTS, tm), 0)
    oh0 = eid == ex0
    oh1 = eid == ex1
    oh = jnp.where(oh0 | oh1, 1.0, 0.0)
    upper = jnp.where(lax.broadcasted_iota(jnp.int32, (tm, tm), 0) < lax.broadcasted_iota(jnp.int32, (tm, tm), 1),
                      1.0, 0.0).astype(BF16)
    before = _dot(oh.astype(BF16), upper) + carry_ref[:, 0:1]
    rank0 = jnp.sum(jnp.where(oh0, before, 0.0), axis=0, keepdims=True)
    rank1 = jnp.sum(jnp.where(oh1, before, 0.0), axis=0, keepdims=True)
    carry = carry_ref[...] + jnp.sum(oh, axis=1, keepdims=True)
    carry_ref[...] = carry
    cnt_ref[...] = carry.astype(jnp.int32)
    idx_ref[...] = jnp.zeros_like(idx_ref)
    idx_ref[0:1, :] = ex0
    idx_ref[1:2, :] = ex1
    idx_ref[2:3, :] = rank0.astype(jnp.int32)
    idx_ref[3:4, :] = rank1.astype(jnp.int32)
    gate_ref[...] = jnp.zeros_like(gate_ref)
    gate_ref[0:1, :] = gate0
    gate_ref[1:2, :] = gate1


def _router(h1, g_ffn, wr, br):
    n, d = h1.shape
    tm = _pick(n, (512, 256))
    return pl.pallas_call(
        _router_body,
        grid=(n // tm,),
        in_specs=[pl.BlockSpec((tm, d), lambda i: (i, 0)),
                  pl.BlockSpec((1, d), lambda i: (0, 0)),
                  pl.BlockSpec((ROUTER_ROWS, d), lambda i: (0, 0)),
                  pl.BlockSpec((ROUTER_ROWS, 1), lambda i: (0, 0))],
        out_specs=[pl.BlockSpec((tm, d), lambda i: (i, 0)),
                   pl.BlockSpec((8, tm), lambda i: (0, i)),
                   pl.BlockSpec((8, tm), lambda i: (0, i)),
                   pl.BlockSpec((N_EXPERTS, LANES), lambda i: (0, 0))],
        out_shape=[jax.ShapeDtypeStruct((n, d), F32),
                   jax.ShapeDtypeStruct((8, n), jnp.int32),
                   jax.ShapeDtypeStruct((8, n), F32),
                   jax.ShapeDtypeStruct((N_EXPERTS, LANES), jnp.int32)],
        scratch_shapes=[pltpu.VMEM((N_EXPERTS, LANES), F32)],
        compiler_params=pltpu.CompilerParams(dimension_semantics=("arbitrary",),
                                             vmem_limit_bytes=VMEM_LIMIT),
        name="router",
    )(h1, g_ffn, wr, br)


def _row_copy(src_ref, src_row, dst_ref, dst_row, sem):
    return pltpu.make_async_copy(src_ref.at[pl.ds(src_row, 1)], dst_ref.at[pl.ds(dst_row, 1)], sem)


def _dispatch_body(ps_ref, idx_ref, c_ref, xs_in_ref, xs_ref, sem):
    del xs_in_ref
    td = c_ref.shape[0]

    def issue(t, carry):
        for k in range(TOP_K):
            dest = ps_ref[idx_ref[k, t]] + idx_ref[TOP_K + k, t]
            _row_copy(c_ref, t, xs_ref, dest, sem).start()
        return carry

    lax.fori_loop(0, td, issue, 0)

    def drain(t, carry):
        for k in range(TOP_K):
            _row_copy(c_ref, 0, xs_ref, 0, sem).wait()
        return carry

    lax.fori_loop(0, td, drain, 0)


def _dispatch(pad_start, idx, c, xs_zero):
    n, d = c.shape
    td = _pick(n, (256,))
    grid_spec = pltpu.PrefetchScalarGridSpec(
        num_scalar_prefetch=1,
        grid=(n // td,),
        in_specs=[pl.BlockSpec((8, td), lambda i, ps: (0, i), memory_space=pltpu.SMEM),
                  pl.BlockSpec((td, d), lambda i, ps: (i, 0)),
                  pl.BlockSpec(memory_space=pl.ANY)],
        out_specs=pl.BlockSpec(memory_space=pl.ANY),
        scratch_shapes=[pltpu.SemaphoreType.DMA(())],
    )
    return pl.pallas_call(
        _dispatch_body,
        grid_spec=grid_spec,
        out_shape=jax.ShapeDtypeStruct(xs_zero.shape, xs_zero.dtype),
        input_output_aliases={3: 0},
        compiler_params=pltpu.CompilerParams(dimension_semantics=("arbitrary",),
                                             vmem_limit_bytes=VMEM_LIMIT),
        name="dispatch",
    )(pad_start, idx, c, xs_zero)


def _expert_body(be_ref, na_ref, xs_ref, wg_ref, wu_ref, wd_ref, yb_ref):
    del be_ref
    active = pl.program_id(0) < na_ref[0]

    @pl.when(active)
    def _():
        x = xs_ref[...].astype(BF16)
        a = _dot(x, wg_ref[0])
        b = _dot(x, wu_ref[0])
        hid = (a * jax.nn.sigmoid(a)) * b
        yb_ref[...] = _dot(hid.astype(BF16), wd_ref[0])

    @pl.when(jnp.logical_not(active))
    def _():
        yb_ref[...] = jnp.zeros_like(yb_ref)


def _experts(block_exp, n_active, xs, w_gate, w_up, w_down):
    n_slots, d = xs.shape
    de = w_gate.shape[2]
    nb = n_slots // MOE_BLOCK
    grid_spec = pltpu.PrefetchScalarGridSpec(
        num_scalar_prefetch=2,
        grid=(nb,),
        in_specs=[pl.BlockSpec((MOE_BLOCK, d), lambda j, be, na: (j, 0)),
                  pl.BlockSpec((1, d, de), lambda j, be, na: (be[j], 0, 0)),
                  pl.BlockSpec((1, d, de), lambda j, be, na: (be[j], 0, 0)),
                  pl.BlockSpec((1, de, d), lambda j, be, na: (be[j], 0, 0))],
        out_specs=pl.BlockSpec((MOE_BLOCK, d), lambda j, be, na: (j, 0)),
    )
    return pl.pallas_call(
        _expert_body,
        grid_spec=grid_spec,
        out_shape=jax.ShapeDtypeStruct((n_slots, d), F32),
        compiler_params=pltpu.CompilerParams(dimension_semantics=("arbitrary",),
                                             vmem_limit_bytes=VMEM_LIMIT),
        name="experts",
    )(block_exp, n_active, xs, w_gate, w_up, w_down)


def _combine_body(ps_ref, idx_ref, gate_ref, h1_ref, p_ref, yb_ref, gple_ref, wpg_ref, wpp_ref, gfin_ref,
                  y_ref, buf_ref, sem, *, final_norm):
    tc = h1_ref.shape[0]

    def issue(t, carry):
        for k in range(TOP_K):
            src = ps_ref[idx_ref[k, t]] + idx_ref[TOP_K + k, t]
            _row_copy(yb_ref, src, buf_ref.at[k], t, sem).start()
        return carry

    lax.fori_loop(0, tc, issue, 0)

    def drain(t, carry):
        for k in range(TOP_K):
            _row_copy(yb_ref, 0, buf_ref.at[k], 0, sem).wait()
        return carry

    lax.fori_loop(0, tc, drain, 0)

    gt = gate_ref[...].T
    h2 = h1_ref[...] + (gt[:, 0:1] * buf_ref[0] + gt[:, 1:2] * buf_ref[1])
    a = _rms(h2, gple_ref[...]).astype(BF16)
    h3 = h2 + jax.nn.sigmoid(_dot(a, wpg_ref[...])) * _dot(p_ref[...].astype(BF16), wpp_ref[...])
    if final_norm:
        h3 = _rms(h3, gfin_ref[...])
    y_ref[...] = h3


def _combine(pad_start, idx, gates, h1, p, yb, g_ple, w_pg, w_pp, g_final, *, final_norm):
    n, d = h1.shape
    ple = p.shape[1]
    tc = _pick(n, (256,))
    const2 = lambda i, ps: (0, 0)
    grid_spec = pltpu.PrefetchScalarGridSpec(
        num_scalar_prefetch=1,
        grid=(n // tc,),
        in_specs=[pl.BlockSpec((8, tc), lambda i, ps: (0, i), memory_space=pltpu.SMEM),
                  pl.BlockSpec((8, tc), lambda i, ps: (0, i)),
                  pl.BlockSpec((tc, d), lambda i, ps: (i, 0)),
                  pl.BlockSpec((tc, ple), lambda i, ps: (i, 0)),
                  pl.BlockSpec(memory_space=pl.ANY),
                  pl.BlockSpec((1, d), const2),
                  pl.BlockSpec((d, d), const2, pipeline_mode=pl.Buffered(1)),
                  pl.BlockSpec((ple, d), const2, pipeline_mode=pl.Buffered(1)),
                  pl.BlockSpec((1, d), const2)],
        out_specs=pl.BlockSpec((tc, d), lambda i, ps: (i, 0)),
        scratch_shapes=[pltpu.VMEM((TOP_K, tc, d), F32), pltpu.SemaphoreType.DMA(())],
    )
    return pl.pallas_call(
        functools.partial(_combine_body, final_norm=final_norm),
        grid_spec=grid_spec,
        out_shape=jax.ShapeDtypeStruct((n, d), F32),
        compiler_params=pltpu.CompilerParams(dimension_semantics=("arbitrary",),
                                             vmem_limit_bytes=VMEM_LIMIT),
        name="combine",
    )(pad_start, idx, gates, h1, p, yb, g_ple, w_pg, w_pp, g_final)


def _layer(h, p, s0_sample, lb, n_prompt, batch, seq, dec_batch, dec_seq,
           g_mix, w_in, g_head, w_pa, ln_v_g, ln_v_b, w_s, b_s, w_pb, w_o,
           g_ffn, w_gr, b_gr, w_er, b_er, w_gate, w_up, w_down, g_ple, w_pg, w_pp, g_final, final_norm):
    n, d = h.shape
    n_heads = d // HEAD_DIM
    row = lambda a: a.reshape(1, -1).astype(F32)

    z = _in_proj(h, row(g_mix), w_in.astype(BF16))

    lb_row, gh_row = row(lb), row(g_head)
    s0_prompt = jnp.zeros((batch, n_heads, HEAD_DIM, HEAD_DIM), F32)
    og_p, st_p = _hgrn(z, s0_prompt, lb_row, gh_row, row_base=0, batch=batch, seq=seq,
                       d_model=d, name="hgrn_prompt")
    og_s, st_s = _hgrn(z, s0_sample.astype(F32), lb_row, gh_row, row_base=n_prompt, batch=dec_batch,
                       seq=dec_seq, d_model=d, name="hgrn_sample")

    start = PAST_LEN % MLP_CHUNK
    assert start + dec_seq <= MLP_CHUNK
    rep = MLP_CHUNK // dec_seq
    ws_s = jnp.tile(w_s[:, start:start + dec_seq, start:start + dec_seq], (1, rep, rep))
    bs_s = jnp.tile(b_s[:, start:start + dec_seq], (1, rep))
    h1, vn = _mix(z, og_p, og_s, h, row(ln_v_g), row(ln_v_b), w_s, ws_s, b_s[..., None], bs_s[..., None],
                  w_pa.astype(BF16), w_pb.astype(BF16), w_o.astype(BF16), n_prompt=n_prompt, dec_seq=dec_seq)

    pad_rows = ROUTER_ROWS - N_EXPERTS - N_GROUPS
    wr = jnp.concatenate([w_er.T, w_gr.T, jnp.zeros((pad_rows, d), F32)], axis=0)
    br = jnp.concatenate([b_er, b_gr, jnp.zeros((pad_rows,), F32)]).reshape(ROUTER_ROWS, 1).astype(F32)
    c, idx, gates, cnt = _router(h1, row(g_ffn), wr, br)

    counts = cnt[:, 0]
    padded = (counts + MOE_BLOCK - 1) // MOE_BLOCK * MOE_BLOCK
    pad_end = jnp.cumsum(padded)
    pad_start = (pad_end - padded).astype(jnp.int32)
    n_blocks = -(-(n * TOP_K) // MOE_BLOCK) + N_EXPERTS
    block_first = jnp.arange(n_blocks, dtype=jnp.int32) * MOE_BLOCK
    block_exp = jnp.minimum(jnp.sum(pad_end[None, :] <= block_first[:, None], axis=1), N_EXPERTS - 1).astype(jnp.int32)
    n_active = (pad_end[-1:] // MOE_BLOCK).astype(jnp.int32)

    xs = _dispatch(pad_start, idx, c, jnp.zeros((n_blocks * MOE_BLOCK, d), F32))
    yb = _experts(block_exp, n_active, xs, w_gate.astype(BF16), w_up.astype(BF16), w_down.astype(BF16))
    h3 = _combine(pad_start, idx, gates, h1, p, yb, row(g_ple), w_pg.astype(BF16), w_pp.astype(BF16),
                  row(g_final), final_norm=final_norm)
    return h3, st_p, st_s, vn


def kernel(x_prompt, x_sample, p_prompt, p_sample, state_hgrn, g_mix, w_in, lb_logits, g_head, w_pa, ln_v_g, ln_v_b, w_s, b_s, w_pb, w_o, g_ffn, w_gr, b_gr, w_er, b_er, w_gate, w_up, w_down, g_ple, w_pg, w_pp, g_final):
    batch, seq, d = x_prompt.shape
    dec_batch, dec_seq, _ = x_sample.shape
    depth = g_mix.shape[0]
    n_prompt = batch * seq
    width = w_pb.shape[1]
    lbs = jnp.cumsum(jax.nn.softmax(lb_logits.astype(F32), axis=0), axis=0)
    h = jnp.concatenate([x_prompt.reshape(n_prompt, d), x_sample.reshape(dec_batch * dec_seq, d)], axis=0)
    sp, ss, vp, vs = [], [], [], []
    for i in range(depth):
        p = jnp.concatenate([p_prompt[i].reshape(n_prompt, -1), p_sample[i].reshape(dec_batch * dec_seq, -1)], axis=0)
        h, st_p, st_s, vn = _layer(
            h, p, state_hgrn[i], lbs[i], n_prompt, batch, seq, dec_batch, dec_seq,
            g_mix[i], w_in[i], g_head[i], w_pa[i], ln_v_g[i], ln_v_b[i], w_s[i], b_s[i], w_pb[i], w_o[i],
            g_ffn[i], w_gr[i], b_gr[i], w_er[i], b_er[i], w_gate[i], w_up[i], w_down[i],
            g_ple[i], w_pg[i], w_pp[i], g_final, i == depth - 1)
        keep = min((seq - 1) % MLP_CHUNK + 1, seq)
        keep_s = min((PAST_LEN % MLP_CHUNK + dec_seq - 1) % MLP_CHUNK + 1, dec_seq)
        sp.append(st_p.astype(x_prompt.dtype))
        ss.append(st_s.astype(state_hgrn.dtype))
        vp.append(vn[:n_prompt].reshape(batch, seq, width)[:, seq - keep:])
        vs.append(vn[n_prompt:].reshape(dec_batch, dec_seq, width)[:, dec_seq - keep_s:])
    y_prompt = h[:n_prompt].reshape(batch, seq, d)
    y_sample = h[n_prompt:].reshape(dec_batch, dec_seq, d)
    return (y_prompt, y_sample, jnp.stack(sp), jnp.stack(ss), jnp.stack(vp), jnp.stack(vs))
```

```python
import functools
import math

import jax
import jax.numpy as jnp
from jax import lax
from jax.experimental import pallas as pl
from jax.experimental.pallas import tpu as pltpu

F32 = jnp.float32
BF16 = jnp.bfloat16

EPS = 1e-6
HEAD_DIM = 128
REC_CHUNK = 64
REC_HALF = 32
MLP_CHUNK = 128
MLP_GROUPS = 4
N_GROUPS = 4
E_PER_GROUP = 8
N_EXPERTS = N_GROUPS * E_PER_GROUP
TOP_K = 2
MOE_BLOCK = 128
PAST_LEN = 2048
LANES = 128
SUBLANES = 8
ROUTER_ROWS = 40
VMEM_LIMIT = 56 * 1024 * 1024
LOG2_E = 1.4426950408889634


def _pick(n, candidates):
    for c in candidates:
        if n % c == 0:
            return c
    raise ValueError(f"no tile in {candidates} divides {n}")


def _dot(a, b):
    return jnp.dot(a, b, preferred_element_type=F32)


def _dot_nt(a, b, precision=None):
    return lax.dot_general(a, b, (((1,), (1,)), ((), ())), precision=precision, preferred_element_type=F32)


def _dot_tn(a, b):
    return lax.dot_general(a, b, (((0,), (0,)), ((), ())), preferred_element_type=F32)


def _rms(x, g):
    return x * lax.rsqrt(jnp.mean(x * x, axis=-1, keepdims=True) + EPS) * g


def _gelu(x):
    c = math.sqrt(2.0 / math.pi)
    return x * (0.5 * (1.0 + jnp.tanh(c * (x + 0.044715 * (x * x * x)))))


def _sigmoid(x):
    return 0.5 * jnp.tanh(0.5 * x) + 0.5


def _in_proj_body(xp_ref, xs_ref, g_ref, w_ref, z_ref, xn_ref, *, n_prompt_tiles):
    first = pl.program_id(1) == 0
    is_p = pl.program_id(0) < n_prompt_tiles

    @pl.when(first & is_p)
    def _():
        xn_ref[...] = _rms(xp_ref[...], g_ref[...]).astype(BF16)

    @pl.when(first & jnp.logical_not(is_p))
    def _():
        xn_ref[...] = _rms(xs_ref[...], g_ref[...]).astype(BF16)

    z_ref[...] = _dot(xn_ref[...], w_ref[...]).astype(z_ref.dtype)


def _in_proj(xp, xs, g, w):
    (n_p, d), n_s = xp.shape, xs.shape[0]
    width = w.shape[1]
    tm = _pick(math.gcd(n_p, n_s), (1024, 512, 256))
    tn = _pick(width, (1024, 512, 256, 128))
    npt, nst = n_p // tm, n_s // tm
    return pl.pallas_call(
        functools.partial(_in_proj_body, n_prompt_tiles=npt),
        grid=(npt + nst, width // tn),
        in_specs=[pl.BlockSpec((tm, d), lambda i, j: (jnp.minimum(i, npt - 1), 0)),
                  pl.BlockSpec((tm, d), lambda i, j: (jnp.clip(i - npt, 0, nst - 1), 0)),
                  pl.BlockSpec((1, d), lambda i, j: (0, 0)),
                  pl.BlockSpec((d, tn), lambda i, j: (0, j))],
        out_specs=pl.BlockSpec((tm, tn), lambda i, j: (i, j)),
        out_shape=jax.ShapeDtypeStruct((n_p + n_s, width), BF16),
        scratch_shapes=[pltpu.VMEM((tm, d), BF16)],
        compiler_params=pltpu.CompilerParams(dimension_semantics=("parallel", "arbitrary"),
                                             vmem_limit_bytes=VMEM_LIMIT),
        name="in_proj",
    )(xp, xs, g, w)


def _cumsum_rows(x, row):
    d = 1
    while d < x.shape[0]:
        x = x + jnp.where(row >= d, pltpu.roll(x, d, 0), 0.0)
        d *= 2
    return x


def _hgrn_body(q_ref, f_ref, i_ref, zo_ref, s0_ref, lb_ref, gh_ref, og_ref, sout_ref,
               st_ref, qs_ref, qe_ref, kd_ref, dec_ref, a_ref, o_ref, *key_refs, chunk, n_chunks, heads):
    ti = pl.program_id(2)

    @pl.when(ti == 0)
    def _():
        for h in range(heads):
            st_ref[h] = s0_ref[0, h].T

    half = min(REC_HALF, chunk)
    n_half = chunk // half
    row = lax.broadcasted_iota(jnp.int32, (chunk, HEAD_DIM), 0)
    masks = []
    for g in range(n_half):
        r = lax.broadcasted_iota(jnp.int32, (half, half * (g + 1)), 0)
        c = lax.broadcasted_iota(jnp.int32, (half, half * (g + 1)), 1)
        masks.append(r + g * half >= c)

    def chunk_step(ci, carry):
        r0 = pl.multiple_of(ci * chunk, chunk)
        rows = pl.ds(r0, chunk)
        for h in range(heads):
            hs = slice(h * HEAD_DIM, (h + 1) * HEAD_DIM)
            q = q_ref[rows, hs].astype(F32)
            lb = lb_ref[:, hs]
            c1 = 0.5 * (1.0 - lb)
            f = (lb + c1) + c1 * jnp.tanh(0.5 * f_ref[rows, hs].astype(F32))
            kk = 1.0 - f
            b = _cumsum_rows(jnp.log(f), row) * LOG2_E
            b_last = b[chunk - 1:chunk, :]
            dec_ref[0:1, hs] = jnp.exp2(b_last)
            mids, ks = [], []
            for g in range(n_half):
                rg = slice(g * half, (g + 1) * half)
                mid = b[g * half + half // 2 - 1:g * half + half // 2, :]
                qs_g = q[rg] * jnp.exp2(b[rg] - mid)
                ks_g = kk[rg] * jnp.exp2(mid - b[rg])
                mids.append(mid)
                ks.append(ks_g)
                qs_ref[rg, hs] = qs_g.astype(BF16)
                qe_ref[rg, hs] = (qs_g * jnp.exp2(mid)).astype(BF16)
                kd_ref[rg, hs] = (ks_g * jnp.exp2(b_last - mid)).astype(BF16)
                for gp in range(g):
                    key_refs[g][gp * half:(gp + 1) * half, hs] = (
                        ks[gp] * jnp.exp2(mid - mids[gp])).astype(BF16)
                key_refs[g][rg, hs] = ks_g.astype(BF16)
        for h in range(heads):
            hs = slice(h * HEAD_DIM, (h + 1) * HEAD_DIM)
            for g in range(n_half):
                rg = slice(g * half, (g + 1) * half)
                a = _dot_nt(qs_ref[rg, hs], key_refs[g][:, hs])
                a_ref[h, rg, 0:half * (g + 1)] = jnp.where(masks[g], a, 0.0).astype(BF16)
        for h in range(heads):
            hs = slice(h * HEAD_DIM, (h + 1) * HEAD_DIM)
            v = i_ref[rows, hs]
            st = st_ref[h]
            o_inter = _dot_nt(qe_ref[:, hs], st.astype(BF16))
            for g in range(n_half):
                rg = slice(g * half, (g + 1) * half)
                o_ref[rg, hs] = o_inter[rg] + _dot(a_ref[h, rg, 0:half * (g + 1)], v[0:half * (g + 1)])
            st_ref[h] = st * dec_ref[0:1, hs] + _dot_tn(v, kd_ref[:, hs])
        for h in range(heads):
            hs = slice(h * HEAD_DIM, (h + 1) * HEAD_DIM)
            o = o_ref[:, hs]
            o_n = o * lax.rsqrt(jnp.mean(o * o, axis=-1, keepdims=True) + EPS) * (0.5 * gh_ref[:, hs])
            gate2 = jnp.tanh(0.5 * zo_ref[rows, hs].astype(F32)) + 1.0
            og_ref[rows, hs] = (o_n * gate2).astype(og_ref.dtype)
        return carry

    lax.fori_loop(0, n_chunks, chunk_step, 0)

    @pl.when(ti == pl.num_programs(2) - 1)
    def _():
        for h in range(heads):
            sout_ref[0, h] = st_ref[h].T


def _hgrn(z, s0, lb, g_head, *, row_base, batch, seq, d_model, name):
    n_heads = d_model // HEAD_DIM
    heads = min(16, n_heads)
    chunk = min(REC_CHUNK, seq)
    half = min(REC_HALF, chunk)
    assert seq % chunk == 0 and chunk % half == 0
    rt = max(_pick(seq, (512, 256, 128, 64, 32)), chunk)
    assert row_base % rt == 0
    tiles = seq // rt
    wblk = heads * HEAD_DIM
    cpb = d_model // wblk

    def zspec(section):
        return pl.BlockSpec((rt, wblk),
                            lambda b, hg, i: (row_base // rt + b * tiles + i, section * cpb + hg))

    return pl.pallas_call(
        functools.partial(_hgrn_body, chunk=chunk, n_chunks=rt // chunk, heads=heads),
        grid=(batch, n_heads // heads, tiles),
        in_specs=[zspec(0), zspec(1), zspec(2), zspec(3),
                  pl.BlockSpec((1, heads, HEAD_DIM, HEAD_DIM), lambda b, hg, i: (b, hg, 0, 0)),
                  pl.BlockSpec((1, wblk), lambda b, hg, i: (0, hg)),
                  pl.BlockSpec((1, wblk), lambda b, hg, i: (0, hg))],
        out_specs=[pl.BlockSpec((rt, wblk), lambda b, hg, i: (b * tiles + i, hg)),
                   pl.BlockSpec((1, heads, HEAD_DIM, HEAD_DIM), lambda b, hg, i: (b, hg, 0, 0))],
        out_shape=[jax.ShapeDtypeStruct((batch * seq, d_model), BF16),
                   jax.ShapeDtypeStruct((batch, n_heads, HEAD_DIM, HEAD_DIM), F32)],
        scratch_shapes=[pltpu.VMEM((heads, HEAD_DIM, HEAD_DIM), F32),
                        pltpu.VMEM((chunk, wblk), BF16),
                        pltpu.VMEM((chunk, wblk), BF16),
                        pltpu.VMEM((chunk, wblk), BF16),
                        pltpu.VMEM((SUBLANES, wblk), F32),
                        pltpu.VMEM((heads, chunk, chunk), BF16),
                        pltpu.VMEM((chunk, wblk), F32)]
                       + [pltpu.VMEM((half * (g + 1), wblk), BF16) for g in range(chunk // half)],
        compiler_params=pltpu.CompilerParams(dimension_semantics=("parallel", "parallel", "arbitrary"),
                                             vmem_limit_bytes=VMEM_LIMIT),
        name=name,
    )(z, z, z, z, s0, lb, g_head)


def _mix_body(zu_ref, zv_ref, zga_ref, zgb_ref, ogp_ref, ogs_ref, xp_ref, xs_ref, lng_ref, lnb_ref,
              wsp_ref, wss_ref, bsp_ref, bss_ref, wpa_ref, wpb_ref, wo_ref,
              h1_ref, vnp_ref, vns_ref, sg_ref, *, n_prompt_tiles, dec_seq):
    is_p = pl.program_id(0) < n_prompt_tiles
    tm, width = zu_ref.shape
    gd = width // MLP_GROUPS
    u = _gelu(zu_ref[...].astype(F32))
    gv = _gelu(zv_ref[...].astype(F32))
    xc = gv - jnp.mean(gv, axis=-1, keepdims=True)
    vn = xc * lax.rsqrt(jnp.mean(xc * xc, axis=-1, keepdims=True) + EPS) * lng_ref[...] + lnb_ref[...]

    @pl.when(is_p)
    def _():
        vnp_ref[...] = vn

    @pl.when(jnp.logical_not(is_p))
    def _():
        vns_ref[...] = vn

    vnb = vn.astype(BF16)
    r = lax.broadcasted_iota(jnp.int32, (MLP_CHUNK, MLP_CHUNK), 0)
    c = lax.broadcasted_iota(jnp.int32, (MLP_CHUNK, MLP_CHUNK), 1)
    causal = r >= c
    same_stream = (r // dec_seq) == (c // dec_seq)
    for g in range(MLP_GROUPS):
        w_p = jnp.where(causal, wsp_ref[g], 0.0)
        w_s = jnp.where(causal & same_stream, wss_ref[g], 0.0)
        w = jnp.where(is_p, w_p, w_s).astype(BF16)
        bias = jnp.where(is_p, bsp_ref[g], bss_ref[g])
        for cc in range(tm // MLP_CHUNK):
            rows = slice(cc * MLP_CHUNK, (cc + 1) * MLP_CHUNK)
            cols = slice(g * gd, (g + 1) * gd)
            s = _dot(w, vnb[rows, cols]) + bias
            sg_ref[rows, cols] = (u[rows, cols] * s).astype(BF16)
    y_b = _dot(sg_ref[...], wpb_ref[...])
    og = jnp.where(is_p, ogp_ref[...].astype(F32), ogs_ref[...].astype(F32)).astype(BF16)
    y_a = _dot(og, wpa_ref[...])
    m = _sigmoid(zga_ref[...].astype(F32)) * y_a + _sigmoid(zgb_ref[...].astype(F32)) * y_b
    x = jnp.where(is_p, xp_ref[...], xs_ref[...])
    h1_ref[...] = x + _dot(m.astype(BF16), wo_ref[...])


def _mix(z, og_p, og_s, xp, xs, ln_g, ln_b, ws_p, ws_s, bs_p, bs_s, w_pa, w_pb, w_o, *, batch, dec_seq):
    (n_p, d), n_s = xp.shape, xs.shape[0]
    n = n_p + n_s
    width = w_pb.shape[0]
    tm = 256
    assert n_s % tm == 0 and (n_p // batch) % tm == 0 and tm % MLP_CHUNK == 0 and MLP_CHUNK % dec_seq == 0
    npt, nst = n_p // tm, n_s // tm
    tpb = npt // batch
    u_blk = 4 * d // width
    ga_blk = (4 * d + 2 * width) // d
    const2 = lambda i: (0, 0)
    const3 = lambda i: (0, 0, 0)
    p_map = lambda i: (jnp.minimum(i, npt - 1), 0)
    s_map = lambda i: (jnp.clip(i - npt, 0, nst - 1), 0)
    return pl.pallas_call(
        functools.partial(_mix_body, n_prompt_tiles=npt, dec_seq=dec_seq),
        grid=(n // tm,),
        in_specs=[pl.BlockSpec((tm, width), lambda i: (i, u_blk)),
                  pl.BlockSpec((tm, width), lambda i: (i, u_blk + 1)),
                  pl.BlockSpec((tm, d), lambda i: (i, ga_blk)),
                  pl.BlockSpec((tm, d), lambda i: (i, ga_blk + 1)),
                  pl.BlockSpec((tm, d), p_map),
                  pl.BlockSpec((tm, d), s_map),
                  pl.BlockSpec((tm, d), p_map),
                  pl.BlockSpec((tm, d), s_map),
                  pl.BlockSpec((1, width), const2),
                  pl.BlockSpec((1, width), const2),
                  pl.BlockSpec((MLP_GROUPS, MLP_CHUNK, MLP_CHUNK), const3),
                  pl.BlockSpec((MLP_GROUPS, MLP_CHUNK, MLP_CHUNK), const3),
                  pl.BlockSpec((MLP_GROUPS, MLP_CHUNK, 1), const3),
                  pl.BlockSpec((MLP_GROUPS, MLP_CHUNK, 1), const3),
                  pl.BlockSpec((d, d), const2, pipeline_mode=pl.Buffered(1)),
                  pl.BlockSpec((width, d), const2, pipeline_mode=pl.Buffered(1)),
                  pl.BlockSpec((d, d), const2, pipeline_mode=pl.Buffered(1))],
        out_specs=[pl.BlockSpec((tm, d), lambda i: (i, 0)),
                   pl.BlockSpec((tm, width), lambda i: (jnp.minimum(i // tpb, batch - 1), 0)),
                   pl.BlockSpec((tm, width), s_map)],
        out_shape=[jax.ShapeDtypeStruct((n, d), F32),
                   jax.ShapeDtypeStruct((batch * tm, width), F32),
                   jax.ShapeDtypeStruct((n_s, width), F32)],
        scratch_shapes=[pltpu.VMEM((tm, width), BF16)],
        compiler_params=pltpu.CompilerParams(dimension_semantics=("arbitrary",),
                                             vmem_limit_bytes=VMEM_LIMIT),
        name="mix",
    )(z, z, z, z, og_p, og_s, xp, xs, ln_g, ln_b, ws_p, ws_s, bs_p, bs_s, w_pa, w_pb, w_o)


def _router_body(h_ref, g_ref, wr_ref, br_ref, c_ref, idx_ref, gate_ref, cnt_ref, carry_ref):
    @pl.when(pl.program_id(0) == 0)
    def _():
        carry_ref[...] = jnp.zeros_like(carry_ref)

    tm = h_ref.shape[0]
    c = _rms(h_ref[...], g_ref[...])
    c_ref[...] = c
    lt = _dot_nt(wr_ref[...], c, precision=lax.Precision.HIGHEST) + br_ref[...]
    le = lt[0:N_EXPERTS]
    lg = lt[N_EXPERTS:N_EXPERTS + N_GROUPS]
    gmax = jnp.max(lg, axis=0, keepdims=True)
    p_sel = 1.0 / jnp.sum(jnp.exp(lg - gmax), axis=0, keepdims=True)
    best = lg[0:1]
    gi = jnp.zeros((1, tm), jnp.int32)
    for g in range(1, N_GROUPS):
        better = lg[g:g + 1] > best
        gi = jnp.where(better, g, gi)
        best = jnp.where(better, lg[g:g + 1], best)
    leg = jnp.zeros((E_PER_GROUP, tm), F32)
    for g in range(N_GROUPS):
        leg = jnp.where(gi == g, le[g * E_PER_GROUP:(g + 1) * E_PER_GROUP], leg)
    sub = lax.broadcasted_iota(jnp.int32, (E_PER_GROUP, tm), 0).astype(F32)
    v1 = jnp.max(leg, axis=0, keepdims=True)
    i1 = jnp.min(jnp.where(leg == v1, sub, float(E_PER_GROUP)), axis=0, keepdims=True)
    rest = jnp.where(sub == i1, -jnp.inf, leg)
    v2 = jnp.max(rest, axis=0, keepdims=True)
    i2 = jnp.min(jnp.where(rest == v2, sub, float(E_PER_GROUP)), axis=0, keepdims=True)
    e2 = jnp.exp(v2 - v1)
    den = 1.0 + e2
    gate0 = p_sel * (1.0 / den)
    gate1 = p_sel * (e2 / den)
    ex0 = gi * E_PER_GROUP + i1.astype(jnp.int32)
    ex1 = gi * E_PER_GROUP + i2.astype(jnp.int32)
    eid = lax.broadcasted_iota(jnp.int32, (N_EXPERTS, tm), 0)
    oh0 = eid == ex0
    oh1 = eid == ex1
    oh = jnp.where(oh0 | oh1, 1.0, 0.0)
    upper = jnp.where(lax.broadcasted_iota(jnp.int32, (tm, tm), 0) < lax.broadcasted_iota(jnp.int32, (tm, tm), 1),
                      1.0, 0.0).astype(BF16)
    before = _dot(oh.astype(BF16), upper) + carry_ref[:, 0:1]
    rank0 = jnp.sum(jnp.where(oh0, before, 0.0), axis=0, keepdims=True)
    rank1 = jnp.sum(jnp.where(oh1, before, 0.0), axis=0, keepdims=True)
    carry = carry_ref[...] + jnp.sum(oh, axis=1, keepdims=True)
    carry_ref[...] = carry
    cnt_ref[...] = carry.astype(jnp.int32)
    idx_ref[...] = jnp.zeros_like(idx_ref)
    idx_ref[0:1, :] = ex0
    idx_ref[1:2, :] = ex1
    idx_ref[2:3, :] = rank0.astype(jnp.int32)
    idx_ref[3:4, :] = rank1.astype(jnp.int32)
    gate_ref[...] = jnp.zeros_like(gate_ref)
    gate_ref[0:1, :] = gate0
    gate_ref[1:2, :] = gate1


def _router(h1, g_ffn, wr, br):
    n, d = h1.shape
    tm = _pick(n, (512, 256))
    return pl.pallas_call(
        _router_body,
        grid=(n // tm,),
        in_specs=[pl.BlockSpec((tm, d), lambda i: (i, 0)),
                  pl.BlockSpec((1, d), lambda i: (0, 0)),
                  pl.BlockSpec((ROUTER_ROWS, d), lambda i: (0, 0)),
                  pl.BlockSpec((ROUTER_ROWS, 1), lambda i: (0, 0))],
        out_specs=[pl.BlockSpec((tm, d), lambda i: (i, 0)),
                   pl.BlockSpec((SUBLANES, tm), lambda i: (0, i)),
                   pl.BlockSpec((SUBLANES, tm), lambda i: (0, i)),
                   pl.BlockSpec((N_EXPERTS, LANES), lambda i: (0, 0))],
        out_shape=[jax.ShapeDtypeStruct((n, d), F32),
                   jax.ShapeDtypeStruct((SUBLANES, n), jnp.int32),
                   jax.ShapeDtypeStruct((SUBLANES, n), F32),
                   jax.ShapeDtypeStruct((N_EXPERTS, LANES), jnp.int32)],
        scratch_shapes=[pltpu.VMEM((N_EXPERTS, LANES), F32)],
        compiler_params=pltpu.CompilerParams(dimension_semantics=("arbitrary",),
                                             vmem_limit_bytes=VMEM_LIMIT),
        name="router",
    )(h1, g_ffn, wr, br)


def _row_copy(src_ref, src_row, dst_ref, dst_row, sem):
    return pltpu.make_async_copy(src_ref.at[pl.ds(src_row, 1)], dst_ref.at[pl.ds(dst_row, 1)], sem)


def _dispatch_body(fill_lo_ref, fill_hi_ref, dest_ref, c_ref, xs_ref, zero_ref, sem, zsem):
    td = c_ref.shape[0]

    @pl.when(pl.program_id(0) == 0)
    def _():
        zero_ref[...] = jnp.zeros_like(zero_ref)

        def per_expert(e, carry):
            lo, hi = fill_lo_ref[e], fill_hi_ref[e]

            def start(r, c2):
                _row_copy(zero_ref, 0, xs_ref, r, zsem).start()
                return c2

            def wait(r, c2):
                _row_copy(zero_ref, 0, xs_ref, 0, zsem).wait()
                return c2

            lax.fori_loop(lo, hi, start, 0)
            lax.fori_loop(lo, hi, wait, 0)
            return carry

        lax.fori_loop(0, N_EXPERTS, per_expert, 0)

        def block_copy(j):
            return pltpu.make_async_copy(zero_ref, xs_ref.at[pl.ds(j * MOE_BLOCK, MOE_BLOCK)], zsem)

        first_unused = fill_hi_ref[N_EXPERTS - 1] // MOE_BLOCK
        n_blocks = xs_ref.shape[0] // MOE_BLOCK

        def start_block(j, carry):
            block_copy(j).start()
            return carry

        def wait_block(j, carry):
            block_copy(0).wait()
            return carry

        lax.fori_loop(first_unused, n_blocks, start_block, 0)
        lax.fori_loop(first_unused, n_blocks, wait_block, 0)

    def issue(u, carry):
        t0 = pl.multiple_of(u * SUBLANES, SUBLANES)
        for s in range(SUBLANES):
            for k in range(TOP_K):
                _row_copy(c_ref, t0 + s, xs_ref, dest_ref[k, t0 + s], sem).start()
        return carry

    lax.fori_loop(0, td // SUBLANES, issue, 0)
    for k in range(TOP_K):
        pltpu.make_async_copy(c_ref, xs_ref.at[pl.ds(0, td)], sem).wait()


def _dispatch(fill_lo, fill_hi, dest, c, n_slots):
    n, d = c.shape
    td = _pick(n, (512, 256))
    grid_spec = pltpu.PrefetchScalarGridSpec(
        num_scalar_prefetch=2,
        grid=(n // td,),
        in_specs=[pl.BlockSpec((TOP_K, td), lambda i, lo, hi: (0, i), memory_space=pltpu.SMEM),
                  pl.BlockSpec((td, d), lambda i, lo, hi: (i, 0))],
        out_specs=pl.BlockSpec(memory_space=pl.ANY),
        scratch_shapes=[pltpu.VMEM((MOE_BLOCK, d), F32), pltpu.SemaphoreType.DMA(()), pltpu.SemaphoreType.DMA(())],
    )
    return pl.pallas_call(
        _dispatch_body,
        grid_spec=grid_spec,
        out_shape=jax.ShapeDtypeStruct((n_slots, d), F32),
        compiler_params=pltpu.CompilerParams(dimension_semantics=("arbitrary",),
                                             vmem_limit_bytes=VMEM_LIMIT),
        name="dispatch",
    )(fill_lo, fill_hi, dest, c)


def _expert_body(be_ref, na_ref, xs_ref, wg_ref, wu_ref, wd_ref, yb_ref):
    del be_ref
    active = pl.program_id(0) < na_ref[0]

    @pl.when(active)
    def _():
        x = xs_ref[...].astype(BF16)
        a = _dot(x, wg_ref[0])
        b = _dot(x, wu_ref[0])
        hid = (a * _sigmoid(a)) * b
        yb_ref[...] = _dot(hid.astype(BF16), wd_ref[0])

    @pl.when(jnp.logical_not(active))
    def _():
        yb_ref[...] = jnp.zeros_like(yb_ref)


def _experts(block_exp, n_active, xs, w_gate, w_up, w_down):
    n_slots, d = xs.shape
    de = w_gate.shape[2]
    nb = n_slots // MOE_BLOCK
    blk = lambda j, be, na: (jnp.minimum(j, na[0] - 1), 0)
    wblk = lambda j, be, na: (be[jnp.minimum(j, na[0] - 1)], 0, 0)
    grid_spec = pltpu.PrefetchScalarGridSpec(
        num_scalar_prefetch=2,
        grid=(nb,),
        in_specs=[pl.BlockSpec((MOE_BLOCK, d), blk),
                  pl.BlockSpec((1, d, de), wblk),
                  pl.BlockSpec((1, d, de), wblk),
                  pl.BlockSpec((1, de, d), wblk)],
        out_specs=pl.BlockSpec((MOE_BLOCK, d), lambda j, be, na: (j, 0)),
    )
    return pl.pallas_call(
        _expert_body,
        grid_spec=grid_spec,
        out_shape=jax.ShapeDtypeStruct((n_slots, d), F32),
        compiler_params=pltpu.CompilerParams(dimension_semantics=("arbitrary",),
                                             vmem_limit_bytes=VMEM_LIMIT),
        name="experts",
    )(block_exp, n_active, xs, w_gate, w_up, w_down)


def _combine_body(dcur_ref, dnxt_ref, gate_ref, h1_ref, p_ref, yb_ref, gple_ref, wpg_ref, wpp_ref, gfin_ref,
                  yp_ref, ys_ref, buf_ref, sem, *, n_prompt_tiles, final_norm):
    i = pl.program_id(0)
    n_tiles = pl.num_programs(0)
    tc = h1_ref.shape[0]
    slot = lax.rem(i, 2)

    def issue(dest_ref, into):
        def body(u, carry):
            t0 = pl.multiple_of(u * SUBLANES, SUBLANES)
            for s in range(SUBLANES):
                for k in range(TOP_K):
                    _row_copy(yb_ref, dest_ref[k, t0 + s], buf_ref.at[into, k], t0 + s, sem.at[into]).start()
            return carry

        lax.fori_loop(0, tc // SUBLANES, body, 0)

    @pl.when(i == 0)
    def _():
        issue(dcur_ref, 0)

    @pl.when(i + 1 < n_tiles)
    def _():
        issue(dnxt_ref, 1 - slot)

    for k in range(TOP_K):
        pltpu.make_async_copy(yb_ref.at[pl.ds(0, tc)], buf_ref.at[slot, k], sem.at[slot]).wait()

    gt = gate_ref[...].T
    h2 = h1_ref[...] + (gt[:, 0:1] * buf_ref[slot, 0] + gt[:, 1:2] * buf_ref[slot, 1])
    a = _rms(h2, gple_ref[...]).astype(BF16)
    h3 = h2 + _sigmoid(_dot(a, wpg_ref[...])) * _dot(p_ref[...].astype(BF16), wpp_ref[...])
    if final_norm:
        h3 = _rms(h3, gfin_ref[...])

    @pl.when(i < n_prompt_tiles)
    def _():
        yp_ref[...] = h3

    @pl.when(i >= n_prompt_tiles)
    def _():
        ys_ref[...] = h3


def _combine(dest, gates, h1, p, yb, g_ple, w_pg, w_pp, g_final, *, n_prompt, final_norm):
    n, d = h1.shape
    ple = p.shape[1]
    tc = 256
    assert n_prompt % tc == 0 and (n - n_prompt) % tc == 0
    nt, npt = n // tc, n_prompt // tc
    nst = nt - npt
    const2 = lambda i: (0, 0)
    return pl.pallas_call(
        functools.partial(_combine_body, n_prompt_tiles=npt, final_norm=final_norm),
        grid=(nt,),
        in_specs=[pl.BlockSpec((TOP_K, tc), lambda i: (0, i), memory_space=pltpu.SMEM),
                  pl.BlockSpec((TOP_K, tc), lambda i: (0, jnp.minimum(i + 1, nt - 1)), memory_space=pltpu.SMEM),
                  pl.BlockSpec((SUBLANES, tc), lambda i: (0, i)),
                  pl.BlockSpec((tc, d), lambda i: (i, 0)),
                  pl.BlockSpec((tc, ple), lambda i: (i, 0)),
                  pl.BlockSpec(memory_space=pl.ANY),
                  pl.BlockSpec((1, d), const2),
                  pl.BlockSpec((d, d), const2, pipeline_mode=pl.Buffered(1)),
                  pl.BlockSpec((ple, d), const2, pipeline_mode=pl.Buffered(1)),
                  pl.BlockSpec((1, d), const2)],
        out_specs=[pl.BlockSpec((tc, d), lambda i: (jnp.minimum(i, npt - 1), 0)),
                   pl.BlockSpec((tc, d), lambda i: (jnp.clip(i - npt, 0, nst - 1), 0))],
        out_shape=[jax.ShapeDtypeStruct((n_prompt, d), F32),
                   jax.ShapeDtypeStruct((n - n_prompt, d), F32)],
        scratch_shapes=[pltpu.VMEM((2, TOP_K, tc, d), F32), pltpu.SemaphoreType.DMA((2,))],
        compiler_params=pltpu.CompilerParams(dimension_semantics=("arbitrary",),
                                             vmem_limit_bytes=VMEM_LIMIT),
        name="combine",
    )(dest, dest, gates, h1, p, yb, g_ple, w_pg, w_pp, g_final)


def _layer(hp, hs, p, s0_sample, lb, batch, seq, dec_batch, dec_seq,
           g_mix, w_in, g_head, w_pa, ln_v_g, ln_v_b, w_s, b_s, w_pb, w_o,
           g_ffn, w_gr, b_gr, w_er, b_er, w_gate, w_up, w_down, g_ple, w_pg, w_pp, g_final, final_norm):
    n_prompt, d = hp.shape
    n = n_prompt + hs.shape[0]
    n_heads = d // HEAD_DIM
    row = lambda a: a.reshape(1, -1).astype(F32)

    z = _in_proj(hp, hs, row(g_mix), w_in.astype(BF16))

    lb_row, gh_row = row(lb), row(g_head)
    s0_prompt = jnp.zeros((batch, n_heads, HEAD_DIM, HEAD_DIM), F32)
    og_p, st_p = _hgrn(z, s0_prompt, lb_row, gh_row, row_base=0, batch=batch, seq=seq,
                       d_model=d, name="hgrn_prompt")
    og_s, st_s = _hgrn(z, s0_sample.astype(F32), lb_row, gh_row, row_base=n_prompt, batch=dec_batch,
                       seq=dec_seq, d_model=d, name="hgrn_sample")

    start = PAST_LEN % MLP_CHUNK
    assert start + dec_seq <= MLP_CHUNK
    rep = MLP_CHUNK // dec_seq
    ws_s = jnp.tile(w_s[:, start:start + dec_seq, start:start + dec_seq], (1, rep, rep))
    bs_s = jnp.tile(b_s[:, start:start + dec_seq], (1, rep))
    h1, vn_p, vn_s = _mix(z, og_p, og_s, hp, hs, row(ln_v_g), row(ln_v_b), w_s, ws_s, b_s[..., None],
                          bs_s[..., None], w_pa.astype(BF16), w_pb.astype(BF16), w_o.astype(BF16),
                          batch=batch, dec_seq=dec_seq)

    pad_rows = ROUTER_ROWS - N_EXPERTS - N_GROUPS
    wr = jnp.concatenate([w_er.T, w_gr.T, jnp.zeros((pad_rows, d), F32)], axis=0)
    br = jnp.concatenate([b_er, b_gr, jnp.zeros((pad_rows,), F32)]).reshape(ROUTER_ROWS, 1).astype(F32)
    c, idx, gates, cnt = _router(h1, row(g_ffn), wr, br)

    counts = cnt[:, 0]
    padded = (counts + MOE_BLOCK - 1) // MOE_BLOCK * MOE_BLOCK
    pad_end = jnp.cumsum(padded).astype(jnp.int32)
    pad_start = pad_end - padded
    n_blocks = -(-(n * TOP_K) // MOE_BLOCK) + N_EXPERTS
    block_first = jnp.arange(n_blocks, dtype=jnp.int32) * MOE_BLOCK
    block_exp = jnp.minimum(jnp.sum(pad_end[None, :] <= block_first[:, None], axis=1), N_EXPERTS - 1).astype(jnp.int32)
    n_active = pad_end[-1:] // MOE_BLOCK
    dest = pad_start[idx[0:TOP_K]] + idx[TOP_K:2 * TOP_K]

    xs = _dispatch(pad_start + counts, pad_end, dest, c, n_blocks * MOE_BLOCK)
    yb = _experts(block_exp, n_active, xs, w_gate.astype(BF16), w_up.astype(BF16), w_down.astype(BF16))
    yp, ys = _combine(dest, gates, h1, p, yb, row(g_ple), w_pg.astype(BF16), w_pp.astype(BF16),
                      row(g_final), n_prompt=n_prompt, final_norm=final_norm)
    return yp, ys, st_p, st_s, vn_p, vn_s


def kernel(x_prompt, x_sample, p_prompt, p_sample, state_hgrn, g_mix, w_in, lb_logits, g_head, w_pa, ln_v_g, ln_v_b, w_s, b_s, w_pb, w_o, g_ffn, w_gr, b_gr, w_er, b_er, w_gate, w_up, w_down, g_ple, w_pg, w_pp, g_final):
    batch, seq, d = x_prompt.shape
    dec_batch, dec_seq, _ = x_sample.shape
    depth = g_mix.shape[0]
    n_prompt, n_sample = batch * seq, dec_batch * dec_seq
    width = w_pb.shape[1]
    lbs = jnp.cumsum(jax.nn.softmax(lb_logits.astype(F32), axis=0), axis=0)
    hp, hs = x_prompt.reshape(n_prompt, d), x_sample.reshape(n_sample, d)
    keep = min((seq - 1) % MLP_CHUNK + 1, seq)
    keep_s = min((PAST_LEN % MLP_CHUNK + dec_seq - 1) % MLP_CHUNK + 1, dec_seq)
    sp, ss, vp, vs = [], [], [], []
    for i in range(depth):
        p = jnp.concatenate([p_prompt[i].reshape(n_prompt, -1), p_sample[i].reshape(n_sample, -1)], axis=0)
        hp, hs, st_p, st_s, vn_p, vn_s = _layer(
            hp, hs, p, state_hgrn[i], lbs[i], batch, seq, dec_batch, dec_seq,
            g_mix[i], w_in[i], g_head[i], w_pa[i], ln_v_g[i], ln_v_b[i], w_s[i], b_s[i], w_pb[i], w_o[i],
            g_ffn[i], w_gr[i], b_gr[i], w_er[i], b_er[i], w_gate[i], w_up[i], w_down[i],
            g_ple[i], w_pg[i], w_pp[i], g_final, i == depth - 1)
        sp.append(st_p.astype(x_prompt.dtype))
        ss.append(st_s.astype(state_hgrn.dtype))
        vn_last = vn_p.reshape(batch, -1, width)
        vp.append(vn_last[:, vn_last.shape[1] - keep:])
        vs.append(vn_s.reshape(dec_batch, dec_seq, width)[:, dec_seq - keep_s:])
    y_prompt = hp.reshape(batch, seq, d)
    y_sample = hs.reshape(dec_batch, dec_seq, d)
    return (y_prompt, y_sample, jnp.stack(sp), jnp.stack(ss), jnp.stack(vp), jnp.stack(vs))
```

```python
import functools
import math

import jax
import jax.numpy as jnp
from jax import lax
from jax.experimental import pallas as pl
from jax.experimental.pallas import tpu as pltpu

F32 = jnp.float32
BF16 = jnp.bfloat16

EPS = 1e-6
HEAD_DIM = 128
REC_CHUNK = 64
REC_HALF = 32
MLP_CHUNK = 128
MLP_GROUPS = 4
N_GROUPS = 4
E_PER_GROUP = 8
N_EXPERTS = N_GROUPS * E_PER_GROUP
TOP_K = 2
MOE_BLOCK = 128
PAST_LEN = 2048
LANES = 128
SUBLANES = 8
ROUTER_ROWS = 40
VMEM_LIMIT = 56 * 1024 * 1024
LOG2_E = 1.4426950408889634
WEIGHT_STAGE_ROWS = 256


def _pick(n, candidates):
    for c in candidates:
        if n % c == 0:
            return c
    raise ValueError(f"no tile in {candidates} divides {n}")


def _dot(a, b):
    return jnp.dot(a, b, preferred_element_type=F32)


def _dot_nt(a, b, precision=None):
    return lax.dot_general(a, b, (((1,), (1,)), ((), ())), precision=precision, preferred_element_type=F32)


def _dot_tn(a, b):
    return lax.dot_general(a, b, (((0,), (0,)), ((), ())), preferred_element_type=F32)


def _rms(x, g):
    return x * lax.rsqrt(jnp.mean(x * x, axis=-1, keepdims=True) + EPS) * g


def _gelu(x):
    c = math.sqrt(2.0 / math.pi)
    return x * (0.5 * (1.0 + jnp.tanh(c * (x + 0.044715 * (x * x * x)))))


def _sigmoid(x):
    return 0.5 * jnp.tanh(0.5 * x) + 0.5


def _load_as_bf16(w_hbm, w_vmem, stage_ref, sem):
    rows = min(stage_ref.shape[1], w_hbm.shape[0])
    assert w_hbm.shape[0] % rows == 0
    n_chunks = w_hbm.shape[0] // rows

    def copy(c):
        return pltpu.make_async_copy(w_hbm.at[pl.ds(c * rows, rows)],
                                     stage_ref.at[c % 2, pl.ds(0, rows)], sem.at[c % 2])

    copy(0).start()
    for c in range(n_chunks):
        if c + 1 < n_chunks:
            copy(c + 1).start()
        copy(c).wait()
        w_vmem[pl.ds(c * rows, rows), :] = stage_ref[c % 2, pl.ds(0, rows)].astype(BF16)


def _in_proj_body(xp_ref, xs_ref, g_ref, w_ref, z_ref, xn_ref, *, n_prompt_tiles):
    first = pl.program_id(1) == 0
    is_p = pl.program_id(0) < n_prompt_tiles

    @pl.when(first & is_p)
    def _():
        xn_ref[...] = _rms(xp_ref[...], g_ref[...]).astype(BF16)

    @pl.when(first & jnp.logical_not(is_p))
    def _():
        xn_ref[...] = _rms(xs_ref[...], g_ref[...]).astype(BF16)

    z_ref[...] = _dot(xn_ref[...], w_ref[...].astype(BF16)).astype(z_ref.dtype)


def _in_proj(xp, xs, g, w):
    (n_p, d), n_s = xp.shape, xs.shape[0]
    width = w.shape[1]
    tm = _pick(math.gcd(n_p, n_s), (1024, 512, 256))
    tn = _pick(width, (1024, 512, 256, 128))
    npt, nst = n_p // tm, n_s // tm
    return pl.pallas_call(
        functools.partial(_in_proj_body, n_prompt_tiles=npt),
        grid=(npt + nst, width // tn),
        in_specs=[pl.BlockSpec((tm, d), lambda i, j: (jnp.minimum(i, npt - 1), 0)),
                  pl.BlockSpec((tm, d), lambda i, j: (jnp.clip(i - npt, 0, nst - 1), 0),
                               pipeline_mode=pl.Buffered(1)),
                  pl.BlockSpec((1, d), lambda i, j: (0, 0)),
                  pl.BlockSpec((d, tn), lambda i, j: (0, j))],
        out_specs=pl.BlockSpec((tm, tn), lambda i, j: (i, j)),
        out_shape=jax.ShapeDtypeStruct((n_p + n_s, width), BF16),
        scratch_shapes=[pltpu.VMEM((tm, d), BF16)],
        compiler_params=pltpu.CompilerParams(dimension_semantics=("parallel", "arbitrary"),
                                             vmem_limit_bytes=VMEM_LIMIT),
        name="in_proj",
    )(xp, xs, g, w)


def _cumsum_rows(x, row):
    d = 1
    while d < x.shape[0]:
        x = x + jnp.where(row >= d, pltpu.roll(x, d, 0), 0.0)
        d *= 2
    return x


def _hgrn_body(q_ref, f_ref, i_ref, zo_ref, s0_ref, lb_ref, gh_ref, og_ref, sout_ref,
               st_ref, qs_ref, qe_ref, kd_ref, dec_ref, a_ref, o_ref, *key_refs, chunk, n_chunks, heads):
    ti = pl.program_id(2)

    @pl.when(ti == 0)
    def _():
        for h in range(heads):
            st_ref[h] = s0_ref[0, h].T

    half = min(REC_HALF, chunk)
    n_half = chunk // half
    row = lax.broadcasted_iota(jnp.int32, (chunk, HEAD_DIM), 0)
    masks = []
    for g in range(n_half):
        r = lax.broadcasted_iota(jnp.int32, (half, half * (g + 1)), 0)
        c = lax.broadcasted_iota(jnp.int32, (half, half * (g + 1)), 1)
        masks.append(r + g * half >= c)

    def chunk_step(ci, carry):
        r0 = pl.multiple_of(ci * chunk, chunk)
        rows = pl.ds(r0, chunk)
        for h in range(heads):
            hs = slice(h * HEAD_DIM, (h + 1) * HEAD_DIM)
            q = q_ref[rows, hs].astype(F32)
            lb = lb_ref[:, hs]
            c1 = 0.5 * (1.0 - lb)
            f = (lb + c1) + c1 * jnp.tanh(0.5 * f_ref[rows, hs].astype(F32))
            kk = 1.0 - f
            b = _cumsum_rows(jnp.log(f), row) * LOG2_E
            b_last = b[chunk - 1:chunk, :]
            dec_ref[0:1, hs] = jnp.exp2(b_last)
            mids, ks = [], []
            for g in range(n_half):
                rg = slice(g * half, (g + 1) * half)
                mid = b[g * half + half // 2 - 1:g * half + half // 2, :]
                qs_g = q[rg] * jnp.exp2(b[rg] - mid)
                ks_g = kk[rg] * jnp.exp2(mid - b[rg])
                mids.append(mid)
                ks.append(ks_g)
                qs_ref[rg, hs] = qs_g.astype(BF16)
                qe_ref[rg, hs] = (qs_g * jnp.exp2(mid)).astype(BF16)
                kd_ref[rg, hs] = (ks_g * jnp.exp2(b_last - mid)).astype(BF16)
                for gp in range(g):
                    key_refs[g][gp * half:(gp + 1) * half, hs] = (
                        ks[gp] * jnp.exp2(mid - mids[gp])).astype(BF16)
                key_refs[g][rg, hs] = ks_g.astype(BF16)
        for h in range(heads):
            hs = slice(h * HEAD_DIM, (h + 1) * HEAD_DIM)
            for g in range(n_half):
                rg = slice(g * half, (g + 1) * half)
                a = _dot_nt(qs_ref[rg, hs], key_refs[g][:, hs])
                a_ref[h, rg, 0:half * (g + 1)] = jnp.where(masks[g], a, 0.0).astype(BF16)
        for h in range(heads):
            hs = slice(h * HEAD_DIM, (h + 1) * HEAD_DIM)
            v = i_ref[rows, hs]
            st = st_ref[h]
            o_inter = _dot_nt(qe_ref[:, hs], st.astype(BF16))
            for g in range(n_half):
                rg = slice(g * half, (g + 1) * half)
                o_ref[rg, hs] = o_inter[rg] + _dot(a_ref[h, rg, 0:half * (g + 1)], v[0:half * (g + 1)])
            st_ref[h] = st * dec_ref[0:1, hs] + _dot_tn(v, kd_ref[:, hs])
        for h in range(heads):
            hs = slice(h * HEAD_DIM, (h + 1) * HEAD_DIM)
            o = o_ref[:, hs]
            o_n = o * lax.rsqrt(jnp.mean(o * o, axis=-1, keepdims=True) + EPS) * (0.5 * gh_ref[:, hs])
            gate2 = jnp.tanh(0.5 * zo_ref[rows, hs].astype(F32)) + 1.0
            og_ref[rows, hs] = (o_n * gate2).astype(og_ref.dtype)
        return carry

    lax.fori_loop(0, n_chunks, chunk_step, 0)

    @pl.when(ti == pl.num_programs(2) - 1)
    def _():
        for h in range(heads):
            sout_ref[0, h] = st_ref[h].T


def _hgrn(z, s0, lb, g_head, *, row_base, batch, seq, d_model, name):
    n_heads = d_model // HEAD_DIM
    heads = min(16, n_heads)
    chunk = min(REC_CHUNK, seq)
    half = min(REC_HALF, chunk)
    assert seq % chunk == 0 and chunk % half == 0
    rt = max(_pick(seq, (512, 256, 128, 64, 32)), chunk)
    assert row_base % rt == 0
    tiles = seq // rt
    wblk = heads * HEAD_DIM
    cpb = d_model // wblk

    def zspec(section):
        return pl.BlockSpec((rt, wblk),
                            lambda b, hg, i: (row_base // rt + b * tiles + i, section * cpb + hg))

    return pl.pallas_call(
        functools.partial(_hgrn_body, chunk=chunk, n_chunks=rt // chunk, heads=heads),
        grid=(batch, n_heads // heads, tiles),
        in_specs=[zspec(0), zspec(1), zspec(2), zspec(3),
                  pl.BlockSpec((1, heads, HEAD_DIM, HEAD_DIM), lambda b, hg, i: (b, hg, 0, 0)),
                  pl.BlockSpec((1, wblk), lambda b, hg, i: (0, hg)),
                  pl.BlockSpec((1, wblk), lambda b, hg, i: (0, hg))],
        out_specs=[pl.BlockSpec((rt, wblk), lambda b, hg, i: (b * tiles + i, hg)),
                   pl.BlockSpec((1, heads, HEAD_DIM, HEAD_DIM), lambda b, hg, i: (b, hg, 0, 0))],
        out_shape=[jax.ShapeDtypeStruct((batch * seq, d_model), BF16),
                   jax.ShapeDtypeStruct((batch, n_heads, HEAD_DIM, HEAD_DIM), F32)],
        scratch_shapes=[pltpu.VMEM((heads, HEAD_DIM, HEAD_DIM), F32),
                        pltpu.VMEM((chunk, wblk), BF16),
                        pltpu.VMEM((chunk, wblk), BF16),
                        pltpu.VMEM((chunk, wblk), BF16),
                        pltpu.VMEM((SUBLANES, wblk), F32),
                        pltpu.VMEM((heads, chunk, chunk), BF16),
                        pltpu.VMEM((chunk, wblk), F32)]
                       + [pltpu.VMEM((half * (g + 1), wblk), BF16) for g in range(chunk // half)],
        compiler_params=pltpu.CompilerParams(dimension_semantics=("parallel", "parallel", "arbitrary"),
                                             vmem_limit_bytes=VMEM_LIMIT),
        name=name,
    )(z, z, z, z, s0, lb, g_head)


def _mix_body(zu_ref, zv_ref, zga_ref, zgb_ref, ogp_ref, ogs_ref, xp_ref, xs_ref, lng_ref, lnb_ref,
              wsp_ref, wss_ref, bsp_ref, bss_ref, wpa_hbm, wpb_hbm, wo_hbm,
              h1_ref, vnp_ref, vns_ref, sg_ref, wpa_ref, wpb_ref, wo_ref, stage_ref, wsem,
              *, n_prompt_tiles, dec_seq):
    @pl.when(pl.program_id(0) == 0)
    def _():
        _load_as_bf16(wpa_hbm, wpa_ref, stage_ref, wsem)
        _load_as_bf16(wpb_hbm, wpb_ref, stage_ref, wsem)
        _load_as_bf16(wo_hbm, wo_ref, stage_ref, wsem)

    is_p = pl.program_id(0) < n_prompt_tiles
    tm, width = zu_ref.shape
    gd = width // MLP_GROUPS
    u = _gelu(zu_ref[...].astype(F32))
    gv = _gelu(zv_ref[...].astype(F32))
    xc = gv - jnp.mean(gv, axis=-1, keepdims=True)
    vn = xc * lax.rsqrt(jnp.mean(xc * xc, axis=-1, keepdims=True) + EPS) * lng_ref[...] + lnb_ref[...]

    @pl.when(is_p)
    def _():
        vnp_ref[...] = vn

    @pl.when(jnp.logical_not(is_p))
    def _():
        vns_ref[...] = vn

    vnb = vn.astype(BF16)
    r = lax.broadcasted_iota(jnp.int32, (MLP_CHUNK, MLP_CHUNK), 0)
    c = lax.broadcasted_iota(jnp.int32, (MLP_CHUNK, MLP_CHUNK), 1)
    causal = r >= c
    same_stream = (r // dec_seq) == (c // dec_seq)
    for g in range(MLP_GROUPS):
        w_p = jnp.where(causal, wsp_ref[g], 0.0)
        w_s = jnp.where(causal & same_stream, wss_ref[g], 0.0)
        w = jnp.where(is_p, w_p, w_s).astype(BF16)
        bias = jnp.where(is_p, bsp_ref[g], bss_ref[g])
        for cc in range(tm // MLP_CHUNK):
            rows = slice(cc * MLP_CHUNK, (cc + 1) * MLP_CHUNK)
            cols = slice(g * gd, (g + 1) * gd)
            s = _dot(w, vnb[rows, cols]) + bias
            sg_ref[rows, cols] = (u[rows, cols] * s).astype(BF16)
    y_b = _dot(sg_ref[...], wpb_ref[...])
    og = jnp.where(is_p, ogp_ref[...].astype(F32), ogs_ref[...].astype(F32)).astype(BF16)
    y_a = _dot(og, wpa_ref[...])
    m = _sigmoid(zga_ref[...].astype(F32)) * y_a + _sigmoid(zgb_ref[...].astype(F32)) * y_b
    x = jnp.where(is_p, xp_ref[...], xs_ref[...])
    h1_ref[...] = x + _dot(m.astype(BF16), wo_ref[...])


def _mix(z, og_p, og_s, xp, xs, ln_g, ln_b, ws_p, ws_s, bs_p, bs_s, w_pa, w_pb, w_o, *, batch, dec_seq):
    (n_p, d), n_s = xp.shape, xs.shape[0]
    n = n_p + n_s
    width = w_pb.shape[0]
    tm = 256
    assert n_s % tm == 0 and (n_p // batch) % tm == 0 and tm % MLP_CHUNK == 0 and MLP_CHUNK % dec_seq == 0
    npt, nst = n_p // tm, n_s // tm
    tpb = npt // batch
    u_blk = 4 * d // width
    ga_blk = (4 * d + 2 * width) // d
    const2 = lambda i: (0, 0)
    const3 = lambda i: (0, 0, 0)
    p_map = lambda i: (jnp.minimum(i, npt - 1), 0)
    s_map = lambda i: (jnp.clip(i - npt, 0, nst - 1), 0)
    return pl.pallas_call(
        functools.partial(_mix_body, n_prompt_tiles=npt, dec_seq=dec_seq),
        grid=(n // tm,),
        in_specs=[pl.BlockSpec((tm, width), lambda i: (i, u_blk)),
                  pl.BlockSpec((tm, width), lambda i: (i, u_blk + 1)),
                  pl.BlockSpec((tm, d), lambda i: (i, ga_blk)),
                  pl.BlockSpec((tm, d), lambda i: (i, ga_blk + 1)),
                  pl.BlockSpec((tm, d), p_map),
                  pl.BlockSpec((tm, d), s_map),
                  pl.BlockSpec((tm, d), p_map),
                  pl.BlockSpec((tm, d), s_map),
                  pl.BlockSpec((1, width), const2),
                  pl.BlockSpec((1, width), const2),
                  pl.BlockSpec((MLP_GROUPS, MLP_CHUNK, MLP_CHUNK), const3),
                  pl.BlockSpec((MLP_GROUPS, MLP_CHUNK, MLP_CHUNK), const3),
                  pl.BlockSpec((MLP_GROUPS, MLP_CHUNK, 1), const3),
                  pl.BlockSpec((MLP_GROUPS, MLP_CHUNK, 1), const3),
                  pl.BlockSpec(memory_space=pl.ANY),
                  pl.BlockSpec(memory_space=pl.ANY),
                  pl.BlockSpec(memory_space=pl.ANY)],
        out_specs=[pl.BlockSpec((tm, d), lambda i: (i, 0)),
                   pl.BlockSpec((tm, width), lambda i: (jnp.minimum(i // tpb, batch - 1), 0)),
                   pl.BlockSpec((tm, width), s_map)],
        out_shape=[jax.ShapeDtypeStruct((n, d), F32),
                   jax.ShapeDtypeStruct((batch * tm, width), F32),
                   jax.ShapeDtypeStruct((n_s, width), F32)],
        scratch_shapes=[pltpu.VMEM((tm, width), BF16),
                        pltpu.VMEM((d, d), BF16), pltpu.VMEM((width, d), BF16), pltpu.VMEM((d, d), BF16),
                        pltpu.VMEM((2, WEIGHT_STAGE_ROWS, d), F32), pltpu.SemaphoreType.DMA((2,))],
        compiler_params=pltpu.CompilerParams(dimension_semantics=("arbitrary",),
                                             vmem_limit_bytes=VMEM_LIMIT),
        name="mix",
    )(z, z, z, z, og_p, og_s, xp, xs, ln_g, ln_b, ws_p, ws_s, bs_p, bs_s, w_pa, w_pb, w_o)


def _router_body(h_ref, g_ref, wr_ref, br_ref, c_ref, idx_ref, gate_ref, cnt_ref, carry_ref):
    @pl.when(pl.program_id(0) == 0)
    def _():
        carry_ref[...] = jnp.zeros_like(carry_ref)

    tm = h_ref.shape[0]
    c = _rms(h_ref[...], g_ref[...])
    c_ref[...] = c
    lt = _dot_nt(wr_ref[...], c, precision=lax.Precision.HIGHEST) + br_ref[...]
    le = lt[0:N_EXPERTS]
    lg = lt[N_EXPERTS:N_EXPERTS + N_GROUPS]
    gmax = jnp.max(lg, axis=0, keepdims=True)
    p_sel = 1.0 / jnp.sum(jnp.exp(lg - gmax), axis=0, keepdims=True)
    best = lg[0:1]
    gi = jnp.zeros((1, tm), jnp.int32)
    for g in range(1, N_GROUPS):
        better = lg[g:g + 1] > best
        gi = jnp.where(better, g, gi)
        best = jnp.where(better, lg[g:g + 1], best)
    leg = jnp.zeros((E_PER_GROUP, tm), F32)
    for g in range(N_GROUPS):
        leg = jnp.where(gi == g, le[g * E_PER_GROUP:(g + 1) * E_PER_GROUP], leg)
    sub = lax.broadcasted_iota(jnp.int32, (E_PER_GROUP, tm), 0).astype(F32)
    v1 = jnp.max(leg, axis=0, keepdims=True)
    i1 = jnp.min(jnp.where(leg == v1, sub, float(E_PER_GROUP)), axis=0, keepdims=True)
    rest = jnp.where(sub == i1, -jnp.inf, leg)
    v2 = jnp.max(rest, axis=0, keepdims=True)
    i2 = jnp.min(jnp.where(rest == v2, sub, float(E_PER_GROUP)), axis=0, keepdims=True)
    e2 = jnp.exp(v2 - v1)
    den = 1.0 + e2
    gate0 = p_sel * (1.0 / den)
    gate1 = p_sel * (e2 / den)
    ex0 = gi * E_PER_GROUP + i1.astype(jnp.int32)
    ex1 = gi * E_PER_GROUP + i2.astype(jnp.int32)
    eid = lax.broadcasted_iota(jnp.int32, (N_EXPERTS, tm), 0)
    oh0 = eid == ex0
    oh1 = eid == ex1
    oh = jnp.where(oh0 | oh1, 1.0, 0.0)
    upper = jnp.where(lax.broadcasted_iota(jnp.int32, (tm, tm), 0) < lax.broadcasted_iota(jnp.int32, (tm, tm), 1),
                      1.0, 0.0).astype(BF16)
    before = _dot(oh.astype(BF16), upper) + carry_ref[:, 0:1]
    rank0 = jnp.sum(jnp.where(oh0, before, 0.0), axis=0, keepdims=True)
    rank1 = jnp.sum(jnp.where(oh1, before, 0.0), axis=0, keepdims=True)
    carry = carry_ref[...] + jnp.sum(oh, axis=1, keepdims=True)
    carry_ref[...] = carry
    cnt_ref[...] = carry.astype(jnp.int32)
    idx_ref[...] = jnp.zeros_like(idx_ref)
    idx_ref[0:1, :] = ex0
    idx_ref[1:2, :] = ex1
    idx_ref[2:3, :] = rank0.astype(jnp.int32)
    idx_ref[3:4, :] = rank1.astype(jnp.int32)
    gate_ref[...] = jnp.zeros_like(gate_ref)
    gate_ref[0:1, :] = gate0
    gate_ref[1:2, :] = gate1


def _router(h1, g_ffn, wr, br):
    n, d = h1.shape
    tm = _pick(n, (512, 256))
    return pl.pallas_call(
        _router_body,
        grid=(n // tm,),
        in_specs=[pl.BlockSpec((tm, d), lambda i: (i, 0)),
                  pl.BlockSpec((1, d), lambda i: (0, 0)),
                  pl.BlockSpec((ROUTER_ROWS, d), lambda i: (0, 0)),
                  pl.BlockSpec((ROUTER_ROWS, 1), lambda i: (0, 0))],
        out_specs=[pl.BlockSpec((tm, d), lambda i: (i, 0)),
                   pl.BlockSpec((SUBLANES, tm), lambda i: (0, i)),
                   pl.BlockSpec((SUBLANES, tm), lambda i: (0, i)),
                   pl.BlockSpec((N_EXPERTS, LANES), lambda i: (0, 0))],
        out_shape=[jax.ShapeDtypeStruct((n, d), F32),
                   jax.ShapeDtypeStruct((SUBLANES, n), jnp.int32),
                   jax.ShapeDtypeStruct((SUBLANES, n), F32),
                   jax.ShapeDtypeStruct((N_EXPERTS, LANES), jnp.int32)],
        scratch_shapes=[pltpu.VMEM((N_EXPERTS, LANES), F32)],
        compiler_params=pltpu.CompilerParams(dimension_semantics=("arbitrary",),
                                             vmem_limit_bytes=VMEM_LIMIT),
        name="router",
    )(h1, g_ffn, wr, br)


def _row_copy(src_ref, src_row, dst_ref, dst_row, sem):
    return pltpu.make_async_copy(src_ref.at[pl.ds(src_row, 1)], dst_ref.at[pl.ds(dst_row, 1)], sem)


def _dispatch_body(fill_lo_ref, fill_hi_ref, dest_ref, c_ref, xs_ref, zero_ref, sem, zsem):
    td = c_ref.shape[0]

    @pl.when(pl.program_id(0) == 0)
    def _():
        zero_ref[...] = jnp.zeros_like(zero_ref)

        def per_expert(e, carry):
            lo, hi = fill_lo_ref[e], fill_hi_ref[e]

            def start(r, c2):
                _row_copy(zero_ref, 0, xs_ref, r, zsem).start()
                return c2

            def wait(r, c2):
                _row_copy(zero_ref, 0, xs_ref, 0, zsem).wait()
                return c2

            lax.fori_loop(lo, hi, start, 0)
            lax.fori_loop(lo, hi, wait, 0)
            return carry

        lax.fori_loop(0, N_EXPERTS, per_expert, 0)

        def block_copy(j):
            return pltpu.make_async_copy(zero_ref, xs_ref.at[pl.ds(j * MOE_BLOCK, MOE_BLOCK)], zsem)

        first_unused = fill_hi_ref[N_EXPERTS - 1] // MOE_BLOCK
        n_blocks = xs_ref.shape[0] // MOE_BLOCK

        def start_block(j, carry):
            block_copy(j).start()
            return carry

        def wait_block(j, carry):
            block_copy(0).wait()
            return carry

        lax.fori_loop(first_unused, n_blocks, start_block, 0)
        lax.fori_loop(first_unused, n_blocks, wait_block, 0)

    def issue(u, carry):
        t0 = pl.multiple_of(u * SUBLANES, SUBLANES)
        for s in range(SUBLANES):
            for k in range(TOP_K):
                _row_copy(c_ref, t0 + s, xs_ref, dest_ref[k, t0 + s], sem).start()
        return carry

    lax.fori_loop(0, td // SUBLANES, issue, 0)
    for k in range(TOP_K):
        pltpu.make_async_copy(c_ref, xs_ref.at[pl.ds(0, td)], sem).wait()


def _dispatch(fill_lo, fill_hi, dest, c, n_slots):
    n, d = c.shape
    td = _pick(n, (512, 256))
    grid_spec = pltpu.PrefetchScalarGridSpec(
        num_scalar_prefetch=2,
        grid=(n // td,),
        in_specs=[pl.BlockSpec((TOP_K, td), lambda i, lo, hi: (0, i), memory_space=pltpu.SMEM),
                  pl.BlockSpec((td, d), lambda i, lo, hi: (i, 0))],
        out_specs=pl.BlockSpec(memory_space=pl.ANY),
        scratch_shapes=[pltpu.VMEM((MOE_BLOCK, d), F32), pltpu.SemaphoreType.DMA(()), pltpu.SemaphoreType.DMA(())],
    )
    return pl.pallas_call(
        _dispatch_body,
        grid_spec=grid_spec,
        out_shape=jax.ShapeDtypeStruct((n_slots, d), F32),
        compiler_params=pltpu.CompilerParams(dimension_semantics=("arbitrary",),
                                             vmem_limit_bytes=VMEM_LIMIT),
        name="dispatch",
    )(fill_lo, fill_hi, dest, c)


def _expert_body(be_ref, na_ref, nxt_ref, xs_ref, wg_hbm, wu_hbm, wd_hbm, yb_ref,
                 wg_ref, wu_ref, wd_ref, sg_ref, su_ref, sd_ref, wsem):
    j = pl.program_id(0)
    active = j < na_ref[0]
    e = be_ref[j]
    first = (j == 0) | (be_ref[jnp.maximum(j - 1, 0)] != e)

    def weight_copies(expert):
        return (pltpu.make_async_copy(wg_hbm.at[expert], sg_ref, wsem.at[0]),
                pltpu.make_async_copy(wu_hbm.at[expert], su_ref, wsem.at[1]),
                pltpu.make_async_copy(wd_hbm.at[expert], sd_ref, wsem.at[2]))

    @pl.when(j == 0)
    def _():
        for cp in weight_copies(e):
            cp.start()

    @pl.when(active & first)
    def _():
        for cp, stage, dst in zip(weight_copies(e), (sg_ref, su_ref, sd_ref), (wg_ref, wu_ref, wd_ref)):
            cp.wait()

            def convert(r, carry, stage=stage, dst=dst):
                rows = pl.ds(pl.multiple_of(r * MOE_BLOCK, MOE_BLOCK), MOE_BLOCK)
                dst[rows, :] = stage[rows, :].astype(BF16)
                return carry

            lax.fori_loop(0, stage.shape[0] // MOE_BLOCK, convert, 0)

        @pl.when(nxt_ref[e] < N_EXPERTS)
        def _():
            for cp in weight_copies(nxt_ref[e]):
                cp.start()

    @pl.when(active)
    def _():
        x = xs_ref[...].astype(BF16)
        a = _dot(x, wg_ref[...])
        b = _dot(x, wu_ref[...])
        hid = (a * _sigmoid(a)) * b
        yb_ref[...] = _dot(hid.astype(BF16), wd_ref[...])

    @pl.when(jnp.logical_not(active))
    def _():
        yb_ref[...] = jnp.zeros_like(yb_ref)


def _experts(block_exp, n_active, next_exp, xs, w_gate, w_up, w_down):
    n_slots, d = xs.shape
    de = w_gate.shape[2]
    nb = n_slots // MOE_BLOCK
    assert d % MOE_BLOCK == 0 and de % MOE_BLOCK == 0
    blk = lambda j, be, na, nxt: (jnp.minimum(j, na[0] - 1), 0)
    grid_spec = pltpu.PrefetchScalarGridSpec(
        num_scalar_prefetch=3,
        grid=(nb,),
        in_specs=[pl.BlockSpec((MOE_BLOCK, d), blk),
                  pl.BlockSpec(memory_space=pl.ANY),
                  pl.BlockSpec(memory_space=pl.ANY),
                  pl.BlockSpec(memory_space=pl.ANY)],
        out_specs=pl.BlockSpec((MOE_BLOCK, d), lambda j, be, na, nxt: (j, 0)),
        scratch_shapes=[pltpu.VMEM((d, de), BF16), pltpu.VMEM((d, de), BF16), pltpu.VMEM((de, d), BF16),
                        pltpu.VMEM((d, de), F32), pltpu.VMEM((d, de), F32), pltpu.VMEM((de, d), F32),
                        pltpu.SemaphoreType.DMA((3,))],
    )
    return pl.pallas_call(
        _expert_body,
        grid_spec=grid_spec,
        out_shape=jax.ShapeDtypeStruct((n_slots, d), F32),
        compiler_params=pltpu.CompilerParams(dimension_semantics=("arbitrary",),
                                             vmem_limit_bytes=VMEM_LIMIT),
        name="experts",
    )(block_exp, n_active, next_exp, xs, w_gate, w_up, w_down)


def _combine_body(dcur_ref, dnxt_ref, gate_ref, h1_ref, p_ref, yb_ref, gple_ref, wpg_hbm, wpp_hbm, gfin_ref,
                  yp_ref, ys_ref, buf_ref, sem, wpg_ref, wpp_ref, stage_ref, wsem,
                  *, n_prompt_tiles, final_norm):
    i = pl.program_id(0)

    @pl.when(i == 0)
    def _():
        _load_as_bf16(wpg_hbm, wpg_ref, stage_ref, wsem)
        _load_as_bf16(wpp_hbm, wpp_ref, stage_ref, wsem)

    n_tiles = pl.num_programs(0)
    tc = h1_ref.shape[0]
    slot = lax.rem(i, 2)

    def issue(dest_ref, into):
        def body(u, carry):
            t0 = pl.multiple_of(u * SUBLANES, SUBLANES)
            for s in range(SUBLANES):
                for k in range(TOP_K):
                    _row_copy(yb_ref, dest_ref[k, t0 + s], buf_ref.at[into, k], t0 + s, sem.at[into]).start()
            return carry

        lax.fori_loop(0, tc // SUBLANES, body, 0)

    @pl.when(i == 0)
    def _():
        issue(dcur_ref, 0)

    @pl.when(i + 1 < n_tiles)
    def _():
        issue(dnxt_ref, 1 - slot)

    for k in range(TOP_K):
        pltpu.make_async_copy(yb_ref.at[pl.ds(0, tc)], buf_ref.at[slot, k], sem.at[slot]).wait()

    gt = gate_ref[...].T
    h2 = h1_ref[...] + (gt[:, 0:1] * buf_ref[slot, 0] + gt[:, 1:2] * buf_ref[slot, 1])
    a = _rms(h2, gple_ref[...]).astype(BF16)
    h3 = h2 + _sigmoid(_dot(a, wpg_ref[...])) * _dot(p_ref[...].astype(BF16), wpp_ref[...])
    if final_norm:
        h3 = _rms(h3, gfin_ref[...])

    @pl.when(i < n_prompt_tiles)
    def _():
        yp_ref[...] = h3

    @pl.when(i >= n_prompt_tiles)
    def _():
        ys_ref[...] = h3


def _combine(dest, gates, h1, p, yb, g_ple, w_pg, w_pp, g_final, *, n_prompt, final_norm):
    n, d = h1.shape
    ple = p.shape[1]
    tc = 256
    assert n_prompt % tc == 0 and (n - n_prompt) % tc == 0
    nt, npt = n // tc, n_prompt // tc
    nst = nt - npt
    const2 = lambda i: (0, 0)
    return pl.pallas_call(
        functools.partial(_combine_body, n_prompt_tiles=npt, final_norm=final_norm),
        grid=(nt,),
        in_specs=[pl.BlockSpec((TOP_K, tc), lambda i: (0, i), memory_space=pltpu.SMEM),
                  pl.BlockSpec((TOP_K, tc), lambda i: (0, jnp.minimum(i + 1, nt - 1)), memory_space=pltpu.SMEM),
                  pl.BlockSpec((SUBLANES, tc), lambda i: (0, i)),
                  pl.BlockSpec((tc, d), lambda i: (i, 0)),
                  pl.BlockSpec((tc, ple), lambda i: (i, 0)),
                  pl.BlockSpec(memory_space=pl.ANY),
                  pl.BlockSpec((1, d), const2),
                  pl.BlockSpec(memory_space=pl.ANY),
                  pl.BlockSpec(memory_space=pl.ANY),
                  pl.BlockSpec((1, d), const2)],
        out_specs=[pl.BlockSpec((tc, d), lambda i: (jnp.minimum(i, npt - 1), 0)),
                   pl.BlockSpec((tc, d), lambda i: (jnp.clip(i - npt, 0, nst - 1), 0))],
        out_shape=[jax.ShapeDtypeStruct((n_prompt, d), F32),
                   jax.ShapeDtypeStruct((n - n_prompt, d), F32)],
        scratch_shapes=[pltpu.VMEM((2, TOP_K, tc, d), F32), pltpu.SemaphoreType.DMA((2,)),
                        pltpu.VMEM((d, d), BF16), pltpu.VMEM((ple, d), BF16),
                        pltpu.VMEM((2, WEIGHT_STAGE_ROWS, d), F32), pltpu.SemaphoreType.DMA((2,))],
        compiler_params=pltpu.CompilerParams(dimension_semantics=("arbitrary",),
                                             vmem_limit_bytes=VMEM_LIMIT),
        name="combine",
    )(dest, dest, gates, h1, p, yb, g_ple, w_pg, w_pp, g_final)


def _layer(hp, hs, p, s0_sample, lb, batch, seq, dec_batch, dec_seq,
           g_mix, w_in, g_head, w_pa, ln_v_g, ln_v_b, w_s, b_s, w_pb, w_o,
           g_ffn, w_gr, b_gr, w_er, b_er, w_gate, w_up, w_down, g_ple, w_pg, w_pp, g_final, final_norm):
    n_prompt, d = hp.shape
    n = n_prompt + hs.shape[0]
    n_heads = d // HEAD_DIM
    row = lambda a: a.reshape(1, -1).astype(F32)

    z = _in_proj(hp, hs, row(g_mix), w_in)

    lb_row, gh_row = row(lb), row(g_head)
    s0_prompt = jnp.zeros((batch, n_heads, HEAD_DIM, HEAD_DIM), F32)
    og_p, st_p = _hgrn(z, s0_prompt, lb_row, gh_row, row_base=0, batch=batch, seq=seq,
                       d_model=d, name="hgrn_prompt")
    og_s, st_s = _hgrn(z, s0_sample.astype(F32), lb_row, gh_row, row_base=n_prompt, batch=dec_batch,
                       seq=dec_seq, d_model=d, name="hgrn_sample")

    start = PAST_LEN % MLP_CHUNK
    assert start + dec_seq <= MLP_CHUNK
    rep = MLP_CHUNK // dec_seq
    ws_s = jnp.tile(w_s[:, start:start + dec_seq, start:start + dec_seq], (1, rep, rep))
    bs_s = jnp.tile(b_s[:, start:start + dec_seq], (1, rep))
    h1, vn_p, vn_s = _mix(z, og_p, og_s, hp, hs, row(ln_v_g), row(ln_v_b), w_s, ws_s, b_s[..., None],
                          bs_s[..., None], w_pa, w_pb, w_o,
                          batch=batch, dec_seq=dec_seq)

    pad_rows = ROUTER_ROWS - N_EXPERTS - N_GROUPS
    wr = jnp.concatenate([w_er.T, w_gr.T, jnp.zeros((pad_rows, d), F32)], axis=0)
    br = jnp.concatenate([b_er, b_gr, jnp.zeros((pad_rows,), F32)]).reshape(ROUTER_ROWS, 1).astype(F32)
    c, idx, gates, cnt = _router(h1, row(g_ffn), wr, br)

    counts = cnt[:, 0]
    padded = (counts + MOE_BLOCK - 1) // MOE_BLOCK * MOE_BLOCK
    pad_end = jnp.cumsum(padded).astype(jnp.int32)
    pad_start = pad_end - padded
    n_blocks = -(-(n * TOP_K) // MOE_BLOCK) + N_EXPERTS
    block_first = jnp.arange(n_blocks, dtype=jnp.int32) * MOE_BLOCK
    block_exp = jnp.minimum(jnp.sum(pad_end[None, :] <= block_first[:, None], axis=1), N_EXPERTS - 1).astype(jnp.int32)
    n_active = jnp.maximum(pad_end[-1:] // MOE_BLOCK, 1)
    hit = idx[0:TOP_K, :, None] == jnp.arange(N_EXPERTS, dtype=jnp.int32)
    dest = jnp.sum(jnp.where(hit, pad_start, 0), axis=-1) + idx[TOP_K:2 * TOP_K]

    xs = _dispatch(pad_start + counts, pad_end, dest, c, n_blocks * MOE_BLOCK)
    used = jnp.where(counts > 0, jnp.arange(N_EXPERTS, dtype=jnp.int32), N_EXPERTS)
    next_exp = jnp.concatenate([lax.cummin(used, axis=0, reverse=True)[1:],
                                jnp.full((1,), N_EXPERTS, jnp.int32)]).astype(jnp.int32)
    yb = _experts(block_exp, n_active, next_exp, xs, w_gate, w_up, w_down)
    yp, ys = _combine(dest, gates, h1, p, yb, row(g_ple), w_pg, w_pp,
                      row(g_final), n_prompt=n_prompt, final_norm=final_norm)
    return yp, ys, st_p, st_s, vn_p, vn_s


def kernel(x_prompt, x_sample, p_prompt, p_sample, state_hgrn, g_mix, w_in, lb_logits, g_head, w_pa, ln_v_g, ln_v_b, w_s, b_s, w_pb, w_o, g_ffn, w_gr, b_gr, w_er, b_er, w_gate, w_up, w_down, g_ple, w_pg, w_pp, g_final):
    batch, seq, d = x_prompt.shape
    dec_batch, dec_seq, _ = x_sample.shape
    depth = g_mix.shape[0]
    n_prompt, n_sample = batch * seq, dec_batch * dec_seq
    width = w_pb.shape[1]
    lbs = jnp.cumsum(jax.nn.softmax(lb_logits.astype(F32), axis=0), axis=0)
    hp, hs = x_prompt.reshape(n_prompt, d), x_sample.reshape(n_sample, d)
    keep = min((seq - 1) % MLP_CHUNK + 1, seq)
    keep_s = min((PAST_LEN % MLP_CHUNK + dec_seq - 1) % MLP_CHUNK + 1, dec_seq)
    sp, ss, vp, vs = [], [], [], []
    for i in range(depth):
        p = jnp.concatenate([p_prompt[i].reshape(n_prompt, -1), p_sample[i].reshape(n_sample, -1)], axis=0)
        hp, hs, st_p, st_s, vn_p, vn_s = _layer(
            hp, hs, p, state_hgrn[i], lbs[i], batch, seq, dec_batch, dec_seq,
            g_mix[i], w_in[i], g_head[i], w_pa[i], ln_v_g[i], ln_v_b[i], w_s[i], b_s[i], w_pb[i], w_o[i],
            g_ffn[i], w_gr[i], b_gr[i], w_er[i], b_er[i], w_gate[i], w_up[i], w_down[i],
            g_ple[i], w_pg[i], w_pp[i], g_final, i == depth - 1)
        sp.append(st_p.astype(x_prompt.dtype))
        ss.append(st_s.astype(state_hgrn.dtype))
        vn_last = vn_p.reshape(batch, -1, width)
        vp.append(vn_last[:, vn_last.shape[1] - keep:])
        vs.append(vn_s.reshape(dec_batch, dec_seq, width)[:, dec_seq - keep_s:])
    y_prompt = hp.reshape(batch, seq, d)
    y_sample = hs.reshape(dec_batch, dec_seq, d)
    return (y_prompt, y_sample, jnp.stack(sp), jnp.stack(ss), jnp.stack(vp), jnp.stack(vs))
```

```python
import functools
import math

import jax
import jax.numpy as jnp
from jax import lax
from jax.experimental import pallas as pl
from jax.experimental.pallas import tpu as pltpu

F32 = jnp.float32
BF16 = jnp.bfloat16

EPS = 1e-6
HEAD_DIM = 128
REC_CHUNK = 64
REC_HALF = 32
MLP_CHUNK = 128
MLP_GROUPS = 4
N_GROUPS = 4
E_PER_GROUP = 8
N_EXPERTS = N_GROUPS * E_PER_GROUP
TOP_K = 2
MOE_BLOCK = 256
PAST_LEN = 2048
LANES = 128
SUBLANES = 8
ROUTER_ROWS = 40
VMEM_LIMIT = 56 * 1024 * 1024
LOG2_E = 1.4426950408889634
WEIGHT_STAGE_ROWS = 256


def _pick(n, candidates):
    for c in candidates:
        if n % c == 0:
            return c
    raise ValueError(f"no tile in {candidates} divides {n}")


def _dot(a, b):
    return jnp.dot(a, b, preferred_element_type=F32)


def _dot_nt(a, b, precision=None):
    return lax.dot_general(a, b, (((1,), (1,)), ((), ())), precision=precision, preferred_element_type=F32)


def _dot_tn(a, b):
    return lax.dot_general(a, b, (((0,), (0,)), ((), ())), preferred_element_type=F32)


def _rms(x, g):
    return x * lax.rsqrt(jnp.mean(x * x, axis=-1, keepdims=True) + EPS) * g


def _gelu(x):
    c = math.sqrt(2.0 / math.pi)
    return x * (0.5 * (1.0 + jnp.tanh(c * (x + 0.044715 * (x * x * x)))))


def _sigmoid(x):
    return 0.5 * jnp.tanh(0.5 * x) + 0.5


def _store_chunks(ref, x):
    for c in range(ref.shape[0]):
        ref[c] = x[:, c * LANES:(c + 1) * LANES].astype(ref.dtype)


def _load_chunks(ref):
    return jnp.concatenate([ref[c] for c in range(ref.shape[0])], axis=1)


def _row_copy(src_ref, src_row, dst_ref, dst_row, sem):
    return pltpu.make_async_copy(src_ref.at[:, pl.ds(src_row, 1), :], dst_ref.at[:, pl.ds(dst_row, 1), :], sem)


def _rows_copy(src_ref, dst_ref, dst_row, n_rows, sem):
    return pltpu.make_async_copy(src_ref, dst_ref.at[:, pl.ds(dst_row, n_rows), :], sem)


def _load_as_bf16(w_hbm, w_vmem, stage_ref, sem):
    rows = min(stage_ref.shape[1], w_hbm.shape[0])
    assert w_hbm.shape[0] % rows == 0
    n_chunks = w_hbm.shape[0] // rows

    def copy(c):
        return pltpu.make_async_copy(w_hbm.at[pl.ds(c * rows, rows)],
                                     stage_ref.at[c % 2, pl.ds(0, rows)], sem.at[c % 2])

    copy(0).start()
    for c in range(n_chunks):
        if c + 1 < n_chunks:
            copy(c + 1).start()
        copy(c).wait()
        w_vmem[pl.ds(c * rows, rows), :] = stage_ref[c % 2, pl.ds(0, rows)].astype(BF16)


def _in_proj_body(xp_ref, xs_ref, g_ref, w_ref, z_ref, xn_ref, *, n_prompt_tiles):
    first = pl.program_id(1) == 0
    is_p = pl.program_id(0) < n_prompt_tiles

    @pl.when(first & is_p)
    def _():
        xn_ref[...] = _rms(xp_ref[...], g_ref[...]).astype(BF16)

    @pl.when(first & jnp.logical_not(is_p))
    def _():
        xn_ref[...] = _rms(xs_ref[...], g_ref[...]).astype(BF16)

    z_ref[...] = _dot(xn_ref[...], w_ref[...].astype(BF16)).astype(z_ref.dtype)


def _in_proj(xp, xs, g, w):
    (n_p, d), n_s = xp.shape, xs.shape[0]
    width = w.shape[1]
    tm = _pick(math.gcd(n_p, n_s), (1024, 512, 256))
    tn = _pick(width, (1024, 512, 256, 128))
    npt, nst = n_p // tm, n_s // tm
    return pl.pallas_call(
        functools.partial(_in_proj_body, n_prompt_tiles=npt),
        grid=(npt + nst, width // tn),
        in_specs=[pl.BlockSpec((tm, d), lambda i, j: (jnp.minimum(i, npt - 1), 0)),
                  pl.BlockSpec((tm, d), lambda i, j: (jnp.clip(i - npt, 0, nst - 1), 0),
                               pipeline_mode=pl.Buffered(1)),
                  pl.BlockSpec((1, d), lambda i, j: (0, 0)),
                  pl.BlockSpec((d, tn), lambda i, j: (0, j))],
        out_specs=pl.BlockSpec((tm, tn), lambda i, j: (i, j)),
        out_shape=jax.ShapeDtypeStruct((n_p + n_s, width), BF16),
        scratch_shapes=[pltpu.VMEM((tm, d), BF16)],
        compiler_params=pltpu.CompilerParams(dimension_semantics=("parallel", "arbitrary"),
                                             vmem_limit_bytes=VMEM_LIMIT),
        name="in_proj",
    )(xp, xs, g, w)


def _cumsum_rows(x, row):
    d = 1
    while d < x.shape[0]:
        x = x + jnp.where(row >= d, pltpu.roll(x, d, 0), 0.0)
        d *= 2
    return x


def _hgrn_body(q_ref, f_ref, i_ref, zo_ref, s0_ref, lb_ref, gh_ref, og_ref, sout_ref,
               st_ref, qs_ref, qe_ref, kd_ref, dec_ref, a_ref, o_ref, *key_refs, chunk, n_chunks, heads):
    ti = pl.program_id(2)

    @pl.when(ti == 0)
    def _():
        for h in range(heads):
            st_ref[h] = s0_ref[0, h].T

    half = min(REC_HALF, chunk)
    n_half = chunk // half
    row = lax.broadcasted_iota(jnp.int32, (chunk, HEAD_DIM), 0)
    masks = []
    for g in range(n_half):
        r = lax.broadcasted_iota(jnp.int32, (half, half * (g + 1)), 0)
        c = lax.broadcasted_iota(jnp.int32, (half, half * (g + 1)), 1)
        masks.append(r + g * half >= c)

    def chunk_step(ci, carry):
        r0 = pl.multiple_of(ci * chunk, chunk)
        rows = pl.ds(r0, chunk)
        for h in range(heads):
            hs = slice(h * HEAD_DIM, (h + 1) * HEAD_DIM)
            q = q_ref[rows, hs].astype(F32)
            lb = lb_ref[:, hs]
            c1 = 0.5 * (1.0 - lb)
            f = (lb + c1) + c1 * jnp.tanh(0.5 * f_ref[rows, hs].astype(F32))
            kk = 1.0 - f
            b = _cumsum_rows(jnp.log(f), row) * LOG2_E
            b_last = b[chunk - 1:chunk, :]
            dec_ref[0:1, hs] = jnp.exp2(b_last)
            mids, ks = [], []
            for g in range(n_half):
                rg = slice(g * half, (g + 1) * half)
                mid = b[g * half + half // 2 - 1:g * half + half // 2, :]
                qs_g = q[rg] * jnp.exp2(b[rg] - mid)
                ks_g = kk[rg] * jnp.exp2(mid - b[rg])
                mids.append(mid)
                ks.append(ks_g)
                qs_ref[rg, hs] = qs_g.astype(BF16)
                qe_ref[rg, hs] = (qs_g * jnp.exp2(mid)).astype(BF16)
                kd_ref[rg, hs] = (ks_g * jnp.exp2(b_last - mid)).astype(BF16)
                for gp in range(g):
                    key_refs[g][gp * half:(gp + 1) * half, hs] = (
                        ks[gp] * jnp.exp2(mid - mids[gp])).astype(BF16)
                key_refs[g][rg, hs] = ks_g.astype(BF16)
        for h in range(heads):
            hs = slice(h * HEAD_DIM, (h + 1) * HEAD_DIM)
            for g in range(n_half):
                rg = slice(g * half, (g + 1) * half)
                a = _dot_nt(qs_ref[rg, hs], key_refs[g][:, hs])
                a_ref[h, rg, 0:half * (g + 1)] = jnp.where(masks[g], a, 0.0).astype(BF16)
        for h in range(heads):
            hs = slice(h * HEAD_DIM, (h + 1) * HEAD_DIM)
            v = i_ref[rows, hs]
            st = st_ref[h]
            o_inter = _dot_nt(qe_ref[:, hs], st.astype(BF16))
            for g in range(n_half):
                rg = slice(g * half, (g + 1) * half)
                o_ref[rg, hs] = o_inter[rg] + _dot(a_ref[h, rg, 0:half * (g + 1)], v[0:half * (g + 1)])
            st_ref[h] = st * dec_ref[0:1, hs] + _dot_tn(v, kd_ref[:, hs])
        for h in range(heads):
            hs = slice(h * HEAD_DIM, (h + 1) * HEAD_DIM)
            o = o_ref[:, hs]
            o_n = o * lax.rsqrt(jnp.mean(o * o, axis=-1, keepdims=True) + EPS) * (0.5 * gh_ref[:, hs])
            gate2 = jnp.tanh(0.5 * zo_ref[rows, hs].astype(F32)) + 1.0
            og_ref[rows, hs] = (o_n * gate2).astype(og_ref.dtype)
        return carry

    lax.fori_loop(0, n_chunks, chunk_step, 0)

    @pl.when(ti == pl.num_programs(2) - 1)
    def _():
        for h in range(heads):
            sout_ref[0, h] = st_ref[h].T


def _hgrn(z, s0, lb, g_head, *, row_base, batch, seq, d_model, name):
    n_heads = d_model // HEAD_DIM
    heads = min(16, n_heads)
    chunk = min(REC_CHUNK, seq)
    half = min(REC_HALF, chunk)
    assert seq % chunk == 0 and chunk % half == 0
    rt = max(_pick(seq, (512, 256, 128, 64, 32)), chunk)
    assert row_base % rt == 0
    tiles = seq // rt
    wblk = heads * HEAD_DIM
    cpb = d_model // wblk

    def zspec(section):
        return pl.BlockSpec((rt, wblk),
                            lambda b, hg, i: (row_base // rt + b * tiles + i, section * cpb + hg))

    return pl.pallas_call(
        functools.partial(_hgrn_body, chunk=chunk, n_chunks=rt // chunk, heads=heads),
        grid=(batch, n_heads // heads, tiles),
        in_specs=[zspec(0), zspec(1), zspec(2), zspec(3),
                  pl.BlockSpec((1, heads, HEAD_DIM, HEAD_DIM), lambda b, hg, i: (b, hg, 0, 0)),
                  pl.BlockSpec((1, wblk), lambda b, hg, i: (0, hg)),
                  pl.BlockSpec((1, wblk), lambda b, hg, i: (0, hg))],
        out_specs=[pl.BlockSpec((rt, wblk), lambda b, hg, i: (b * tiles + i, hg)),
                   pl.BlockSpec((1, heads, HEAD_DIM, HEAD_DIM), lambda b, hg, i: (b, hg, 0, 0))],
        out_shape=[jax.ShapeDtypeStruct((batch * seq, d_model), BF16),
                   jax.ShapeDtypeStruct((batch, n_heads, HEAD_DIM, HEAD_DIM), F32)],
        scratch_shapes=[pltpu.VMEM((heads, HEAD_DIM, HEAD_DIM), F32),
                        pltpu.VMEM((chunk, wblk), BF16),
                        pltpu.VMEM((chunk, wblk), BF16),
                        pltpu.VMEM((chunk, wblk), BF16),
                        pltpu.VMEM((SUBLANES, wblk), F32),
                        pltpu.VMEM((heads, chunk, chunk), BF16),
                        pltpu.VMEM((chunk, wblk), F32)]
                       + [pltpu.VMEM((half * (g + 1), wblk), BF16) for g in range(chunk // half)],
        compiler_params=pltpu.CompilerParams(dimension_semantics=("parallel", "parallel", "arbitrary"),
                                             vmem_limit_bytes=VMEM_LIMIT),
        name=name,
    )(z, z, z, z, s0, lb, g_head)


def _mix_body(zu_ref, zv_ref, zga_ref, zgb_ref, ogp_ref, ogs_ref, xp_ref, xs_ref, lng_ref, lnb_ref,
              wsp_ref, wss_ref, bsp_ref, bss_ref, wpa_hbm, wpb_hbm, wo_hbm,
              h1_ref, vnp_ref, vns_ref, sg_ref, wpa_ref, wpb_ref, wo_ref, stage_ref, wsem,
              *, n_prompt_tiles, dec_seq):
    @pl.when(pl.program_id(0) == 0)
    def _():
        _load_as_bf16(wpa_hbm, wpa_ref, stage_ref, wsem)
        _load_as_bf16(wpb_hbm, wpb_ref, stage_ref, wsem)
        _load_as_bf16(wo_hbm, wo_ref, stage_ref, wsem)

    is_p = pl.program_id(0) < n_prompt_tiles
    tm, width = zu_ref.shape
    gd = width // MLP_GROUPS
    u = _gelu(zu_ref[...].astype(F32))
    gv = _gelu(zv_ref[...].astype(F32))
    xc = gv - jnp.mean(gv, axis=-1, keepdims=True)
    vn = xc * lax.rsqrt(jnp.mean(xc * xc, axis=-1, keepdims=True) + EPS) * lng_ref[...] + lnb_ref[...]

    @pl.when(is_p)
    def _():
        vnp_ref[...] = vn

    @pl.when(jnp.logical_not(is_p))
    def _():
        vns_ref[...] = vn

    vnb = vn.astype(BF16)
    r = lax.broadcasted_iota(jnp.int32, (MLP_CHUNK, MLP_CHUNK), 0)
    c = lax.broadcasted_iota(jnp.int32, (MLP_CHUNK, MLP_CHUNK), 1)
    causal = r >= c
    same_stream = (r // dec_seq) == (c // dec_seq)
    for g in range(MLP_GROUPS):
        w_p = jnp.where(causal, wsp_ref[g], 0.0)
        w_s = jnp.where(causal & same_stream, wss_ref[g], 0.0)
        w = jnp.where(is_p, w_p, w_s).astype(BF16)
        bias = jnp.where(is_p, bsp_ref[g], bss_ref[g])
        for cc in range(tm // MLP_CHUNK):
            rows = slice(cc * MLP_CHUNK, (cc + 1) * MLP_CHUNK)
            cols = slice(g * gd, (g + 1) * gd)
            s = _dot(w, vnb[rows, cols]) + bias
            sg_ref[rows, cols] = (u[rows, cols] * s).astype(BF16)
    y_b = _dot(sg_ref[...], wpb_ref[...])
    og = jnp.where(is_p, ogp_ref[...].astype(F32), ogs_ref[...].astype(F32)).astype(BF16)
    y_a = _dot(og, wpa_ref[...])
    m = _sigmoid(zga_ref[...].astype(F32)) * y_a + _sigmoid(zgb_ref[...].astype(F32)) * y_b
    x = jnp.where(is_p, xp_ref[...], xs_ref[...])
    h1_ref[...] = x + _dot(m.astype(BF16), wo_ref[...])


def _mix(z, og_p, og_s, xp, xs, ln_g, ln_b, ws_p, ws_s, bs_p, bs_s, w_pa, w_pb, w_o, *, batch, dec_seq):
    (n_p, d), n_s = xp.shape, xs.shape[0]
    n = n_p + n_s
    width = w_pb.shape[0]
    tm = 256
    assert n_s % tm == 0 and (n_p // batch) % tm == 0 and tm % MLP_CHUNK == 0 and MLP_CHUNK % dec_seq == 0
    npt, nst = n_p // tm, n_s // tm
    tpb = npt // batch
    u_blk = 4 * d // width
    ga_blk = (4 * d + 2 * width) // d
    const2 = lambda i: (0, 0)
    const3 = lambda i: (0, 0, 0)
    p_map = lambda i: (jnp.minimum(i, npt - 1), 0)
    s_map = lambda i: (jnp.clip(i - npt, 0, nst - 1), 0)
    return pl.pallas_call(
        functools.partial(_mix_body, n_prompt_tiles=npt, dec_seq=dec_seq),
        grid=(n // tm,),
        in_specs=[pl.BlockSpec((tm, width), lambda i: (i, u_blk)),
                  pl.BlockSpec((tm, width), lambda i: (i, u_blk + 1)),
                  pl.BlockSpec((tm, d), lambda i: (i, ga_blk)),
                  pl.BlockSpec((tm, d), lambda i: (i, ga_blk + 1)),
                  pl.BlockSpec((tm, d), p_map),
                  pl.BlockSpec((tm, d), s_map),
                  pl.BlockSpec((tm, d), p_map),
                  pl.BlockSpec((tm, d), s_map),
                  pl.BlockSpec((1, width), const2),
                  pl.BlockSpec((1, width), const2),
                  pl.BlockSpec((MLP_GROUPS, MLP_CHUNK, MLP_CHUNK), const3),
                  pl.BlockSpec((MLP_GROUPS, MLP_CHUNK, MLP_CHUNK), const3),
                  pl.BlockSpec((MLP_GROUPS, MLP_CHUNK, 1), const3),
                  pl.BlockSpec((MLP_GROUPS, MLP_CHUNK, 1), const3),
                  pl.BlockSpec(memory_space=pl.ANY),
                  pl.BlockSpec(memory_space=pl.ANY),
                  pl.BlockSpec(memory_space=pl.ANY)],
        out_specs=[pl.BlockSpec((tm, d), lambda i: (i, 0)),
                   pl.BlockSpec((tm, width), lambda i: (jnp.minimum(i // tpb, batch - 1), 0)),
                   pl.BlockSpec((tm, width), s_map)],
        out_shape=[jax.ShapeDtypeStruct((n, d), F32),
                   jax.ShapeDtypeStruct((batch * tm, width), F32),
                   jax.ShapeDtypeStruct((n_s, width), F32)],
        scratch_shapes=[pltpu.VMEM((tm, width), BF16),
                        pltpu.VMEM((d, d), BF16), pltpu.VMEM((width, d), BF16), pltpu.VMEM((d, d), BF16),
                        pltpu.VMEM((2, WEIGHT_STAGE_ROWS, d), F32), pltpu.SemaphoreType.DMA((2,))],
        compiler_params=pltpu.CompilerParams(dimension_semantics=("arbitrary",),
                                             vmem_limit_bytes=VMEM_LIMIT),
        name="mix",
    )(z, z, z, z, og_p, og_s, xp, xs, ln_g, ln_b, ws_p, ws_s, bs_p, bs_s, w_pa, w_pb, w_o)


def _router_body(h_ref, g_ref, wr_ref, br_ref, c_ref, idx_ref, gate_ref, cnt_ref, carry_ref):
    @pl.when(pl.program_id(0) == 0)
    def _():
        carry_ref[...] = jnp.zeros_like(carry_ref)

    tm = h_ref.shape[0]
    c = _rms(h_ref[...], g_ref[...])
    _store_chunks(c_ref, c)
    lt = _dot_nt(wr_ref[...], c, precision=lax.Precision.HIGHEST) + br_ref[...]
    le = lt[0:N_EXPERTS]
    lg = lt[N_EXPERTS:N_EXPERTS + N_GROUPS]
    gmax = jnp.max(lg, axis=0, keepdims=True)
    p_sel = 1.0 / jnp.sum(jnp.exp(lg - gmax), axis=0, keepdims=True)
    best = lg[0:1]
    gi = jnp.zeros((1, tm), jnp.int32)
    for g in range(1, N_GROUPS):
        better = lg[g:g + 1] > best
        gi = jnp.where(better, g, gi)
        best = jnp.where(better, lg[g:g + 1], best)
    leg = jnp.zeros((E_PER_GROUP, tm), F32)
    for g in range(N_GROUPS):
        leg = jnp.where(gi == g, le[g * E_PER_GROUP:(g + 1) * E_PER_GROUP], leg)
    sub = lax.broadcasted_iota(jnp.int32, (E_PER_GROUP, tm), 0).astype(F32)
    v1 = jnp.max(leg, axis=0, keepdims=True)
    i1 = jnp.min(jnp.where(leg == v1, sub, float(E_PER_GROUP)), axis=0, keepdims=True)
    rest = jnp.where(sub == i1, -jnp.inf, leg)
    v2 = jnp.max(rest, axis=0, keepdims=True)
    i2 = jnp.min(jnp.where(rest == v2, sub, float(E_PER_GROUP)), axis=0, keepdims=True)
    e2 = jnp.exp(v2 - v1)
    den = 1.0 + e2
    gate0 = p_sel * (1.0 / den)
    gate1 = p_sel * (e2 / den)
    ex0 = gi * E_PER_GROUP + i1.astype(jnp.int32)
    ex1 = gi * E_PER_GROUP + i2.astype(jnp.int32)
    eid = lax.broadcasted_iota(jnp.int32, (N_EXPERTS, tm), 0)
    oh0 = eid == ex0
    oh1 = eid == ex1
    oh = jnp.where(oh0 | oh1, 1.0, 0.0)
    upper = jnp.where(lax.broadcasted_iota(jnp.int32, (tm, tm), 0) < lax.broadcasted_iota(jnp.int32, (tm, tm), 1),
                      1.0, 0.0).astype(BF16)
    before = _dot(oh.astype(BF16), upper) + carry_ref[:, 0:1]
    rank0 = jnp.sum(jnp.where(oh0, before, 0.0), axis=0, keepdims=True)
    rank1 = jnp.sum(jnp.where(oh1, before, 0.0), axis=0, keepdims=True)
    carry = carry_ref[...] + jnp.sum(oh, axis=1, keepdims=True)
    carry_ref[...] = carry
    cnt_ref[...] = carry.astype(jnp.int32)
    idx_ref[...] = jnp.zeros_like(idx_ref)
    idx_ref[0:1, :] = ex0
    idx_ref[1:2, :] = ex1
    idx_ref[2:3, :] = rank0.astype(jnp.int32)
    idx_ref[3:4, :] = rank1.astype(jnp.int32)
    gate_ref[...] = jnp.zeros_like(gate_ref)
    gate_ref[0:1, :] = gate0
    gate_ref[1:2, :] = gate1


def _router(h1, g_ffn, wr, br):
    n, d = h1.shape
    tm = _pick(n, (512, 256))
    return pl.pallas_call(
        _router_body,
        grid=(n // tm,),
        in_specs=[pl.BlockSpec((tm, d), lambda i: (i, 0)),
                  pl.BlockSpec((1, d), lambda i: (0, 0)),
                  pl.BlockSpec((ROUTER_ROWS, d), lambda i: (0, 0)),
                  pl.BlockSpec((ROUTER_ROWS, 1), lambda i: (0, 0))],
        out_specs=[pl.BlockSpec((d // LANES, tm, LANES), lambda i: (0, i, 0)),
                   pl.BlockSpec((SUBLANES, tm), lambda i: (0, i)),
                   pl.BlockSpec((SUBLANES, tm), lambda i: (0, i)),
                   pl.BlockSpec((N_EXPERTS, LANES), lambda i: (0, 0))],
        out_shape=[jax.ShapeDtypeStruct((d // LANES, n, LANES), F32),
                   jax.ShapeDtypeStruct((SUBLANES, n), jnp.int32),
                   jax.ShapeDtypeStruct((SUBLANES, n), F32),
                   jax.ShapeDtypeStruct((N_EXPERTS, LANES), jnp.int32)],
        scratch_shapes=[pltpu.VMEM((N_EXPERTS, LANES), F32)],
        compiler_params=pltpu.CompilerParams(dimension_semantics=("arbitrary",),
                                             vmem_limit_bytes=VMEM_LIMIT),
        name="router",
    )(h1, g_ffn, wr, br)


def _dispatch_body(fill_lo_ref, fill_hi_ref, dest_ref, c_ref, xs_ref, zero_ref, sem, zsem):
    td = c_ref.shape[1]

    @pl.when(pl.program_id(0) == 0)
    def _():
        zero_ref[...] = jnp.zeros_like(zero_ref)

        def per_expert(e, carry):
            lo, hi = fill_lo_ref[e], fill_hi_ref[e]

            def start(r, c2):
                _row_copy(zero_ref, 0, xs_ref, r, zsem).start()
                return c2

            def wait(r, c2):
                _row_copy(zero_ref, 0, xs_ref, 0, zsem).wait()
                return c2

            lax.fori_loop(lo, hi, start, 0)
            lax.fori_loop(lo, hi, wait, 0)
            return carry

        lax.fori_loop(0, N_EXPERTS, per_expert, 0)

        def block_copy(j):
            return _rows_copy(zero_ref, xs_ref, j * MOE_BLOCK, MOE_BLOCK, zsem)

        first_unused = fill_hi_ref[N_EXPERTS - 1] // MOE_BLOCK
        n_blocks = xs_ref.shape[1] // MOE_BLOCK

        def start_block(j, carry):
            block_copy(j).start()
            return carry

        def wait_block(j, carry):
            block_copy(0).wait()
            return carry

        lax.fori_loop(first_unused, n_blocks, start_block, 0)
        lax.fori_loop(first_unused, n_blocks, wait_block, 0)

    for t in range(td):
        for k in range(TOP_K):
            _row_copy(c_ref, t, xs_ref, dest_ref[k, t], sem).start()
    for k in range(TOP_K):
        _rows_copy(c_ref, xs_ref, 0, td, sem).wait()


def _dispatch(fill_lo, fill_hi, dest, c, n_slots):
    nc, n, _ = c.shape
    td = _pick(n, (512, 256))
    grid_spec = pltpu.PrefetchScalarGridSpec(
        num_scalar_prefetch=2,
        grid=(n // td,),
        in_specs=[pl.BlockSpec((TOP_K, td), lambda i, lo, hi: (0, i), memory_space=pltpu.SMEM),
                  pl.BlockSpec((nc, td, LANES), lambda i, lo, hi: (0, i, 0))],
        out_specs=pl.BlockSpec(memory_space=pl.ANY),
        scratch_shapes=[pltpu.VMEM((nc, MOE_BLOCK, LANES), F32),
                        pltpu.SemaphoreType.DMA(()), pltpu.SemaphoreType.DMA(())],
    )
    return pl.pallas_call(
        _dispatch_body,
        grid_spec=grid_spec,
        out_shape=jax.ShapeDtypeStruct((nc, n_slots, LANES), F32),
        compiler_params=pltpu.CompilerParams(dimension_semantics=("arbitrary",),
                                             vmem_limit_bytes=VMEM_LIMIT),
        name="dispatch",
    )(fill_lo, fill_hi, dest, c)


def _expert_body(be_ref, na_ref, nxt_ref, xs_ref, wg_hbm, wu_hbm, wd_hbm, yb_ref,
                 wg_ref, wu_ref, wd_ref, sg_ref, su_ref, sd_ref, wsem):
    j = pl.program_id(0)
    active = j < na_ref[0]
    e = be_ref[j]
    first = (j == 0) | (be_ref[jnp.maximum(j - 1, 0)] != e)

    def weight_copies(expert):
        return (pltpu.make_async_copy(wg_hbm.at[expert], sg_ref, wsem.at[0]),
                pltpu.make_async_copy(wu_hbm.at[expert], su_ref, wsem.at[1]),
                pltpu.make_async_copy(wd_hbm.at[expert], sd_ref, wsem.at[2]))

    @pl.when(j == 0)
    def _():
        for cp in weight_copies(e):
            cp.start()

    @pl.when(active & first)
    def _():
        for cp, stage, dst in zip(weight_copies(e), (sg_ref, su_ref, sd_ref), (wg_ref, wu_ref, wd_ref)):
            cp.wait()

            def convert(r, carry, stage=stage, dst=dst):
                rows = pl.ds(pl.multiple_of(r * LANES, LANES), LANES)
                dst[rows, :] = stage[rows, :].astype(BF16)
                return carry

            lax.fori_loop(0, stage.shape[0] // LANES, convert, 0)

        @pl.when(nxt_ref[e] < N_EXPERTS)
        def _():
            for cp in weight_copies(nxt_ref[e]):
                cp.start()

    @pl.when(active)
    def _():
        x = _load_chunks(xs_ref).astype(BF16)
        a = _dot(x, wg_ref[...])
        b = _dot(x, wu_ref[...])
        hid = (a * _sigmoid(a)) * b
        _store_chunks(yb_ref, _dot(hid.astype(BF16), wd_ref[...]))

    @pl.when(jnp.logical_not(active))
    def _():
        yb_ref[...] = jnp.zeros_like(yb_ref)


def _experts(block_exp, n_active, next_exp, xs, w_gate, w_up, w_down):
    nc, n_slots, _ = xs.shape
    d, de = w_gate.shape[1:]
    nb = n_slots // MOE_BLOCK
    assert d % LANES == 0 and de % LANES == 0
    blk = lambda j, be, na, nxt: (0, jnp.minimum(j, na[0] - 1), 0)
    grid_spec = pltpu.PrefetchScalarGridSpec(
        num_scalar_prefetch=3,
        grid=(nb,),
        in_specs=[pl.BlockSpec((nc, MOE_BLOCK, LANES), blk),
                  pl.BlockSpec(memory_space=pl.ANY),
                  pl.BlockSpec(memory_space=pl.ANY),
                  pl.BlockSpec(memory_space=pl.ANY)],
        out_specs=pl.BlockSpec((nc, MOE_BLOCK, LANES), lambda j, be, na, nxt: (0, j, 0)),
        scratch_shapes=[pltpu.VMEM((d, de), BF16), pltpu.VMEM((d, de), BF16), pltpu.VMEM((de, d), BF16),
                        pltpu.VMEM((d, de), F32), pltpu.VMEM((d, de), F32), pltpu.VMEM((de, d), F32),
                        pltpu.SemaphoreType.DMA((3,))],
    )
    return pl.pallas_call(
        _expert_body,
        grid_spec=grid_spec,
        out_shape=jax.ShapeDtypeStruct((nc, n_slots, LANES), F32),
        compiler_params=pltpu.CompilerParams(dimension_semantics=("arbitrary",),
                                             vmem_limit_bytes=VMEM_LIMIT),
        name="experts",
    )(block_exp, n_active, next_exp, xs, w_gate, w_up, w_down)


def _combine_body(dcur_ref, dnxt_ref, gate_ref, h1_ref, p_ref, yb_ref, gple_ref, wpg_hbm, wpp_hbm, gfin_ref,
                  yp_ref, ys_ref, buf_ref, sem, wpg_ref, wpp_ref, stage_ref, wsem,
                  *, n_prompt_tiles, final_norm):
    i = pl.program_id(0)

    @pl.when(i == 0)
    def _():
        _load_as_bf16(wpg_hbm, wpg_ref, stage_ref, wsem)
        _load_as_bf16(wpp_hbm, wpp_ref, stage_ref, wsem)

    n_tiles = pl.num_programs(0)
    tc = h1_ref.shape[0]
    slot = lax.rem(i, 2)

    @pl.when(i == 0)
    def _():
        def body(t, carry):
            for k in range(TOP_K):
                _row_copy(yb_ref, dcur_ref[k, t], buf_ref.at[0, k], t, sem.at[0]).start()
            return carry

        lax.fori_loop(0, tc, body, 0)

    def wait_slot(s):
        for k in range(TOP_K):
            pltpu.make_async_copy(yb_ref.at[:, pl.ds(0, tc), :], buf_ref.at[s, k], sem.at[s]).wait()

    wait_slot(slot)
    gt = gate_ref[...].T
    h2 = h1_ref[...] + (gt[:, 0:1] * _load_chunks(buf_ref.at[slot, 0])
                        + gt[:, 1:2] * _load_chunks(buf_ref.at[slot, 1]))
    for t in range(tc):
        for k in range(TOP_K):
            _row_copy(yb_ref, dnxt_ref[k, t], buf_ref.at[1 - slot, k], t, sem.at[1 - slot]).start()
    a = _rms(h2, gple_ref[...]).astype(BF16)
    h3 = h2 + _sigmoid(_dot(a, wpg_ref[...])) * _dot(p_ref[...].astype(BF16), wpp_ref[...])
    if final_norm:
        h3 = _rms(h3, gfin_ref[...])

    @pl.when(i < n_prompt_tiles)
    def _():
        yp_ref[...] = h3

    @pl.when(i >= n_prompt_tiles)
    def _():
        ys_ref[...] = h3

    @pl.when(i == n_tiles - 1)
    def _():
        wait_slot(1 - slot)


def _combine(dest, gates, h1, p, yb, g_ple, w_pg, w_pp, g_final, *, n_prompt, final_norm):
    n, d = h1.shape
    ple = p.shape[1]
    tc = 256
    assert n_prompt % tc == 0 and (n - n_prompt) % tc == 0
    nt, npt = n // tc, n_prompt // tc
    nst = nt - npt
    const2 = lambda i: (0, 0)
    return pl.pallas_call(
        functools.partial(_combine_body, n_prompt_tiles=npt, final_norm=final_norm),
        grid=(nt,),
        in_specs=[pl.BlockSpec((TOP_K, tc), lambda i: (0, i), memory_space=pltpu.SMEM),
                  pl.BlockSpec((TOP_K, tc), lambda i: (0, jnp.minimum(i + 1, nt - 1)), memory_space=pltpu.SMEM),
                  pl.BlockSpec((SUBLANES, tc), lambda i: (0, i)),
                  pl.BlockSpec((tc, d), lambda i: (i, 0)),
                  pl.BlockSpec((tc, ple), lambda i: (i, 0)),
                  pl.BlockSpec(memory_space=pl.ANY),
                  pl.BlockSpec((1, d), const2),
                  pl.BlockSpec(memory_space=pl.ANY),
                  pl.BlockSpec(memory_space=pl.ANY),
                  pl.BlockSpec((1, d), const2)],
        out_specs=[pl.BlockSpec((tc, d), lambda i: (jnp.minimum(i, npt - 1), 0)),
                   pl.BlockSpec((tc, d), lambda i: (jnp.clip(i - npt, 0, nst - 1), 0))],
        out_shape=[jax.ShapeDtypeStruct((n_prompt, d), F32),
                   jax.ShapeDtypeStruct((n - n_prompt, d), F32)],
        scratch_shapes=[pltpu.VMEM((2, TOP_K, d // LANES, tc, LANES), F32), pltpu.SemaphoreType.DMA((2,)),
                        pltpu.VMEM((d, d), BF16), pltpu.VMEM((ple, d), BF16),
                        pltpu.VMEM((2, WEIGHT_STAGE_ROWS, d), F32), pltpu.SemaphoreType.DMA((2,))],
        compiler_params=pltpu.CompilerParams(dimension_semantics=("arbitrary",),
                                             vmem_limit_bytes=VMEM_LIMIT),
        name="combine",
    )(dest, dest, gates, h1, p, yb, g_ple, w_pg, w_pp, g_final)


def _layer(hp, hs, p, s0_sample, lb, batch, seq, dec_batch, dec_seq,
           g_mix, w_in, g_head, w_pa, ln_v_g, ln_v_b, w_s, b_s, w_pb, w_o,
           g_ffn, w_gr, b_gr, w_er, b_er, w_gate, w_up, w_down, g_ple, w_pg, w_pp, g_final, final_norm):
    n_prompt, d = hp.shape
    n = n_prompt + hs.shape[0]
    n_heads = d // HEAD_DIM
    row = lambda a: a.reshape(1, -1).astype(F32)

    z = _in_proj(hp, hs, row(g_mix), w_in)

    lb_row, gh_row = row(lb), row(g_head)
    s0_prompt = jnp.zeros((batch, n_heads, HEAD_DIM, HEAD_DIM), F32)
    og_p, st_p = _hgrn(z, s0_prompt, lb_row, gh_row, row_base=0, batch=batch, seq=seq,
                       d_model=d, name="hgrn_prompt")
    og_s, st_s = _hgrn(z, s0_sample.astype(F32), lb_row, gh_row, row_base=n_prompt, batch=dec_batch,
                       seq=dec_seq, d_model=d, name="hgrn_sample")

    start = PAST_LEN % MLP_CHUNK
    assert start + dec_seq <= MLP_CHUNK
    rep = MLP_CHUNK // dec_seq
    ws_s = jnp.tile(w_s[:, start:start + dec_seq, start:start + dec_seq], (1, rep, rep))
    bs_s = jnp.tile(b_s[:, start:start + dec_seq], (1, rep))
    h1, vn_p, vn_s = _mix(z, og_p, og_s, hp, hs, row(ln_v_g), row(ln_v_b), w_s, ws_s, b_s[..., None],
                          bs_s[..., None], w_pa, w_pb, w_o,
                          batch=batch, dec_seq=dec_seq)

    pad_rows = ROUTER_ROWS - N_EXPERTS - N_GROUPS
    wr = jnp.concatenate([w_er.T, w_gr.T, jnp.zeros((pad_rows, d), F32)], axis=0)
    br = jnp.concatenate([b_er, b_gr, jnp.zeros((pad_rows,), F32)]).reshape(ROUTER_ROWS, 1).astype(F32)
    c, idx, gates, cnt = _router(h1, row(g_ffn), wr, br)

    counts = cnt[:, 0]
    padded = (counts + MOE_BLOCK - 1) // MOE_BLOCK * MOE_BLOCK
    pad_end = jnp.cumsum(padded).astype(jnp.int32)
    pad_start = pad_end - padded
    n_blocks = -(-(n * TOP_K) // MOE_BLOCK) + N_EXPERTS
    block_first = jnp.arange(n_blocks, dtype=jnp.int32) * MOE_BLOCK
    block_exp = jnp.minimum(jnp.sum(pad_end[None, :] <= block_first[:, None], axis=1), N_EXPERTS - 1).astype(jnp.int32)
    n_active = jnp.maximum(pad_end[-1:] // MOE_BLOCK, 1)
    hit = idx[0:TOP_K, :, None] == jnp.arange(N_EXPERTS, dtype=jnp.int32)
    dest = jnp.sum(jnp.where(hit, pad_start, 0), axis=-1) + idx[TOP_K:2 * TOP_K]

    xs = _dispatch(pad_start + counts, pad_end, dest, c, n_blocks * MOE_BLOCK)
    used = jnp.where(counts > 0, jnp.arange(N_EXPERTS, dtype=jnp.int32), N_EXPERTS)
    next_exp = jnp.concatenate([lax.cummin(used, axis=0, reverse=True)[1:],
                                jnp.full((1,), N_EXPERTS, jnp.int32)]).astype(jnp.int32)
    yb = _experts(block_exp, n_active, next_exp, xs, w_gate, w_up, w_down)
    yp, ys = _combine(dest, gates, h1, p, yb, row(g_ple), w_pg, w_pp,
                      row(g_final), n_prompt=n_prompt, final_norm=final_norm)
    return yp, ys, st_p, st_s, vn_p, vn_s


def kernel(x_prompt, x_sample, p_prompt, p_sample, state_hgrn, g_mix, w_in, lb_logits, g_head, w_pa, ln_v_g, ln_v_b, w_s, b_s, w_pb, w_o, g_ffn, w_gr, b_gr, w_er, b_er, w_gate, w_up, w_down, g_ple, w_pg, w_pp, g_final):
    batch, seq, d = x_prompt.shape
    dec_batch, dec_seq, _ = x_sample.shape
    depth = g_mix.shape[0]
    n_prompt, n_sample = batch * seq, dec_batch * dec_seq
    width = w_pb.shape[1]
    lbs = jnp.cumsum(jax.nn.softmax(lb_logits.astype(F32), axis=0), axis=0)
    hp, hs = x_prompt.reshape(n_prompt, d), x_sample.reshape(n_sample, d)
    keep = min((seq - 1) % MLP_CHUNK + 1, seq)
    keep_s = min((PAST_LEN % MLP_CHUNK + dec_seq - 1) % MLP_CHUNK + 1, dec_seq)
    sp, ss, vp, vs = [], [], [], []
    for i in range(depth):
        p = jnp.concatenate([p_prompt[i].reshape(n_prompt, -1), p_sample[i].reshape(n_sample, -1)], axis=0)
        hp, hs, st_p, st_s, vn_p, vn_s = _layer(
            hp, hs, p, state_hgrn[i], lbs[i], batch, seq, dec_batch, dec_seq,
            g_mix[i], w_in[i], g_head[i], w_pa[i], ln_v_g[i], ln_v_b[i], w_s[i], b_s[i], w_pb[i], w_o[i],
            g_ffn[i], w_gr[i], b_gr[i], w_er[i], b_er[i], w_gate[i], w_up[i], w_down[i],
            g_ple[i], w_pg[i], w_pp[i], g_final, i == depth - 1)
        sp.append(st_p.astype(x_prompt.dtype))
        ss.append(st_s.astype(state_hgrn.dtype))
        vn_last = vn_p.reshape(batch, -1, width)
        vp.append(vn_last[:, vn_last.shape[1] - keep:])
        vs.append(vn_s.reshape(dec_batch, dec_seq, width)[:, dec_seq - keep_s:])
    y_prompt = hp.reshape(batch, seq, d)
    y_sample = hs.reshape(dec_batch, dec_seq, d)
    return (y_prompt, y_sample, jnp.stack(sp), jnp.stack(ss), jnp.stack(vp), jnp.stack(vs))
```

```python
import functools
import math

import jax
import jax.numpy as jnp
from jax import lax
from jax.experimental import pallas as pl
from jax.experimental.pallas import tpu as pltpu

F32 = jnp.float32
BF16 = jnp.bfloat16

EPS = 1e-6
HEAD_DIM = 128
REC_CHUNK = 64
REC_HALF = 32
MLP_CHUNK = 128
MLP_GROUPS = 4
N_GROUPS = 4
E_PER_GROUP = 8
N_EXPERTS = N_GROUPS * E_PER_GROUP
TOP_K = 2
MOE_BLOCK = 256
PAST_LEN = 2048
LANES = 128
SUBLANES = 8
ROUTER_ROWS = 48
VMEM_LIMIT = 56 * 1024 * 1024
LOG2_E = 1.4426950408889634
WEIGHT_STAGE_ROWS = 256


def _pick(n, candidates):
    for c in candidates:
        if n % c == 0:
            return c
    raise ValueError(f"no tile in {candidates} divides {n}")


def _dot(a, b):
    return jnp.dot(a, b, preferred_element_type=F32)


def _dot_nt(a, b, precision=None):
    return lax.dot_general(a, b, (((1,), (1,)), ((), ())), precision=precision, preferred_element_type=F32)


def _dot_tn(a, b):
    return lax.dot_general(a, b, (((0,), (0,)), ((), ())), preferred_element_type=F32)


def _rms(x, g):
    return x * lax.rsqrt(jnp.mean(x * x, axis=-1, keepdims=True) + EPS) * g


def _gelu(x):
    c = math.sqrt(2.0 / math.pi)
    return x * (0.5 * (1.0 + jnp.tanh(c * (x + 0.044715 * (x * x * x)))))


def _sigmoid(x):
    return 0.5 * jnp.tanh(0.5 * x) + 0.5


def _store_chunks(ref, x):
    for c in range(ref.shape[0]):
        ref[c] = x[:, c * LANES:(c + 1) * LANES].astype(ref.dtype)


def _load_chunks(ref):
    return jnp.concatenate([ref[c] for c in range(ref.shape[0])], axis=1)


def _row_copy(src_ref, src_row, dst_ref, dst_row, sem):
    return pltpu.make_async_copy(src_ref.at[:, pl.ds(src_row, 1), :], dst_ref.at[:, pl.ds(dst_row, 1), :], sem)


def _rows_copy(src_ref, dst_ref, dst_row, n_rows, sem):
    return pltpu.make_async_copy(src_ref, dst_ref.at[:, pl.ds(dst_row, n_rows), :], sem)


def _load_as_bf16(w_hbm, w_vmem, stage_ref, sem):
    rows = min(stage_ref.shape[1], w_hbm.shape[0])
    assert w_hbm.shape[0] % rows == 0
    n_chunks = w_hbm.shape[0] // rows

    def copy(c):
        return pltpu.make_async_copy(w_hbm.at[pl.ds(c * rows, rows)],
                                     stage_ref.at[c % 2, pl.ds(0, rows)], sem.at[c % 2])

    copy(0).start()
    for c in range(n_chunks):
        if c + 1 < n_chunks:
            copy(c + 1).start()
        copy(c).wait()
        w_vmem[pl.ds(c * rows, rows), :] = stage_ref[c % 2, pl.ds(0, rows)].astype(BF16)


def _in_proj_body(xp_ref, xs_ref, g_ref, w_ref, z_ref, xn_ref, *, n_prompt_tiles):
    first = pl.program_id(1) == 0
    is_p = pl.program_id(0) < n_prompt_tiles

    @pl.when(first & is_p)
    def _():
        xn_ref[...] = _rms(xp_ref[...], g_ref[...]).astype(BF16)

    @pl.when(first & jnp.logical_not(is_p))
    def _():
        xn_ref[...] = _rms(xs_ref[...], g_ref[...]).astype(BF16)

    z_ref[...] = _dot(xn_ref[...], w_ref[...].astype(BF16)).astype(z_ref.dtype)


def _in_proj(xp, xs, g, w):
    (n_p, d), n_s = xp.shape, xs.shape[0]
    width = w.shape[1]
    tm = _pick(math.gcd(n_p, n_s), (1024, 512, 256))
    tn = _pick(width, (1024, 512, 256, 128))
    npt, nst = n_p // tm, n_s // tm
    return pl.pallas_call(
        functools.partial(_in_proj_body, n_prompt_tiles=npt),
        grid=(npt + nst, width // tn),
        in_specs=[pl.BlockSpec((tm, d), lambda i, j: (jnp.minimum(i, npt - 1), 0)),
                  pl.BlockSpec((tm, d), lambda i, j: (jnp.clip(i - npt, 0, nst - 1), 0),
                               pipeline_mode=pl.Buffered(1)),
                  pl.BlockSpec((1, d), lambda i, j: (0, 0)),
                  pl.BlockSpec((d, tn), lambda i, j: (0, j))],
        out_specs=pl.BlockSpec((tm, tn), lambda i, j: (i, j)),
        out_shape=jax.ShapeDtypeStruct((n_p + n_s, width), BF16),
        scratch_shapes=[pltpu.VMEM((tm, d), BF16)],
        compiler_params=pltpu.CompilerParams(dimension_semantics=("parallel", "arbitrary"),
                                             vmem_limit_bytes=VMEM_LIMIT),
        name="in_proj",
    )(xp, xs, g, w)


def _cumsum_rows(x, row):
    d = 1
    while d < x.shape[0]:
        x = x + jnp.where(row >= d, pltpu.roll(x, d, 0), 0.0)
        d *= 2
    return x


def _hgrn_body(q_ref, f_ref, i_ref, zo_ref, s0_ref, lb_ref, gh_ref, og_ref, sout_ref,
               st_ref, qs_ref, qe_ref, kd_ref, dec_ref, a_ref, o_ref, *key_refs, chunk, n_chunks, heads):
    ti = pl.program_id(2)

    @pl.when(ti == 0)
    def _():
        for h in range(heads):
            st_ref[h] = s0_ref[0, h].T

    half = min(REC_HALF, chunk)
    n_half = chunk // half
    row = lax.broadcasted_iota(jnp.int32, (chunk, HEAD_DIM), 0)
    masks = []
    for g in range(n_half):
        r = lax.broadcasted_iota(jnp.int32, (half, half * (g + 1)), 0)
        c = lax.broadcasted_iota(jnp.int32, (half, half * (g + 1)), 1)
        masks.append(r + g * half >= c)

    def chunk_step(ci, carry):
        r0 = pl.multiple_of(ci * chunk, chunk)
        rows = pl.ds(r0, chunk)
        for h in range(heads):
            hs = slice(h * HEAD_DIM, (h + 1) * HEAD_DIM)
            q = q_ref[rows, hs].astype(F32)
            lb = lb_ref[:, hs]
            c1 = 0.5 * (1.0 - lb)
            f = (lb + c1) + c1 * jnp.tanh(0.5 * f_ref[rows, hs].astype(F32))
            kk = 1.0 - f
            b = _cumsum_rows(jnp.log(f), row) * LOG2_E
            b_last = b[chunk - 1:chunk, :]
            dec_ref[0:1, hs] = jnp.exp2(b_last)
            mids, ks = [], []
            for g in range(n_half):
                rg = slice(g * half, (g + 1) * half)
                mid = b[g * half + half // 2 - 1:g * half + half // 2, :]
                qs_g = q[rg] * jnp.exp2(b[rg] - mid)
                ks_g = kk[rg] * jnp.exp2(mid - b[rg])
                mids.append(mid)
                ks.append(ks_g)
                qs_ref[rg, hs] = qs_g.astype(BF16)
                qe_ref[rg, hs] = (qs_g * jnp.exp2(mid)).astype(BF16)
                kd_ref[rg, hs] = (ks_g * jnp.exp2(b_last - mid)).astype(BF16)
                for gp in range(g):
                    key_refs[g][gp * half:(gp + 1) * half, hs] = (
                        ks[gp] * jnp.exp2(mid - mids[gp])).astype(BF16)
                key_refs[g][rg, hs] = ks_g.astype(BF16)
        for h in range(heads):
            hs = slice(h * HEAD_DIM, (h + 1) * HEAD_DIM)
            for g in range(n_half):
                rg = slice(g * half, (g + 1) * half)
                a = _dot_nt(qs_ref[rg, hs], key_refs[g][:, hs])
                a_ref[h, rg, 0:half * (g + 1)] = jnp.where(masks[g], a, 0.0).astype(BF16)
        for h in range(heads):
            hs = slice(h * HEAD_DIM, (h + 1) * HEAD_DIM)
            v = i_ref[rows, hs]
            st = st_ref[h]
            o_inter = _dot_nt(qe_ref[:, hs], st.astype(BF16))
            for g in range(n_half):
                rg = slice(g * half, (g + 1) * half)
                o_ref[rg, hs] = o_inter[rg] + _dot(a_ref[h, rg, 0:half * (g + 1)], v[0:half * (g + 1)])
            st_ref[h] = st * dec_ref[0:1, hs] + _dot_tn(v, kd_ref[:, hs])
        for h in range(heads):
            hs = slice(h * HEAD_DIM, (h + 1) * HEAD_DIM)
            o = o_ref[:, hs]
            o_n = o * lax.rsqrt(jnp.mean(o * o, axis=-1, keepdims=True) + EPS) * (0.5 * gh_ref[:, hs])
            gate2 = jnp.tanh(0.5 * zo_ref[rows, hs].astype(F32)) + 1.0
            og_ref[rows, hs] = (o_n * gate2).astype(og_ref.dtype)
        return carry

    lax.fori_loop(0, n_chunks, chunk_step, 0)

    @pl.when(ti == pl.num_programs(2) - 1)
    def _():
        for h in range(heads):
            sout_ref[0, h] = st_ref[h].T


def _hgrn(z, s0, lb, g_head, *, row_base, batch, seq, d_model, name):
    n_heads = d_model // HEAD_DIM
    heads = min(16, n_heads)
    chunk = min(REC_CHUNK, seq)
    half = min(REC_HALF, chunk)
    assert seq % chunk == 0 and chunk % half == 0
    rt = max(_pick(seq, (512, 256, 128, 64, 32)), chunk)
    assert row_base % rt == 0
    tiles = seq // rt
    wblk = heads * HEAD_DIM
    cpb = d_model // wblk

    def zspec(section):
        return pl.BlockSpec((rt, wblk),
                            lambda b, hg, i: (row_base // rt + b * tiles + i, section * cpb + hg))

    return pl.pallas_call(
        functools.partial(_hgrn_body, chunk=chunk, n_chunks=rt // chunk, heads=heads),
        grid=(batch, n_heads // heads, tiles),
        in_specs=[zspec(0), zspec(1), zspec(2), zspec(3),
                  pl.BlockSpec((1, heads, HEAD_DIM, HEAD_DIM), lambda b, hg, i: (b, hg, 0, 0)),
                  pl.BlockSpec((1, wblk), lambda b, hg, i: (0, hg)),
                  pl.BlockSpec((1, wblk), lambda b, hg, i: (0, hg))],
        out_specs=[pl.BlockSpec((rt, wblk), lambda b, hg, i: (b * tiles + i, hg)),
                   pl.BlockSpec((1, heads, HEAD_DIM, HEAD_DIM), lambda b, hg, i: (b, hg, 0, 0))],
        out_shape=[jax.ShapeDtypeStruct((batch * seq, d_model), BF16),
                   jax.ShapeDtypeStruct((batch, n_heads, HEAD_DIM, HEAD_DIM), F32)],
        scratch_shapes=[pltpu.VMEM((heads, HEAD_DIM, HEAD_DIM), F32),
                        pltpu.VMEM((chunk, wblk), BF16),
                        pltpu.VMEM((chunk, wblk), BF16),
                        pltpu.VMEM((chunk, wblk), BF16),
                        pltpu.VMEM((SUBLANES, wblk), F32),
                        pltpu.VMEM((heads, chunk, chunk), BF16),
                        pltpu.VMEM((chunk, wblk), F32)]
                       + [pltpu.VMEM((half * (g + 1), wblk), BF16) for g in range(chunk // half)],
        compiler_params=pltpu.CompilerParams(dimension_semantics=("parallel", "parallel", "arbitrary"),
                                             vmem_limit_bytes=VMEM_LIMIT),
        name=name,
    )(z, z, z, z, s0, lb, g_head)


def _mix_body(zu_ref, zv_ref, zga_ref, zgb_ref, ogp_ref, ogs_ref, xp_ref, xs_ref, lng_ref, lnb_ref,
              wsp_ref, wss_ref, bsp_ref, bss_ref, wpa_hbm, wpb_hbm, wo_hbm,
              h1_ref, vnp_ref, vns_ref, sg_ref, wpa_ref, wpb_ref, wo_ref, stage_ref, wsem,
              *, n_prompt_tiles, dec_seq):
    @pl.when(pl.program_id(0) == 0)
    def _():
        _load_as_bf16(wpa_hbm, wpa_ref, stage_ref, wsem)
        _load_as_bf16(wpb_hbm, wpb_ref, stage_ref, wsem)
        _load_as_bf16(wo_hbm, wo_ref, stage_ref, wsem)

    is_p = pl.program_id(0) < n_prompt_tiles
    tm, width = zu_ref.shape
    gd = width // MLP_GROUPS
    u = _gelu(zu_ref[...].astype(F32))
    gv = _gelu(zv_ref[...].astype(F32))
    xc = gv - jnp.mean(gv, axis=-1, keepdims=True)
    vn = xc * lax.rsqrt(jnp.mean(xc * xc, axis=-1, keepdims=True) + EPS) * lng_ref[...] + lnb_ref[...]

    @pl.when(is_p)
    def _():
        vnp_ref[...] = vn

    @pl.when(jnp.logical_not(is_p))
    def _():
        vns_ref[...] = vn

    vnb = vn.astype(BF16)
    r = lax.broadcasted_iota(jnp.int32, (MLP_CHUNK, MLP_CHUNK), 0)
    c = lax.broadcasted_iota(jnp.int32, (MLP_CHUNK, MLP_CHUNK), 1)
    causal = r >= c
    same_stream = (r // dec_seq) == (c // dec_seq)
    for g in range(MLP_GROUPS):
        w_p = jnp.where(causal, wsp_ref[g], 0.0)
        w_s = jnp.where(causal & same_stream, wss_ref[g], 0.0)
        w = jnp.where(is_p, w_p, w_s).astype(BF16)
        bias = jnp.where(is_p, bsp_ref[g], bss_ref[g])
        for cc in range(tm // MLP_CHUNK):
            rows = slice(cc * MLP_CHUNK, (cc + 1) * MLP_CHUNK)
            cols = slice(g * gd, (g + 1) * gd)
            s = _dot(w, vnb[rows, cols]) + bias
            sg_ref[rows, cols] = (u[rows, cols] * s).astype(BF16)
    y_b = _dot(sg_ref[...], wpb_ref[...])
    og = jnp.where(is_p, ogp_ref[...].astype(F32), ogs_ref[...].astype(F32)).astype(BF16)
    y_a = _dot(og, wpa_ref[...])
    m = _sigmoid(zga_ref[...].astype(F32)) * y_a + _sigmoid(zgb_ref[...].astype(F32)) * y_b
    x = jnp.where(is_p, xp_ref[...], xs_ref[...])
    h1_ref[...] = x + _dot(m.astype(BF16), wo_ref[...])


def _mix(z, og_p, og_s, xp, xs, ln_g, ln_b, ws_p, ws_s, bs_p, bs_s, w_pa, w_pb, w_o, *, batch, dec_seq):
    (n_p, d), n_s = xp.shape, xs.shape[0]
    n = n_p + n_s
    width = w_pb.shape[0]
    tm = 256
    assert n_s % tm == 0 and (n_p // batch) % tm == 0 and tm % MLP_CHUNK == 0 and MLP_CHUNK % dec_seq == 0
    npt, nst = n_p // tm, n_s // tm
    tpb = npt // batch
    u_blk = 4 * d // width
    ga_blk = (4 * d + 2 * width) // d
    const2 = lambda i: (0, 0)
    const3 = lambda i: (0, 0, 0)
    p_map = lambda i: (jnp.minimum(i, npt - 1), 0)
    s_map = lambda i: (jnp.clip(i - npt, 0, nst - 1), 0)
    return pl.pallas_call(
        functools.partial(_mix_body, n_prompt_tiles=npt, dec_seq=dec_seq),
        grid=(n // tm,),
        in_specs=[pl.BlockSpec((tm, width), lambda i: (i, u_blk)),
                  pl.BlockSpec((tm, width), lambda i: (i, u_blk + 1)),
                  pl.BlockSpec((tm, d), lambda i: (i, ga_blk)),
                  pl.BlockSpec((tm, d), lambda i: (i, ga_blk + 1)),
                  pl.BlockSpec((tm, d), p_map),
                  pl.BlockSpec((tm, d), s_map),
                  pl.BlockSpec((tm, d), p_map),
                  pl.BlockSpec((tm, d), s_map),
                  pl.BlockSpec((1, width), const2),
                  pl.BlockSpec((1, width), const2),
                  pl.BlockSpec((MLP_GROUPS, MLP_CHUNK, MLP_CHUNK), const3),
                  pl.BlockSpec((MLP_GROUPS, MLP_CHUNK, MLP_CHUNK), const3),
                  pl.BlockSpec((MLP_GROUPS, MLP_CHUNK, 1), const3),
                  pl.BlockSpec((MLP_GROUPS, MLP_CHUNK, 1), const3),
                  pl.BlockSpec(memory_space=pl.ANY),
                  pl.BlockSpec(memory_space=pl.ANY),
                  pl.BlockSpec(memory_space=pl.ANY)],
        out_specs=[pl.BlockSpec((tm, d), lambda i: (i, 0)),
                   pl.BlockSpec((tm, width), lambda i: (jnp.minimum(i // tpb, batch - 1), 0)),
                   pl.BlockSpec((tm, width), s_map)],
        out_shape=[jax.ShapeDtypeStruct((n, d), F32),
                   jax.ShapeDtypeStruct((batch * tm, width), F32),
                   jax.ShapeDtypeStruct((n_s, width), F32)],
        scratch_shapes=[pltpu.VMEM((tm, width), BF16),
                        pltpu.VMEM((d, d), BF16), pltpu.VMEM((width, d), BF16), pltpu.VMEM((d, d), BF16),
                        pltpu.VMEM((2, WEIGHT_STAGE_ROWS, d), F32), pltpu.SemaphoreType.DMA((2,))],
        compiler_params=pltpu.CompilerParams(dimension_semantics=("arbitrary",),
                                             vmem_limit_bytes=VMEM_LIMIT),
        name="mix",
    )(z, z, z, z, og_p, og_s, xp, xs, ln_g, ln_b, ws_p, ws_s, bs_p, bs_s, w_pa, w_pb, w_o)


def _router_body(h_ref, g_ref, wr_ref, br_ref, c_ref, idx_ref, gate_ref, cnt_ref, carry_ref):
    @pl.when(pl.program_id(0) == 0)
    def _():
        carry_ref[...] = jnp.zeros_like(carry_ref)

    tm = h_ref.shape[0]
    c = _rms(h_ref[...], g_ref[...])
    _store_chunks(c_ref, c)
    wr = wr_ref[...]
    wr_hi = wr.astype(BF16)
    wr_lo = (wr - wr_hi.astype(F32)).astype(BF16)
    c_hi = c.astype(BF16)
    c_lo = (c - c_hi.astype(F32)).astype(BF16)
    both = _dot_nt(jnp.concatenate([wr_hi, wr_lo], axis=0), c_hi)
    lt = both[0:ROUTER_ROWS] + both[ROUTER_ROWS:2 * ROUTER_ROWS] + _dot_nt(wr_hi, c_lo) + br_ref[...]
    le = lt[0:N_EXPERTS]
    lg = lt[N_EXPERTS:N_EXPERTS + N_GROUPS]
    gmax = jnp.max(lg, axis=0, keepdims=True)
    p_sel = 1.0 / jnp.sum(jnp.exp(lg - gmax), axis=0, keepdims=True)
    best = lg[0:1]
    gi = jnp.zeros((1, tm), jnp.int32)
    for g in range(1, N_GROUPS):
        better = lg[g:g + 1] > best
        gi = jnp.where(better, g, gi)
        best = jnp.where(better, lg[g:g + 1], best)
    leg = jnp.zeros((E_PER_GROUP, tm), F32)
    for g in range(N_GROUPS):
        leg = jnp.where(gi == g, le[g * E_PER_GROUP:(g + 1) * E_PER_GROUP], leg)
    sub = lax.broadcasted_iota(jnp.int32, (E_PER_GROUP, tm), 0).astype(F32)
    v1 = jnp.max(leg, axis=0, keepdims=True)
    i1 = jnp.min(jnp.where(leg == v1, sub, float(E_PER_GROUP)), axis=0, keepdims=True)
    rest = jnp.where(sub == i1, -jnp.inf, leg)
    v2 = jnp.max(rest, axis=0, keepdims=True)
    i2 = jnp.min(jnp.where(rest == v2, sub, float(E_PER_GROUP)), axis=0, keepdims=True)
    e2 = jnp.exp(v2 - v1)
    den = 1.0 + e2
    gate0 = p_sel * (1.0 / den)
    gate1 = p_sel * (e2 / den)
    ex0 = gi * E_PER_GROUP + i1.astype(jnp.int32)
    ex1 = gi * E_PER_GROUP + i2.astype(jnp.int32)
    eid = lax.broadcasted_iota(jnp.int32, (N_EXPERTS, tm), 0)
    oh0 = eid == ex0
    oh1 = eid == ex1
    oh = jnp.where(oh0 | oh1, 1.0, 0.0)
    upper = jnp.where(lax.broadcasted_iota(jnp.int32, (tm, tm), 0) < lax.broadcasted_iota(jnp.int32, (tm, tm), 1),
                      1.0, 0.0).astype(BF16)
    before = _dot(oh.astype(BF16), upper) + carry_ref[:, 0:1]
    rank0 = jnp.sum(jnp.where(oh0, before, 0.0), axis=0, keepdims=True)
    rank1 = jnp.sum(jnp.where(oh1, before, 0.0), axis=0, keepdims=True)
    carry = carry_ref[...] + jnp.sum(oh, axis=1, keepdims=True)
    carry_ref[...] = carry
    cnt_ref[...] = carry.astype(jnp.int32)
    idx_ref[...] = jnp.zeros_like(idx_ref)
    idx_ref[0:1, :] = ex0
    idx_ref[1:2, :] = ex1
    idx_ref[2:3, :] = rank0.astype(jnp.int32)
    idx_ref[3:4, :] = rank1.astype(jnp.int32)
    gate_ref[...] = jnp.zeros_like(gate_ref)
    gate_ref[0:1, :] = gate0
    gate_ref[1:2, :] = gate1


def _router(h1, g_ffn, wr, br):
    n, d = h1.shape
    tm = _pick(n, (512, 256))
    return pl.pallas_call(
        _router_body,
        grid=(n // tm,),
        in_specs=[pl.BlockSpec((tm, d), lambda i: (i, 0)),
                  pl.BlockSpec((1, d), lambda i: (0, 0)),
                  pl.BlockSpec((ROUTER_ROWS, d), lambda i: (0, 0)),
                  pl.BlockSpec((ROUTER_ROWS, 1), lambda i: (0, 0))],
        out_specs=[pl.BlockSpec((d // LANES, tm, LANES), lambda i: (0, i, 0)),
                   pl.BlockSpec((SUBLANES, tm), lambda i: (0, i)),
                   pl.BlockSpec((SUBLANES, tm), lambda i: (0, i)),
                   pl.BlockSpec((N_EXPERTS, LANES), lambda i: (0, 0))],
        out_shape=[jax.ShapeDtypeStruct((d // LANES, n, LANES), F32),
                   jax.ShapeDtypeStruct((SUBLANES, n), jnp.int32),
                   jax.ShapeDtypeStruct((SUBLANES, n), F32),
                   jax.ShapeDtypeStruct((N_EXPERTS, LANES), jnp.int32)],
        scratch_shapes=[pltpu.VMEM((N_EXPERTS, LANES), F32)],
        compiler_params=pltpu.CompilerParams(dimension_semantics=("arbitrary",),
                                             vmem_limit_bytes=VMEM_LIMIT),
        name="router",
    )(h1, g_ffn, wr, br)


def _dispatch_body(fill_lo_ref, fill_hi_ref, dest_ref, c_ref, xs_ref, zero_ref, sem, zsem):
    td = c_ref.shape[1]

    @pl.when(pl.program_id(0) == 0)
    def _():
        zero_ref[...] = jnp.zeros_like(zero_ref)

        def per_expert(e, carry):
            lo, hi = fill_lo_ref[e], fill_hi_ref[e]

            def start(r, c2):
                _row_copy(zero_ref, 0, xs_ref, r, zsem).start()
                return c2

            def wait(r, c2):
                _row_copy(zero_ref, 0, xs_ref, 0, zsem).wait()
                return c2

            lax.fori_loop(lo, hi, start, 0)
            lax.fori_loop(lo, hi, wait, 0)
            return carry

        lax.fori_loop(0, N_EXPERTS, per_expert, 0)

        def block_copy(j):
            return _rows_copy(zero_ref, xs_ref, j * MOE_BLOCK, MOE_BLOCK, zsem)

        first_unused = fill_hi_ref[N_EXPERTS - 1] // MOE_BLOCK
        n_blocks = xs_ref.shape[1] // MOE_BLOCK

        def start_block(j, carry):
            block_copy(j).start()
            return carry

        def wait_block(j, carry):
            block_copy(0).wait()
            return carry

        lax.fori_loop(first_unused, n_blocks, start_block, 0)
        lax.fori_loop(first_unused, n_blocks, wait_block, 0)

    for t in range(td):
        for k in range(TOP_K):
            _row_copy(c_ref, t, xs_ref, dest_ref[k, t], sem).start(priority=k % 2)
    for k in range(TOP_K):
        _rows_copy(c_ref, xs_ref, 0, td, sem).wait()


def _dispatch(fill_lo, fill_hi, dest, c, n_slots):
    nc, n, _ = c.shape
    td = _pick(n, (512, 256))
    grid_spec = pltpu.PrefetchScalarGridSpec(
        num_scalar_prefetch=2,
        grid=(n // td,),
        in_specs=[pl.BlockSpec((TOP_K, td), lambda i, lo, hi: (0, i), memory_space=pltpu.SMEM),
                  pl.BlockSpec((nc, td, LANES), lambda i, lo, hi: (0, i, 0))],
        out_specs=pl.BlockSpec(memory_space=pl.ANY),
        scratch_shapes=[pltpu.VMEM((nc, MOE_BLOCK, LANES), F32),
                        pltpu.SemaphoreType.DMA(()), pltpu.SemaphoreType.DMA(())],
    )
    return pl.pallas_call(
        _dispatch_body,
        grid_spec=grid_spec,
        out_shape=jax.ShapeDtypeStruct((nc, n_slots, LANES), F32),
        compiler_params=pltpu.CompilerParams(dimension_semantics=("arbitrary",),
                                             vmem_limit_bytes=VMEM_LIMIT),
        name="dispatch",
    )(fill_lo, fill_hi, dest, c)


def _expert_body(be_ref, na_ref, nxt_ref, xs_ref, wg_hbm, wu_hbm, wd_hbm, yb_ref,
                 wg_ref, wu_ref, wd_ref, sg_ref, su_ref, sd_ref, wsem):
    j = pl.program_id(0)
    active = j < na_ref[0]
    e = be_ref[j]
    first = (j == 0) | (be_ref[jnp.maximum(j - 1, 0)] != e)

    def weight_copies(expert):
        return (pltpu.make_async_copy(wg_hbm.at[expert], sg_ref, wsem.at[0]),
                pltpu.make_async_copy(wu_hbm.at[expert], su_ref, wsem.at[1]),
                pltpu.make_async_copy(wd_hbm.at[expert], sd_ref, wsem.at[2]))

    @pl.when(j == 0)
    def _():
        for cp in weight_copies(e):
            cp.start()

    @pl.when(active & first)
    def _():
        for cp, stage, dst in zip(weight_copies(e), (sg_ref, su_ref, sd_ref), (wg_ref, wu_ref, wd_ref)):
            cp.wait()

            def convert(r, carry, stage=stage, dst=dst):
                rows = pl.ds(pl.multiple_of(r * LANES, LANES), LANES)
                dst[rows, :] = stage[rows, :].astype(BF16)
                return carry

            lax.fori_loop(0, stage.shape[0] // LANES, convert, 0)

        @pl.when(nxt_ref[e] < N_EXPERTS)
        def _():
            for cp in weight_copies(nxt_ref[e]):
                cp.start()

    @pl.when(active)
    def _():
        x = _load_chunks(xs_ref).astype(BF16)
        a = _dot(x, wg_ref[...])
        b = _dot(x, wu_ref[...])
        hid = (a * _sigmoid(a)) * b
        _store_chunks(yb_ref, _dot(hid.astype(BF16), wd_ref[...]))

    @pl.when(jnp.logical_not(active))
    def _():
        yb_ref[...] = jnp.zeros_like(yb_ref)


def _experts(block_exp, n_active, next_exp, xs, w_gate, w_up, w_down):
    nc, n_slots, _ = xs.shape
    d, de = w_gate.shape[1:]
    nb = n_slots // MOE_BLOCK
    assert d % LANES == 0 and de % LANES == 0
    blk = lambda j, be, na, nxt: (0, jnp.minimum(j, na[0] - 1), 0)
    grid_spec = pltpu.PrefetchScalarGridSpec(
        num_scalar_prefetch=3,
        grid=(nb,),
        in_specs=[pl.BlockSpec((nc, MOE_BLOCK, LANES), blk),
                  pl.BlockSpec(memory_space=pl.ANY),
                  pl.BlockSpec(memory_space=pl.ANY),
                  pl.BlockSpec(memory_space=pl.ANY)],
        out_specs=pl.BlockSpec((nc, MOE_BLOCK, LANES), lambda j, be, na, nxt: (0, j, 0)),
        scratch_shapes=[pltpu.VMEM((d, de), BF16), pltpu.VMEM((d, de), BF16), pltpu.VMEM((de, d), BF16),
                        pltpu.VMEM((d, de), F32), pltpu.VMEM((d, de), F32), pltpu.VMEM((de, d), F32),
                        pltpu.SemaphoreType.DMA((3,))],
    )
    return pl.pallas_call(
        _expert_body,
        grid_spec=grid_spec,
        out_shape=jax.ShapeDtypeStruct((nc, n_slots, LANES), F32),
        compiler_params=pltpu.CompilerParams(dimension_semantics=("arbitrary",),
                                             vmem_limit_bytes=VMEM_LIMIT),
        name="experts",
    )(block_exp, n_active, next_exp, xs, w_gate, w_up, w_down)


def _combine_body(dcur_ref, dnxt_ref, gate_ref, h1_ref, p_ref, yb_ref, gple_ref, wpg_hbm, wpp_hbm, gfin_ref,
                  yp_ref, ys_ref, buf_ref, sem, wpg_ref, wpp_ref, stage_ref, wsem,
                  *, n_prompt_tiles, final_norm):
    i = pl.program_id(0)

    @pl.when(i == 0)
    def _():
        _load_as_bf16(wpg_hbm, wpg_ref, stage_ref, wsem)
        _load_as_bf16(wpp_hbm, wpp_ref, stage_ref, wsem)

    n_tiles = pl.num_programs(0)
    tc = h1_ref.shape[0]
    slot = lax.rem(i, 2)

    @pl.when(i == 0)
    def _():
        def body(t, carry):
            for k in range(TOP_K):
                _row_copy(yb_ref, dcur_ref[k, t], buf_ref.at[0, k], t, sem.at[0]).start()
            return carry

        lax.fori_loop(0, tc, body, 0)

    def wait_slot(s):
        for k in range(TOP_K):
            pltpu.make_async_copy(yb_ref.at[:, pl.ds(0, tc), :], buf_ref.at[s, k], sem.at[s]).wait()

    wait_slot(slot)
    gt = gate_ref[...].T
    h2 = h1_ref[...] + (gt[:, 0:1] * _load_chunks(buf_ref.at[slot, 0])
                        + gt[:, 1:2] * _load_chunks(buf_ref.at[slot, 1]))
    for t in range(tc):
        for k in range(TOP_K):
            _row_copy(yb_ref, dnxt_ref[k, t], buf_ref.at[1 - slot, k], t, sem.at[1 - slot]).start(priority=k % 2)
    a = _rms(h2, gple_ref[...]).astype(BF16)
    h3 = h2 + _sigmoid(_dot(a, wpg_ref[...])) * _dot(p_ref[...].astype(BF16), wpp_ref[...])
    if final_norm:
        h3 = _rms(h3, gfin_ref[...])

    @pl.when(i < n_prompt_tiles)
    def _():
        yp_ref[...] = h3

    @pl.when(i >= n_prompt_tiles)
    def _():
        ys_ref[...] = h3

    @pl.when(i == n_tiles - 1)
    def _():
        wait_slot(1 - slot)


def _combine(dest, gates, h1, p, yb, g_ple, w_pg, w_pp, g_final, *, n_prompt, final_norm):
    n, d = h1.shape
    ple = p.shape[1]
    tc = 256
    assert n_prompt % tc == 0 and (n - n_prompt) % tc == 0
    nt, npt = n // tc, n_prompt // tc
    nst = nt - npt
    const2 = lambda i: (0, 0)
    return pl.pallas_call(
        functools.partial(_combine_body, n_prompt_tiles=npt, final_norm=final_norm),
        grid=(nt,),
        in_specs=[pl.BlockSpec((TOP_K, tc), lambda i: (0, i), memory_space=pltpu.SMEM),
                  pl.BlockSpec((TOP_K, tc), lambda i: (0, jnp.minimum(i + 1, nt - 1)), memory_space=pltpu.SMEM),
                  pl.BlockSpec((SUBLANES, tc), lambda i: (0, i)),
                  pl.BlockSpec((tc, d), lambda i: (i, 0)),
                  pl.BlockSpec((tc, ple), lambda i: (i, 0)),
                  pl.BlockSpec(memory_space=pl.ANY),
                  pl.BlockSpec((1, d), const2),
                  pl.BlockSpec(memory_space=pl.ANY),
                  pl.BlockSpec(memory_space=pl.ANY),
                  pl.BlockSpec((1, d), const2)],
        out_specs=[pl.BlockSpec((tc, d), lambda i: (jnp.minimum(i, npt - 1), 0)),
                   pl.BlockSpec((tc, d), lambda i: (jnp.clip(i - npt, 0, nst - 1), 0))],
        out_shape=[jax.ShapeDtypeStruct((n_prompt, d), F32),
                   jax.ShapeDtypeStruct((n - n_prompt, d), F32)],
        scratch_shapes=[pltpu.VMEM((2, TOP_K, d // LANES, tc, LANES), F32), pltpu.SemaphoreType.DMA((2,)),
                        pltpu.VMEM((d, d), BF16), pltpu.VMEM((ple, d), BF16),
                        pltpu.VMEM((2, WEIGHT_STAGE_ROWS, d), F32), pltpu.SemaphoreType.DMA((2,))],
        compiler_params=pltpu.CompilerParams(dimension_semantics=("arbitrary",),
                                             vmem_limit_bytes=VMEM_LIMIT),
        name="combine",
    )(dest, dest, gates, h1, p, yb, g_ple, w_pg, w_pp, g_final)


def _layer(hp, hs, p, s0_sample, lb, batch, seq, dec_batch, dec_seq,
           g_mix, w_in, g_head, w_pa, ln_v_g, ln_v_b, w_s, b_s, w_pb, w_o,
           g_ffn, w_gr, b_gr, w_er, b_er, w_gate, w_up, w_down, g_ple, w_pg, w_pp, g_final, final_norm):
    n_prompt, d = hp.shape
    n = n_prompt + hs.shape[0]
    n_heads = d // HEAD_DIM
    row = lambda a: a.reshape(1, -1).astype(F32)

    z = _in_proj(hp, hs, row(g_mix), w_in)

    lb_row, gh_row = row(lb), row(g_head)
    s0_prompt = jnp.zeros((batch, n_heads, HEAD_DIM, HEAD_DIM), F32)
    og_p, st_p = _hgrn(z, s0_prompt, lb_row, gh_row, row_base=0, batch=batch, seq=seq,
                       d_model=d, name="hgrn_prompt")
    og_s, st_s = _hgrn(z, s0_sample.astype(F32), lb_row, gh_row, row_base=n_prompt, batch=dec_batch,
                       seq=dec_seq, d_model=d, name="hgrn_sample")

    start = PAST_LEN % MLP_CHUNK
    assert start + dec_seq <= MLP_CHUNK
    rep = MLP_CHUNK // dec_seq
    ws_s = jnp.tile(w_s[:, start:start + dec_seq, start:start + dec_seq], (1, rep, rep))
    bs_s = jnp.tile(b_s[:, start:start + dec_seq], (1, rep))
    h1, vn_p, vn_s = _mix(z, og_p, og_s, hp, hs, row(ln_v_g), row(ln_v_b), w_s, ws_s, b_s[..., None],
                          bs_s[..., None], w_pa, w_pb, w_o,
                          batch=batch, dec_seq=dec_seq)

    pad_rows = ROUTER_ROWS - N_EXPERTS - N_GROUPS
    wr = jnp.concatenate([w_er.T, w_gr.T, jnp.zeros((pad_rows, d), F32)], axis=0)
    br = jnp.concatenate([b_er, b_gr, jnp.zeros((pad_rows,), F32)]).reshape(ROUTER_ROWS, 1).astype(F32)
    c, idx, gates, cnt = _router(h1, row(g_ffn), wr, br)

    counts = cnt[:, 0]
    padded = (counts + MOE_BLOCK - 1) // MOE_BLOCK * MOE_BLOCK
    pad_end = jnp.cumsum(padded).astype(jnp.int32)
    pad_start = pad_end - padded
    n_blocks = -(-(n * TOP_K) // MOE_BLOCK) + N_EXPERTS
    block_first = jnp.arange(n_blocks, dtype=jnp.int32) * MOE_BLOCK
    block_exp = jnp.minimum(jnp.sum(pad_end[None, :] <= block_first[:, None], axis=1), N_EXPERTS - 1).astype(jnp.int32)
    n_active = jnp.maximum(pad_end[-1:] // MOE_BLOCK, 1)
    hit = idx[0:TOP_K, :, None] == jnp.arange(N_EXPERTS, dtype=jnp.int32)
    dest = jnp.sum(jnp.where(hit, pad_start, 0), axis=-1) + idx[TOP_K:2 * TOP_K]

    xs = _dispatch(pad_start + counts, pad_end, dest, c, n_blocks * MOE_BLOCK)
    used = jnp.where(counts > 0, jnp.arange(N_EXPERTS, dtype=jnp.int32), N_EXPERTS)
    next_exp = jnp.concatenate([lax.cummin(used, axis=0, reverse=True)[1:],
                                jnp.full((1,), N_EXPERTS, jnp.int32)]).astype(jnp.int32)
    yb = _experts(block_exp, n_active, next_exp, xs, w_gate, w_up, w_down)
    yp, ys = _combine(dest, gates, h1, p, yb, row(g_ple), w_pg, w_pp,
                      row(g_final), n_prompt=n_prompt, final_norm=final_norm)
    return yp, ys, st_p, st_s, vn_p, vn_s


def kernel(x_prompt, x_sample, p_prompt, p_sample, state_hgrn, g_mix, w_in, lb_logits, g_head, w_pa, ln_v_g, ln_v_b, w_s, b_s, w_pb, w_o, g_ffn, w_gr, b_gr, w_er, b_er, w_gate, w_up, w_down, g_ple, w_pg, w_pp, g_final):
    batch, seq, d = x_prompt.shape
    dec_batch, dec_seq, _ = x_sample.shape
    depth = g_mix.shape[0]
    n_prompt, n_sample = batch * seq, dec_batch * dec_seq
    width = w_pb.shape[1]
    lbs = jnp.cumsum(jax.nn.softmax(lb_logits.astype(F32), axis=0), axis=0)
    hp, hs = x_prompt.reshape(n_prompt, d), x_sample.reshape(n_sample, d)
    keep = min((seq - 1) % MLP_CHUNK + 1, seq)
    keep_s = min((PAST_LEN % MLP_CHUNK + dec_seq - 1) % MLP_CHUNK + 1, dec_seq)
    sp, ss, vp, vs = [], [], [], []
    for i in range(depth):
        p = jnp.concatenate([p_prompt[i].reshape(n_prompt, -1), p_sample[i].reshape(n_sample, -1)], axis=0)
        hp, hs, st_p, st_s, vn_p, vn_s = _layer(
            hp, hs, p, state_hgrn[i], lbs[i], batch, seq, dec_batch, dec_seq,
            g_mix[i], w_in[i], g_head[i], w_pa[i], ln_v_g[i], ln_v_b[i], w_s[i], b_s[i], w_pb[i], w_o[i],
            g_ffn[i], w_gr[i], b_gr[i], w_er[i], b_er[i], w_gate[i], w_up[i], w_down[i],
            g_ple[i], w_pg[i], w_pp[i], g_final, i == depth - 1)
        sp.append(st_p.astype(x_prompt.dtype))
        ss.append(st_s.astype(state_hgrn.dtype))
        vn_last = vn_p.reshape(batch, -1, width)
        vp.append(vn_last[:, vn_last.shape[1] - keep:])
        vs.append(vn_s.reshape(dec_batch, dec_seq, width)[:, dec_seq - keep_s:])
    y_prompt = hp.reshape(batch, seq, d)
    y_sample = hs.reshape(dec_batch, dec_seq, d)
    return (y_prompt, y_sample, jnp.stack(sp), jnp.stack(ss), jnp.stack(vp), jnp.stack(vs))
```

```python
import functools
import math

import jax
import jax.numpy as jnp
from jax import lax
from jax.experimental import pallas as pl
from jax.experimental.pallas import tpu as pltpu

F32 = jnp.float32
BF16 = jnp.bfloat16

EPS = 1e-6
HEAD_DIM = 128
REC_CHUNK = 64
REC_HALF = 32
MLP_CHUNK = 128
MLP_GROUPS = 4
N_GROUPS = 4
E_PER_GROUP = 8
N_EXPERTS = N_GROUPS * E_PER_GROUP
TOP_K = 2
MOE_BLOCK = 256
PAST_LEN = 2048
LANES = 128
SUBLANES = 8
ROUTER_ROWS = 48
VMEM_LIMIT = 56 * 1024 * 1024
LOG2_E = 1.4426950408889634
WEIGHT_STAGE_ROWS = 256


def _pick(n, candidates):
    for c in candidates:
        if n % c == 0:
            return c
    raise ValueError(f"no tile in {candidates} divides {n}")


def _dot(a, b):
    return jnp.dot(a, b, preferred_element_type=F32)


def _dot_nt(a, b, precision=None):
    return lax.dot_general(a, b, (((1,), (1,)), ((), ())), precision=precision, preferred_element_type=F32)


def _dot_tn(a, b):
    return lax.dot_general(a, b, (((0,), (0,)), ((), ())), preferred_element_type=F32)


def _rms(x, g):
    return x * lax.rsqrt(jnp.mean(x * x, axis=-1, keepdims=True) + EPS) * g


def _gelu(x):
    c = math.sqrt(2.0 / math.pi)
    return x * (0.5 * (1.0 + jnp.tanh(c * (x + 0.044715 * (x * x * x)))))


def _sigmoid(x):
    return 0.5 * jnp.tanh(0.5 * x) + 0.5


def _store_chunks(ref, x):
    for c in range(ref.shape[0]):
        ref[c] = x[:, c * LANES:(c + 1) * LANES].astype(ref.dtype)


def _load_chunks(ref):
    return jnp.concatenate([ref[c] for c in range(ref.shape[0])], axis=1)


HIGH_HALF = 0xFFFF0000


def _pack_bf16_pairs(x):
    half = x.shape[1] // 2
    lo = jnp.right_shift(pltpu.bitcast(x[:, :half], jnp.uint32), jnp.uint32(16))
    hi = jnp.bitwise_and(pltpu.bitcast(x[:, half:], jnp.uint32), jnp.uint32(HIGH_HALF))
    return jnp.bitwise_or(lo, hi)


def _unpack_bf16_pairs(p):
    lo = pltpu.bitcast(jnp.left_shift(p, jnp.uint32(16)), F32)
    hi = pltpu.bitcast(jnp.bitwise_and(p, jnp.uint32(HIGH_HALF)), F32)
    return jnp.concatenate([lo, hi], axis=1).astype(BF16)


def _row_copy(src_ref, src_row, dst_ref, dst_row, sem):
    return pltpu.make_async_copy(src_ref.at[:, pl.ds(src_row, 1), :], dst_ref.at[:, pl.ds(dst_row, 1), :], sem)


def _rows_copy(src_ref, dst_ref, dst_row, n_rows, sem):
    return pltpu.make_async_copy(src_ref, dst_ref.at[:, pl.ds(dst_row, n_rows), :], sem)


def _load_as_bf16(w_hbm, w_vmem, stage_ref, sem):
    half = stage_ref.shape[0] // 2
    rows, cols = min(half, w_hbm.shape[0]), min(stage_ref.shape[1], w_hbm.shape[1])
    assert w_hbm.shape[0] % rows == 0 and w_hbm.shape[1] % cols == 0
    chunks = [(r, c) for r in range(0, w_hbm.shape[0], rows) for c in range(0, w_hbm.shape[1], cols)]

    def slot(i):
        return stage_ref.at[pl.ds((i % 2) * half, rows), pl.ds(0, cols)]

    def copy(i):
        r, c = chunks[i]
        return pltpu.make_async_copy(w_hbm.at[pl.ds(r, rows), pl.ds(c, cols)], slot(i), sem.at[i % 2])

    copy(0).start()
    for i, (r, c) in enumerate(chunks):
        if i + 1 < len(chunks):
            copy(i + 1).start()
        copy(i).wait()
        w_vmem[pl.ds(r, rows), pl.ds(c, cols)] = slot(i)[...].astype(BF16)


def _in_proj_body(xp_ref, xs_ref, g_ref, w_ref, z_ref, xn_ref, *, n_prompt_tiles):
    first = pl.program_id(1) == 0
    is_p = pl.program_id(0) < n_prompt_tiles

    @pl.when(first & is_p)
    def _():
        xn_ref[...] = _rms(xp_ref[...], g_ref[...]).astype(BF16)

    @pl.when(first & jnp.logical_not(is_p))
    def _():
        xn_ref[...] = _rms(xs_ref[...], g_ref[...]).astype(BF16)

    z_ref[...] = _dot(xn_ref[...], w_ref[...].astype(BF16)).astype(z_ref.dtype)


def _in_proj(xp, xs, g, w):
    (n_p, d), n_s = xp.shape, xs.shape[0]
    width = w.shape[1]
    tm = _pick(math.gcd(n_p, n_s), (1024, 512, 256))
    tn = _pick(width, (1024, 512, 256, 128))
    npt, nst = n_p // tm, n_s // tm
    return pl.pallas_call(
        functools.partial(_in_proj_body, n_prompt_tiles=npt),
        grid=(npt + nst, width // tn),
        in_specs=[pl.BlockSpec((tm, d), lambda i, j: (jnp.minimum(i, npt - 1), 0)),
                  pl.BlockSpec((tm, d), lambda i, j: (jnp.clip(i - npt, 0, nst - 1), 0),
                               pipeline_mode=pl.Buffered(1)),
                  pl.BlockSpec((1, d), lambda i, j: (0, 0)),
                  pl.BlockSpec((d, tn), lambda i, j: (0, j))],
        out_specs=pl.BlockSpec((tm, tn), lambda i, j: (i, j)),
        out_shape=jax.ShapeDtypeStruct((n_p + n_s, width), BF16),
        scratch_shapes=[pltpu.VMEM((tm, d), BF16)],
        compiler_params=pltpu.CompilerParams(dimension_semantics=("parallel", "arbitrary"),
                                             vmem_limit_bytes=VMEM_LIMIT),
        name="in_proj",
    )(xp, xs, g, w)


def _cumsum_rows(x, tril3):
    hi = x.astype(BF16)
    r1 = x - hi.astype(F32)
    mid = r1.astype(BF16)
    lo = (r1 - mid.astype(F32)).astype(BF16)
    return _dot(tril3, jnp.concatenate([hi, mid, lo], axis=0))


def _hgrn_scratch(chunk, heads):
    half = min(REC_HALF, chunk)
    wblk = heads * HEAD_DIM
    return ([pltpu.VMEM((heads, HEAD_DIM, HEAD_DIM), F32),
             pltpu.VMEM((chunk, wblk), BF16),
             pltpu.VMEM((chunk, wblk), BF16),
             pltpu.VMEM((chunk, wblk), BF16),
             pltpu.VMEM((SUBLANES, wblk), F32),
             pltpu.VMEM((heads, chunk, chunk), BF16),
             pltpu.VMEM((chunk, wblk), F32)]
            + [pltpu.VMEM((half * (g + 1), wblk), BF16) for g in range(chunk // half)])


def _hgrn_chunk(*args, **kwargs):
    for run_pass in _hgrn_passes(*args, **kwargs):
        run_pass()


def _hgrn_passes(q_ref, f_ref, i_ref, zo_ref, lb_ref, gh_ref, og_ref, scratch, rows, og_rows, chunk, heads,
                 reset=None):
    st_ref, qs_ref, qe_ref, kd_ref, dec_ref, a_ref, o_ref = scratch[:7]
    key_refs = scratch[7:]
    half = min(REC_HALF, chunk)
    n_half = chunk // half
    tril = (lax.broadcasted_iota(jnp.int32, (chunk, chunk), 0)
            >= lax.broadcasted_iota(jnp.int32, (chunk, chunk), 1))
    tril = jnp.where(tril, 1.0, 0.0).astype(BF16)
    tril3 = jnp.concatenate([tril, tril, tril], axis=1)
    masks = []
    for g in range(n_half):
        r = lax.broadcasted_iota(jnp.int32, (half, half * (g + 1)), 0)
        c = lax.broadcasted_iota(jnp.int32, (half, half * (g + 1)), 1)
        masks.append(r + g * half >= c)
    head_slices = [slice(h * HEAD_DIM, (h + 1) * HEAD_DIM) for h in range(heads)]

    def operands():
        for hs in head_slices:
            q = q_ref[rows, hs].astype(F32)
            lb = lb_ref[:, hs]
            c1 = 0.5 * (1.0 - lb)
            f = (lb + c1) + c1 * jnp.tanh(0.5 * f_ref[rows, hs].astype(F32))
            kk = 1.0 - f
            b = _cumsum_rows(jnp.log(f), tril3) * LOG2_E
            b_last = b[chunk - 1:chunk, :]
            dec_ref[0:1, hs] = jnp.exp2(b_last)
            mids, ks = [], []
            for g in range(n_half):
                rg = slice(g * half, (g + 1) * half)
                mid = b[g * half + half // 2 - 1:g * half + half // 2, :]
                qs_g = q[rg] * jnp.exp2(b[rg] - mid)
                ks_g = kk[rg] * jnp.exp2(mid - b[rg])
                mids.append(mid)
                ks.append(ks_g)
                qs_ref[rg, hs] = qs_g.astype(BF16)
                qe_ref[rg, hs] = (qs_g * jnp.exp2(mid)).astype(BF16)
                kd_ref[rg, hs] = (ks_g * jnp.exp2(b_last - mid)).astype(BF16)
                for gp in range(g):
                    key_refs[g][gp * half:(gp + 1) * half, hs] = (
                        ks[gp] * jnp.exp2(mid - mids[gp])).astype(BF16)
                key_refs[g][rg, hs] = ks_g.astype(BF16)

    def scores():
        for h, hs in enumerate(head_slices):
            for g in range(n_half):
                rg = slice(g * half, (g + 1) * half)
                a = _dot_nt(qs_ref[rg, hs], key_refs[g][:, hs])
                a_ref[h, rg, 0:half * (g + 1)] = jnp.where(masks[g], a, 0.0).astype(BF16)

    def outputs_and_state():
        for h, hs in enumerate(head_slices):
            v = i_ref[rows, hs]
            st = st_ref[h]
            if reset is not None:
                st = jnp.where(reset, 0.0, st)
            o_inter = _dot_nt(qe_ref[:, hs], st.astype(BF16))
            for g in range(n_half):
                rg = slice(g * half, (g + 1) * half)
                o_ref[rg, hs] = o_inter[rg] + _dot(a_ref[h, rg, 0:half * (g + 1)], v[0:half * (g + 1)])
            st_ref[h] = st * dec_ref[0:1, hs] + _dot_tn(v, kd_ref[:, hs])

    def normalise():
        for hs in head_slices:
            o = o_ref[:, hs]
            o_n = o * lax.rsqrt(jnp.mean(o * o, axis=-1, keepdims=True) + EPS) * (0.5 * gh_ref[:, hs])
            gate2 = jnp.tanh(0.5 * zo_ref[rows, hs].astype(F32)) + 1.0
            og_ref[og_rows, hs] = (o_n * gate2).astype(og_ref.dtype)

    return [operands, scores, outputs_and_state, normalise]


def _hgrn_body(q_ref, f_ref, i_ref, zo_ref, s0_ref, lb_ref, gh_ref, og_ref, sout_ref, *scratch,
               chunk, n_chunks, heads):
    ti = pl.program_id(2)
    st_ref = scratch[0]

    @pl.when(ti == 0)
    def _():
        for h in range(heads):
            st_ref[h] = s0_ref[0, h].T

    def chunk_step(ci, carry):
        rows = pl.ds(pl.multiple_of(ci * chunk, chunk), chunk)
        _hgrn_chunk(q_ref, f_ref, i_ref, zo_ref, lb_ref, gh_ref, og_ref, scratch, rows, rows, chunk, heads)
        return carry

    lax.fori_loop(0, n_chunks, chunk_step, 0)

    @pl.when(ti == pl.num_programs(2) - 1)
    def _():
        for h in range(heads):
            sout_ref[0, h] = st_ref[h].T


def _hgrn(z, s0, lb, g_head, *, row_base, batch, seq, d_model, name):
    n_heads = d_model // HEAD_DIM
    heads = min(16, n_heads)
    chunk = min(REC_CHUNK, seq)
    half = min(REC_HALF, chunk)
    assert seq % chunk == 0 and chunk % half == 0
    rt = max(_pick(seq, (512, 256, 128, 64, 32)), chunk)
    assert row_base % rt == 0
    tiles = seq // rt
    wblk = heads * HEAD_DIM
    cpb = d_model // wblk

    def zspec(section):
        return pl.BlockSpec((rt, wblk),
                            lambda b, hg, i: (row_base // rt + b * tiles + i, section * cpb + hg))

    return pl.pallas_call(
        functools.partial(_hgrn_body, chunk=chunk, n_chunks=rt // chunk, heads=heads),
        grid=(batch, n_heads // heads, tiles),
        in_specs=[zspec(0), zspec(1), zspec(2), zspec(3),
                  pl.BlockSpec((1, heads, HEAD_DIM, HEAD_DIM), lambda b, hg, i: (b, hg, 0, 0)),
                  pl.BlockSpec((1, wblk), lambda b, hg, i: (0, hg)),
                  pl.BlockSpec((1, wblk), lambda b, hg, i: (0, hg))],
        out_specs=[pl.BlockSpec((rt, wblk), lambda b, hg, i: (b * tiles + i, hg)),
                   pl.BlockSpec((1, heads, HEAD_DIM, HEAD_DIM), lambda b, hg, i: (b, hg, 0, 0))],
        out_shape=[jax.ShapeDtypeStruct((batch * seq, d_model), BF16),
                   jax.ShapeDtypeStruct((batch, n_heads, HEAD_DIM, HEAD_DIM), F32)],
        scratch_shapes=_hgrn_scratch(chunk, heads),
        compiler_params=pltpu.CompilerParams(dimension_semantics=("parallel", "parallel", "arbitrary"),
                                             vmem_limit_bytes=VMEM_LIMIT),
        name=name,
    )(z, z, z, z, s0, lb, g_head)


def _mix_body(zu_ref, zv_ref, zga_ref, zgb_ref, ogp_ref, ogs_ref, xp_ref, xs_ref, lng_ref, lnb_ref,
              wsp_ref, wss_ref, bsp_ref, bss_ref, wpa_hbm, wpb_hbm, wo_hbm,
              h1_ref, vnp_ref, vns_ref, sg_ref, wpa_ref, wpb_ref, wo_ref, stage_ref, wsem,
              *, n_prompt_tiles, dec_seq):
    @pl.when(pl.program_id(0) == 0)
    def _():
        _load_as_bf16(wpa_hbm, wpa_ref, stage_ref, wsem)
        _load_as_bf16(wpb_hbm, wpb_ref, stage_ref, wsem)
        _load_as_bf16(wo_hbm, wo_ref, stage_ref, wsem)

    is_p = pl.program_id(0) < n_prompt_tiles
    tm, width = zu_ref.shape
    gd = width // MLP_GROUPS
    u = _gelu(zu_ref[...].astype(F32))
    gv = _gelu(zv_ref[...].astype(F32))
    xc = gv - jnp.mean(gv, axis=-1, keepdims=True)
    vn = xc * lax.rsqrt(jnp.mean(xc * xc, axis=-1, keepdims=True) + EPS) * lng_ref[...] + lnb_ref[...]

    @pl.when(is_p)
    def _():
        vnp_ref[...] = vn

    @pl.when(jnp.logical_not(is_p))
    def _():
        vns_ref[...] = vn

    vnb = vn.astype(BF16)
    r = lax.broadcasted_iota(jnp.int32, (MLP_CHUNK, MLP_CHUNK), 0)
    c = lax.broadcasted_iota(jnp.int32, (MLP_CHUNK, MLP_CHUNK), 1)
    causal = r >= c
    same_stream = (r // dec_seq) == (c // dec_seq)
    for g in range(MLP_GROUPS):
        w_p = jnp.where(causal, wsp_ref[g], 0.0)
        w_s = jnp.where(causal & same_stream, wss_ref[g], 0.0)
        w = jnp.where(is_p, w_p, w_s).astype(BF16)
        bias = jnp.where(is_p, bsp_ref[g], bss_ref[g])
        for cc in range(tm // MLP_CHUNK):
            rows = slice(cc * MLP_CHUNK, (cc + 1) * MLP_CHUNK)
            cols = slice(g * gd, (g + 1) * gd)
            s = _dot(w, vnb[rows, cols]) + bias
            sg_ref[rows, cols] = (u[rows, cols] * s).astype(BF16)
    y_b = _dot(sg_ref[...], wpb_ref[...])
    og = jnp.where(is_p, ogp_ref[...].astype(F32), ogs_ref[...].astype(F32)).astype(BF16)
    y_a = _dot(og, wpa_ref[...])
    m = _sigmoid(zga_ref[...].astype(F32)) * y_a + _sigmoid(zgb_ref[...].astype(F32)) * y_b
    x = jnp.where(is_p, xp_ref[...], xs_ref[...])
    h1_ref[...] = x + _dot(m.astype(BF16), wo_ref[...])


def _mix(z, og_p, og_s, xp, xs, ln_g, ln_b, ws_p, ws_s, bs_p, bs_s, w_pa, w_pb, w_o, *, batch, dec_seq):
    (n_p, d), n_s = xp.shape, xs.shape[0]
    n = n_p + n_s
    width = w_pb.shape[0]
    tm = 256
    assert n_s % tm == 0 and (n_p // batch) % tm == 0 and tm % MLP_CHUNK == 0 and MLP_CHUNK % dec_seq == 0
    npt, nst = n_p // tm, n_s // tm
    tpb = npt // batch
    u_blk = 4 * d // width
    ga_blk = (4 * d + 2 * width) // d
    const2 = lambda i: (0, 0)
    const3 = lambda i: (0, 0, 0)
    p_map = lambda i: (jnp.minimum(i, npt - 1), 0)
    s_map = lambda i: (jnp.clip(i - npt, 0, nst - 1), 0)
    return pl.pallas_call(
        functools.partial(_mix_body, n_prompt_tiles=npt, dec_seq=dec_seq),
        grid=(n // tm,),
        in_specs=[pl.BlockSpec((tm, width), lambda i: (i, u_blk)),
                  pl.BlockSpec((tm, width), lambda i: (i, u_blk + 1)),
                  pl.BlockSpec((tm, d), lambda i: (i, ga_blk)),
                  pl.BlockSpec((tm, d), lambda i: (i, ga_blk + 1)),
                  pl.BlockSpec((tm, d), p_map),
                  pl.BlockSpec((tm, d), s_map),
                  pl.BlockSpec((tm, d), p_map),
                  pl.BlockSpec((tm, d), s_map),
                  pl.BlockSpec((1, width), const2),
                  pl.BlockSpec((1, width), const2),
                  pl.BlockSpec((MLP_GROUPS, MLP_CHUNK, MLP_CHUNK), const3),
                  pl.BlockSpec((MLP_GROUPS, MLP_CHUNK, MLP_CHUNK), const3),
                  pl.BlockSpec((MLP_GROUPS, MLP_CHUNK, 1), const3),
                  pl.BlockSpec((MLP_GROUPS, MLP_CHUNK, 1), const3),
                  pl.BlockSpec(memory_space=pl.ANY),
                  pl.BlockSpec(memory_space=pl.ANY),
                  pl.BlockSpec(memory_space=pl.ANY)],
        out_specs=[pl.BlockSpec((tm, d), lambda i: (i, 0)),
                   pl.BlockSpec((tm, width), lambda i: (jnp.minimum(i // tpb, batch - 1), 0)),
                   pl.BlockSpec((tm, width), s_map)],
        out_shape=[jax.ShapeDtypeStruct((n, d), F32),
                   jax.ShapeDtypeStruct((batch * tm, width), F32),
                   jax.ShapeDtypeStruct((n_s, width), F32)],
        scratch_shapes=[pltpu.VMEM((tm, width), BF16),
                        pltpu.VMEM((d, d), BF16), pltpu.VMEM((width, d), BF16), pltpu.VMEM((d, d), BF16),
                        pltpu.VMEM((2 * WEIGHT_STAGE_ROWS, d), F32), pltpu.SemaphoreType.DMA((2,))],
        compiler_params=pltpu.CompilerParams(dimension_semantics=("arbitrary",),
                                             vmem_limit_bytes=VMEM_LIMIT),
        name="mix",
    )(z, z, z, z, og_p, og_s, xp, xs, ln_g, ln_b, ws_p, ws_s, bs_p, bs_s, w_pa, w_pb, w_o)


def _router_body(h_ref, g_ref, wr_ref, br_ref, c_ref, idx_ref, gate_ref, cnt_ref, carry_ref):
    @pl.when(pl.program_id(0) == 0)
    def _():
        carry_ref[...] = jnp.zeros_like(carry_ref)

    tm = h_ref.shape[0]
    c = _rms(h_ref[...], g_ref[...])
    wr = wr_ref[...]
    wr_hi = wr.astype(BF16)
    wr_lo = (wr - wr_hi.astype(F32)).astype(BF16)
    c_hi = c.astype(BF16)
    c_lo = (c - c_hi.astype(F32)).astype(BF16)
    _store_chunks(c_ref, _pack_bf16_pairs(c_hi.astype(F32)))
    both = _dot_nt(jnp.concatenate([wr_hi, wr_lo], axis=0), c_hi)
    lt = both[0:ROUTER_ROWS] + both[ROUTER_ROWS:2 * ROUTER_ROWS] + _dot_nt(wr_hi, c_lo) + br_ref[...]
    le = lt[0:N_EXPERTS]
    lg = lt[N_EXPERTS:N_EXPERTS + N_GROUPS]
    gmax = jnp.max(lg, axis=0, keepdims=True)
    p_sel = 1.0 / jnp.sum(jnp.exp(lg - gmax), axis=0, keepdims=True)
    best = lg[0:1]
    gi = jnp.zeros((1, tm), jnp.int32)
    for g in range(1, N_GROUPS):
        better = lg[g:g + 1] > best
        gi = jnp.where(better, g, gi)
        best = jnp.where(better, lg[g:g + 1], best)
    leg = jnp.zeros((E_PER_GROUP, tm), F32)
    for g in range(N_GROUPS):
        leg = jnp.where(gi == g, le[g * E_PER_GROUP:(g + 1) * E_PER_GROUP], leg)
    sub = lax.broadcasted_iota(jnp.int32, (E_PER_GROUP, tm), 0).astype(F32)
    v1 = jnp.max(leg, axis=0, keepdims=True)
    i1 = jnp.min(jnp.where(leg == v1, sub, float(E_PER_GROUP)), axis=0, keepdims=True)
    rest = jnp.where(sub == i1, -jnp.inf, leg)
    v2 = jnp.max(rest, axis=0, keepdims=True)
    i2 = jnp.min(jnp.where(rest == v2, sub, float(E_PER_GROUP)), axis=0, keepdims=True)
    e2 = jnp.exp(v2 - v1)
    den = 1.0 + e2
    gate0 = p_sel * (1.0 / den)
    gate1 = p_sel * (e2 / den)
    ex0 = gi * E_PER_GROUP + i1.astype(jnp.int32)
    ex1 = gi * E_PER_GROUP + i2.astype(jnp.int32)
    eid = lax.broadcasted_iota(jnp.int32, (N_EXPERTS, tm), 0)
    oh0 = eid == ex0
    oh1 = eid == ex1
    oh = jnp.where(oh0 | oh1, 1.0, 0.0)
    upper = jnp.where(lax.broadcasted_iota(jnp.int32, (tm, tm), 0) < lax.broadcasted_iota(jnp.int32, (tm, tm), 1),
                      1.0, 0.0).astype(BF16)
    before = _dot(oh.astype(BF16), upper) + carry_ref[:, 0:1]
    rank0 = jnp.sum(jnp.where(oh0, before, 0.0), axis=0, keepdims=True)
    rank1 = jnp.sum(jnp.where(oh1, before, 0.0), axis=0, keepdims=True)
    carry = carry_ref[...] + jnp.sum(oh, axis=1, keepdims=True)
    carry_ref[...] = carry
    cnt_ref[...] = carry.astype(jnp.int32)
    idx_ref[...] = jnp.zeros_like(idx_ref)
    idx_ref[0:1, :] = ex0
    idx_ref[1:2, :] = ex1
    idx_ref[2:3, :] = rank0.astype(jnp.int32)
    idx_ref[3:4, :] = rank1.astype(jnp.int32)
    gate_ref[...] = jnp.zeros_like(gate_ref)
    gate_ref[0:1, :] = gate0
    gate_ref[1:2, :] = gate1


def _router(h1, g_ffn, wr, br):
    n, d = h1.shape
    tm = _pick(n, (512, 256))
    return pl.pallas_call(
        _router_body,
        grid=(n // tm,),
        in_specs=[pl.BlockSpec((tm, d), lambda i: (i, 0)),
                  pl.BlockSpec((1, d), lambda i: (0, 0)),
                  pl.BlockSpec((ROUTER_ROWS, d), lambda i: (0, 0)),
                  pl.BlockSpec((ROUTER_ROWS, 1), lambda i: (0, 0))],
        out_specs=[pl.BlockSpec((d // (2 * LANES), tm, LANES), lambda i: (0, i, 0)),
                   pl.BlockSpec((SUBLANES, tm), lambda i: (0, i)),
                   pl.BlockSpec((SUBLANES, tm), lambda i: (0, i)),
                   pl.BlockSpec((N_EXPERTS, LANES), lambda i: (0, 0))],
        out_shape=[jax.ShapeDtypeStruct((d // (2 * LANES), n, LANES), jnp.uint32),
                   jax.ShapeDtypeStruct((SUBLANES, n), jnp.int32),
                   jax.ShapeDtypeStruct((SUBLANES, n), F32),
                   jax.ShapeDtypeStruct((N_EXPERTS, LANES), jnp.int32)],
        scratch_shapes=[pltpu.VMEM((N_EXPERTS, LANES), F32)],
        compiler_params=pltpu.CompilerParams(dimension_semantics=("arbitrary",),
                                             vmem_limit_bytes=VMEM_LIMIT),
        name="router",
    )(h1, g_ffn, wr, br)


def _dispatch_body(fill_lo_ref, fill_hi_ref, dest_ref, c_ref, xs_ref, zero_ref, sem, zsem):
    td = c_ref.shape[1]

    @pl.when(pl.program_id(0) == 0)
    def _():
        zero_ref[...] = jnp.zeros_like(zero_ref)

        def per_expert(e, carry):
            lo, hi = fill_lo_ref[e], fill_hi_ref[e]

            def start(r, c2):
                _row_copy(zero_ref, 0, xs_ref, r, zsem).start()
                return c2

            def wait(r, c2):
                _row_copy(zero_ref, 0, xs_ref, 0, zsem).wait()
                return c2

            lax.fori_loop(lo, hi, start, 0)
            lax.fori_loop(lo, hi, wait, 0)
            return carry

        lax.fori_loop(0, N_EXPERTS, per_expert, 0)

        def block_copy(j):
            return _rows_copy(zero_ref, xs_ref, j * MOE_BLOCK, MOE_BLOCK, zsem)

        first_unused = fill_hi_ref[N_EXPERTS - 1] // MOE_BLOCK
        n_blocks = xs_ref.shape[1] // MOE_BLOCK

        def start_block(j, carry):
            block_copy(j).start()
            return carry

        def wait_block(j, carry):
            block_copy(0).wait()
            return carry

        lax.fori_loop(first_unused, n_blocks, start_block, 0)
        lax.fori_loop(first_unused, n_blocks, wait_block, 0)

    for t in range(td):
        for k in range(TOP_K):
            _row_copy(c_ref, t, xs_ref, dest_ref[k, t], sem).start(priority=k % 2)
    for k in range(TOP_K):
        _rows_copy(c_ref, xs_ref, 0, td, sem).wait()


def _dispatch(fill_lo, fill_hi, dest, c, n_slots):
    nc, n, _ = c.shape
    td = _pick(n, (512, 256))
    grid_spec = pltpu.PrefetchScalarGridSpec(
        num_scalar_prefetch=2,
        grid=(n // td,),
        in_specs=[pl.BlockSpec((TOP_K, td), lambda i, lo, hi: (0, i), memory_space=pltpu.SMEM),
                  pl.BlockSpec((nc, td, LANES), lambda i, lo, hi: (0, i, 0))],
        out_specs=pl.BlockSpec(memory_space=pl.ANY),
        scratch_shapes=[pltpu.VMEM((nc, MOE_BLOCK, LANES), c.dtype),
                        pltpu.SemaphoreType.DMA(()), pltpu.SemaphoreType.DMA(())],
    )
    return pl.pallas_call(
        _dispatch_body,
        grid_spec=grid_spec,
        out_shape=jax.ShapeDtypeStruct((nc, n_slots, LANES), c.dtype),
        compiler_params=pltpu.CompilerParams(dimension_semantics=("arbitrary",),
                                             vmem_limit_bytes=VMEM_LIMIT),
        name="dispatch",
    )(fill_lo, fill_hi, dest, c)


def _expert_body(be_ref, na_ref, nxt_ref, xs_ref, wg_hbm, wu_hbm, wd_hbm, yb_ref,
                 wg_ref, wu_ref, wd_ref, sg_ref, su_ref, sd_ref, wsem):
    j = pl.program_id(0)
    active = j < na_ref[0]
    e = be_ref[j]
    first = (j == 0) | (be_ref[jnp.maximum(j - 1, 0)] != e)

    def weight_copies(expert):
        return (pltpu.make_async_copy(wg_hbm.at[expert], sg_ref, wsem.at[0]),
                pltpu.make_async_copy(wu_hbm.at[expert], su_ref, wsem.at[1]),
                pltpu.make_async_copy(wd_hbm.at[expert], sd_ref, wsem.at[2]))

    @pl.when(j == 0)
    def _():
        for cp in weight_copies(e):
            cp.start()

    @pl.when(active & first)
    def _():
        for cp, stage, dst in zip(weight_copies(e), (sg_ref, su_ref, sd_ref), (wg_ref, wu_ref, wd_ref)):
            cp.wait()

            def convert(r, carry, stage=stage, dst=dst):
                rows = pl.ds(pl.multiple_of(r * LANES, LANES), LANES)
                dst[rows, :] = stage[rows, :].astype(BF16)
                return carry

            lax.fori_loop(0, stage.shape[0] // LANES, convert, 0)

        @pl.when(nxt_ref[e] < N_EXPERTS)
        def _():
            for cp in weight_copies(nxt_ref[e]):
                cp.start()

    @pl.when(active)
    def _():
        x = _unpack_bf16_pairs(_load_chunks(xs_ref))
        a = _dot(x, wg_ref[...])
        b = _dot(x, wu_ref[...])
        hid = (a * _sigmoid(a)) * b
        _store_chunks(yb_ref, _dot(hid.astype(BF16), wd_ref[...]))

    @pl.when(jnp.logical_not(active))
    def _():
        yb_ref[...] = jnp.zeros_like(yb_ref)


def _experts(block_exp, n_active, next_exp, xs, w_gate, w_up, w_down):
    nc_in, n_slots, _ = xs.shape
    d, de = w_gate.shape[1:]
    nc = d // LANES
    nb = n_slots // MOE_BLOCK
    assert d % LANES == 0 and de % LANES == 0
    blk = lambda j, be, na, nxt: (0, jnp.minimum(j, na[0] - 1), 0)
    grid_spec = pltpu.PrefetchScalarGridSpec(
        num_scalar_prefetch=3,
        grid=(nb,),
        in_specs=[pl.BlockSpec((nc_in, MOE_BLOCK, LANES), blk),
                  pl.BlockSpec(memory_space=pl.ANY),
                  pl.BlockSpec(memory_space=pl.ANY),
                  pl.BlockSpec(memory_space=pl.ANY)],
        out_specs=pl.BlockSpec((nc, MOE_BLOCK, LANES), lambda j, be, na, nxt: (0, j, 0)),
        scratch_shapes=[pltpu.VMEM((d, de), BF16), pltpu.VMEM((d, de), BF16), pltpu.VMEM((de, d), BF16),
                        pltpu.VMEM((d, de), F32), pltpu.VMEM((d, de), F32), pltpu.VMEM((de, d), F32),
                        pltpu.SemaphoreType.DMA((3,))],
    )
    return pl.pallas_call(
        _expert_body,
        grid_spec=grid_spec,
        out_shape=jax.ShapeDtypeStruct((nc, n_slots, LANES), F32),
        compiler_params=pltpu.CompilerParams(dimension_semantics=("arbitrary",),
                                             vmem_limit_bytes=VMEM_LIMIT),
        name="experts",
    )(block_exp, n_active, next_exp, xs, w_gate, w_up, w_down)


def _combine_body(dcur_ref, dnxt_ref, gate_ref, h1_ref, p_ref, yb_ref, gple_ref, wpg_hbm, wpp_hbm, gfin_ref,
                  yp_ref, ys_ref, buf_ref, sem, wpg_ref, wpp_ref, stage_ref, wsem,
                  *, n_prompt_tiles, final_norm):
    i = pl.program_id(0)

    @pl.when(i == 0)
    def _():
        _load_as_bf16(wpg_hbm, wpg_ref, stage_ref, wsem)
        _load_as_bf16(wpp_hbm, wpp_ref, stage_ref, wsem)

    n_tiles = pl.num_programs(0)
    tc = h1_ref.shape[0]
    slot = lax.rem(i, 2)

    @pl.when(i == 0)
    def _():
        def body(t, carry):
            for k in range(TOP_K):
                _row_copy(yb_ref, dcur_ref[k, t], buf_ref.at[0, k], t, sem.at[0]).start()
            return carry

        lax.fori_loop(0, tc, body, 0)

    def wait_slot(s):
        for k in range(TOP_K):
            pltpu.make_async_copy(yb_ref.at[:, pl.ds(0, tc), :], buf_ref.at[s, k], sem.at[s]).wait()

    wait_slot(slot)
    gt = gate_ref[...].T
    h2 = h1_ref[...] + (gt[:, 0:1] * _load_chunks(buf_ref.at[slot, 0])
                        + gt[:, 1:2] * _load_chunks(buf_ref.at[slot, 1]))
    for t in range(tc):
        for k in range(TOP_K):
            _row_copy(yb_ref, dnxt_ref[k, t], buf_ref.at[1 - slot, k], t, sem.at[1 - slot]).start(priority=k % 2)
    a = _rms(h2, gple_ref[...]).astype(BF16)
    h3 = h2 + _sigmoid(_dot(a, wpg_ref[...])) * _dot(p_ref[...].astype(BF16), wpp_ref[...])
    if final_norm:
        h3 = _rms(h3, gfin_ref[...])

    @pl.when(i < n_prompt_tiles)
    def _():
        yp_ref[...] = h3

    @pl.when(i >= n_prompt_tiles)
    def _():
        ys_ref[...] = h3

    @pl.when(i == n_tiles - 1)
    def _():
        wait_slot(1 - slot)


def _combine(dest, gates, h1, p, yb, g_ple, w_pg, w_pp, g_final, *, n_prompt, final_norm):
    n, d = h1.shape
    ple = p.shape[1]
    tc = 256
    assert n_prompt % tc == 0 and (n - n_prompt) % tc == 0
    nt, npt = n // tc, n_prompt // tc
    nst = nt - npt
    const2 = lambda i: (0, 0)
    return pl.pallas_call(
        functools.partial(_combine_body, n_prompt_tiles=npt, final_norm=final_norm),
        grid=(nt,),
        in_specs=[pl.BlockSpec((TOP_K, tc), lambda i: (0, i), memory_space=pltpu.SMEM),
                  pl.BlockSpec((TOP_K, tc), lambda i: (0, jnp.minimum(i + 1, nt - 1)), memory_space=pltpu.SMEM),
                  pl.BlockSpec((SUBLANES, tc), lambda i: (0, i)),
                  pl.BlockSpec((tc, d), lambda i: (i, 0)),
                  pl.BlockSpec((tc, ple), lambda i: (i, 0)),
                  pl.BlockSpec(memory_space=pl.ANY),
                  pl.BlockSpec((1, d), const2),
                  pl.BlockSpec(memory_space=pl.ANY),
                  pl.BlockSpec(memory_space=pl.ANY),
                  pl.BlockSpec((1, d), const2)],
        out_specs=[pl.BlockSpec((tc, d), lambda i: (jnp.minimum(i, npt - 1), 0)),
                   pl.BlockSpec((tc, d), lambda i: (jnp.clip(i - npt, 0, nst - 1), 0))],
        out_shape=[jax.ShapeDtypeStruct((n_prompt, d), F32),
                   jax.ShapeDtypeStruct((n - n_prompt, d), F32)],
        scratch_shapes=[pltpu.VMEM((2, TOP_K, d // LANES, tc, LANES), F32), pltpu.SemaphoreType.DMA((2,)),
                        pltpu.VMEM((d, d), BF16), pltpu.VMEM((ple, d), BF16),
                        pltpu.VMEM((WEIGHT_STAGE_ROWS, d), F32), pltpu.SemaphoreType.DMA((2,))],
        compiler_params=pltpu.CompilerParams(dimension_semantics=("arbitrary",),
                                             vmem_limit_bytes=VMEM_LIMIT),
        name="combine",
    )(dest, dest, gates, h1, p, yb, g_ple, w_pg, w_pp, g_final)


def _layer(hp, hs, p, s0_sample, lb, batch, seq, dec_batch, dec_seq,
           g_mix, w_in, g_head, w_pa, ln_v_g, ln_v_b, w_s, b_s, w_pb, w_o,
           g_ffn, w_gr, b_gr, w_er, b_er, w_gate, w_up, w_down, g_ple, w_pg, w_pp, g_final, final_norm):
    n_prompt, d = hp.shape
    n = n_prompt + hs.shape[0]
    n_heads = d // HEAD_DIM
    row = lambda a: a.reshape(1, -1).astype(F32)

    z = _in_proj(hp, hs, row(g_mix), w_in)

    lb_row, gh_row = row(lb), row(g_head)
    s0_prompt = jnp.zeros((batch, n_heads, HEAD_DIM, HEAD_DIM), F32)
    og_p, st_p = _hgrn(z, s0_prompt, lb_row, gh_row, row_base=0, batch=batch, seq=seq,
                       d_model=d, name="hgrn_prompt")
    og_s, st_s = _hgrn(z, s0_sample.astype(F32), lb_row, gh_row, row_base=n_prompt, batch=dec_batch,
                       seq=dec_seq, d_model=d, name="hgrn_sample")

    start = PAST_LEN % MLP_CHUNK
    assert start + dec_seq <= MLP_CHUNK
    rep = MLP_CHUNK // dec_seq
    ws_s = jnp.tile(w_s[:, start:start + dec_seq, start:start + dec_seq], (1, rep, rep))
    bs_s = jnp.tile(b_s[:, start:start + dec_seq], (1, rep))
    h1, vn_p, vn_s = _mix(z, og_p, og_s, hp, hs, row(ln_v_g), row(ln_v_b), w_s, ws_s, b_s[..., None],
                          bs_s[..., None], w_pa, w_pb, w_o,
                          batch=batch, dec_seq=dec_seq)

    pad_rows = ROUTER_ROWS - N_EXPERTS - N_GROUPS
    wr = jnp.concatenate([w_er.T, w_gr.T, jnp.zeros((pad_rows, d), F32)], axis=0)
    br = jnp.concatenate([b_er, b_gr, jnp.zeros((pad_rows,), F32)]).reshape(ROUTER_ROWS, 1).astype(F32)
    c, idx, gates, cnt = _router(h1, row(g_ffn), wr, br)

    counts = cnt[:, 0]
    padded = (counts + MOE_BLOCK - 1) // MOE_BLOCK * MOE_BLOCK
    pad_end = jnp.cumsum(padded).astype(jnp.int32)
    pad_start = pad_end - padded
    n_blocks = -(-(n * TOP_K) // MOE_BLOCK) + N_EXPERTS
    block_first = jnp.arange(n_blocks, dtype=jnp.int32) * MOE_BLOCK
    block_exp = jnp.minimum(jnp.sum(pad_end[None, :] <= block_first[:, None], axis=1), N_EXPERTS - 1).astype(jnp.int32)
    n_active = jnp.maximum(pad_end[-1:] // MOE_BLOCK, 1)
    hit = idx[0:TOP_K, :, None] == jnp.arange(N_EXPERTS, dtype=jnp.int32)
    dest = jnp.sum(jnp.where(hit, pad_start, 0), axis=-1) + idx[TOP_K:2 * TOP_K]

    xs = _dispatch(pad_start + counts, pad_end, dest, c, n_blocks * MOE_BLOCK)
    used = jnp.where(counts > 0, jnp.arange(N_EXPERTS, dtype=jnp.int32), N_EXPERTS)
    next_exp = jnp.concatenate([lax.cummin(used, axis=0, reverse=True)[1:],
                                jnp.full((1,), N_EXPERTS, jnp.int32)]).astype(jnp.int32)
    yb = _experts(block_exp, n_active, next_exp, xs, w_gate, w_up, w_down)
    yp, ys = _combine(dest, gates, h1, p, yb, row(g_ple), w_pg, w_pp,
                      row(g_final), n_prompt=n_prompt, final_norm=final_norm)
    return yp, ys, st_p, st_s, vn_p, vn_s


def kernel(x_prompt, x_sample, p_prompt, p_sample, state_hgrn, g_mix, w_in, lb_logits, g_head, w_pa, ln_v_g, ln_v_b, w_s, b_s, w_pb, w_o, g_ffn, w_gr, b_gr, w_er, b_er, w_gate, w_up, w_down, g_ple, w_pg, w_pp, g_final):
    batch, seq, d = x_prompt.shape
    dec_batch, dec_seq, _ = x_sample.shape
    depth = g_mix.shape[0]
    n_prompt, n_sample = batch * seq, dec_batch * dec_seq
    width = w_pb.shape[1]
    lbs = jnp.cumsum(jax.nn.softmax(lb_logits.astype(F32), axis=0), axis=0)
    hp, hs = x_prompt.reshape(n_prompt, d), x_sample.reshape(n_sample, d)
    keep = min((seq - 1) % MLP_CHUNK + 1, seq)
    keep_s = min((PAST_LEN % MLP_CHUNK + dec_seq - 1) % MLP_CHUNK + 1, dec_seq)
    sp, ss, vp, vs = [], [], [], []
    for i in range(depth):
        p = jnp.concatenate([p_prompt[i].reshape(n_prompt, -1), p_sample[i].reshape(n_sample, -1)], axis=0)
        hp, hs, st_p, st_s, vn_p, vn_s = _layer(
            hp, hs, p, state_hgrn[i], lbs[i], batch, seq, dec_batch, dec_seq,
            g_mix[i], w_in[i], g_head[i], w_pa[i], ln_v_g[i], ln_v_b[i], w_s[i], b_s[i], w_pb[i], w_o[i],
            g_ffn[i], w_gr[i], b_gr[i], w_er[i], b_er[i], w_gate[i], w_up[i], w_down[i],
            g_ple[i], w_pg[i], w_pp[i], g_final, i == depth - 1)
        sp.append(st_p.astype(x_prompt.dtype))
        ss.append(st_s.astype(state_hgrn.dtype))
        vn_last = vn_p.reshape(batch, -1, width)
        vp.append(vn_last[:, vn_last.shape[1] - keep:])
        vs.append(vn_s.reshape(dec_batch, dec_seq, width)[:, dec_seq - keep_s:])
    y_prompt = hp.reshape(batch, seq, d)
    y_sample = hs.reshape(dec_batch, dec_seq, d)
    return (y_prompt, y_sample, jnp.stack(sp), jnp.stack(ss), jnp.stack(vp), jnp.stack(vs))
```

```python
import functools
import math

import jax
import jax.numpy as jnp
from jax import lax
from jax.experimental import pallas as pl
from jax.experimental.pallas import tpu as pltpu

F32 = jnp.float32
BF16 = jnp.bfloat16

EPS = 1e-6
HEAD_DIM = 128
REC_CHUNK = 64
REC_HALF = 32
MLP_CHUNK = 128
MLP_GROUPS = 4
N_GROUPS = 4
E_PER_GROUP = 8
N_EXPERTS = N_GROUPS * E_PER_GROUP
TOP_K = 2
MOE_BLOCK = 256
PAST_LEN = 2048
LANES = 128
SUBLANES = 8
ROUTER_ROWS = 48
VMEM_LIMIT = 56 * 1024 * 1024
LOG2_E = 1.4426950408889634
WEIGHT_STAGE_ROWS = 256


def _pick(n, candidates):
    for c in candidates:
        if n % c == 0:
            return c
    raise ValueError(f"no tile in {candidates} divides {n}")


def _dot(a, b):
    return jnp.dot(a, b, preferred_element_type=F32)


def _dot_nt(a, b, precision=None):
    return lax.dot_general(a, b, (((1,), (1,)), ((), ())), precision=precision, preferred_element_type=F32)


def _dot_tn(a, b):
    return lax.dot_general(a, b, (((0,), (0,)), ((), ())), preferred_element_type=F32)


def _rms(x, g):
    return x * lax.rsqrt(jnp.mean(x * x, axis=-1, keepdims=True) + EPS) * g


def _gelu(x):
    c = math.sqrt(2.0 / math.pi)
    return x * (0.5 * (1.0 + jnp.tanh(c * (x + 0.044715 * (x * x * x)))))


def _sigmoid(x):
    return 0.5 * jnp.tanh(0.5 * x) + 0.5


def _store_chunks(ref, x):
    for c in range(ref.shape[0]):
        ref[c] = x[:, c * LANES:(c + 1) * LANES].astype(ref.dtype)


def _load_chunks(ref):
    return jnp.concatenate([ref[c] for c in range(ref.shape[0])], axis=1)


HIGH_HALF = 0xFFFF0000


def _pack_bf16_pairs(x):
    half = x.shape[1] // 2
    lo = jnp.right_shift(pltpu.bitcast(x[:, :half], jnp.uint32), jnp.uint32(16))
    hi = jnp.bitwise_and(pltpu.bitcast(x[:, half:], jnp.uint32), jnp.uint32(HIGH_HALF))
    return jnp.bitwise_or(lo, hi)


def _unpack_bf16_pairs(p):
    lo = pltpu.bitcast(jnp.left_shift(p, jnp.uint32(16)), F32)
    hi = pltpu.bitcast(jnp.bitwise_and(p, jnp.uint32(HIGH_HALF)), F32)
    return jnp.concatenate([lo, hi], axis=1).astype(BF16)


def _row_copy(src_ref, src_row, dst_ref, dst_row, sem):
    return pltpu.make_async_copy(src_ref.at[:, pl.ds(src_row, 1), :], dst_ref.at[:, pl.ds(dst_row, 1), :], sem)


def _rows_copy(src_ref, dst_ref, dst_row, n_rows, sem):
    return pltpu.make_async_copy(src_ref, dst_ref.at[:, pl.ds(dst_row, n_rows), :], sem)


def _load_as_bf16(w_hbm, w_vmem, stage_ref, sem):
    half = stage_ref.shape[0] // 2
    rows, cols = min(half, w_hbm.shape[0]), min(stage_ref.shape[1], w_hbm.shape[1])
    assert w_hbm.shape[0] % rows == 0 and w_hbm.shape[1] % cols == 0
    chunks = [(r, c) for r in range(0, w_hbm.shape[0], rows) for c in range(0, w_hbm.shape[1], cols)]

    def slot(i):
        return stage_ref.at[pl.ds((i % 2) * half, rows), pl.ds(0, cols)]

    def copy(i):
        r, c = chunks[i]
        return pltpu.make_async_copy(w_hbm.at[pl.ds(r, rows), pl.ds(c, cols)], slot(i), sem.at[i % 2])

    copy(0).start()
    for i, (r, c) in enumerate(chunks):
        if i + 1 < len(chunks):
            copy(i + 1).start()
        copy(i).wait()
        w_vmem[pl.ds(r, rows), pl.ds(c, cols)] = slot(i)[...].astype(BF16)


def _in_proj_body(xp_ref, xs_ref, g_ref, w_ref, z_ref, xn_ref, *, n_prompt_tiles):
    first = pl.program_id(1) == 0
    is_p = pl.program_id(0) < n_prompt_tiles

    @pl.when(first & is_p)
    def _():
        xn_ref[...] = _rms(xp_ref[...], g_ref[...]).astype(BF16)

    @pl.when(first & jnp.logical_not(is_p))
    def _():
        xn_ref[...] = _rms(xs_ref[...], g_ref[...]).astype(BF16)

    z_ref[...] = _dot(xn_ref[...], w_ref[...].astype(BF16)).astype(z_ref.dtype)


def _in_proj(xp, xs, g, w):
    (n_p, d), n_s = xp.shape, xs.shape[0]
    width = w.shape[1]
    tm = _pick(math.gcd(n_p, n_s), (1024, 512, 256))
    tn = _pick(width, (1024, 512, 256, 128))
    npt, nst = n_p // tm, n_s // tm
    return pl.pallas_call(
        functools.partial(_in_proj_body, n_prompt_tiles=npt),
        grid=(npt + nst, width // tn),
        in_specs=[pl.BlockSpec((tm, d), lambda i, j: (jnp.minimum(i, npt - 1), 0)),
                  pl.BlockSpec((tm, d), lambda i, j: (jnp.clip(i - npt, 0, nst - 1), 0),
                               pipeline_mode=pl.Buffered(1)),
                  pl.BlockSpec((1, d), lambda i, j: (0, 0)),
                  pl.BlockSpec((d, tn), lambda i, j: (0, j))],
        out_specs=pl.BlockSpec((tm, tn), lambda i, j: (i, j)),
        out_shape=jax.ShapeDtypeStruct((n_p + n_s, width), BF16),
        scratch_shapes=[pltpu.VMEM((tm, d), BF16)],
        compiler_params=pltpu.CompilerParams(dimension_semantics=("parallel", "arbitrary"),
                                             vmem_limit_bytes=VMEM_LIMIT),
        name="in_proj",
    )(xp, xs, g, w)


def _cumsum_rows(x, tril3):
    hi = x.astype(BF16)
    r1 = x - hi.astype(F32)
    mid = r1.astype(BF16)
    lo = (r1 - mid.astype(F32)).astype(BF16)
    return _dot(tril3, jnp.concatenate([hi, mid, lo], axis=0))


def _hgrn_scratch(chunk, heads):
    half = min(REC_HALF, chunk)
    wblk = heads * HEAD_DIM
    return ([pltpu.VMEM((heads, HEAD_DIM, HEAD_DIM), F32),
             pltpu.VMEM((chunk, wblk), BF16),
             pltpu.VMEM((chunk, wblk), BF16),
             pltpu.VMEM((chunk, wblk), BF16),
             pltpu.VMEM((SUBLANES, wblk), F32),
             pltpu.VMEM((heads, chunk, chunk), BF16),
             pltpu.VMEM((chunk, wblk), F32)]
            + [pltpu.VMEM((half * (g + 1), wblk), BF16) for g in range(chunk // half)])


def _hgrn_chunk(*args, **kwargs):
    for run_pass in _hgrn_passes(*args, **kwargs):
        run_pass()


def _hgrn_passes(q_ref, f_ref, i_ref, zo_ref, lb_ref, gh_ref, og_ref, scratch, rows, og_rows, chunk, heads,
                 reset=None):
    st_ref, qs_ref, qe_ref, kd_ref, dec_ref, a_ref, o_ref = scratch[:7]
    key_refs = scratch[7:]
    half = min(REC_HALF, chunk)
    n_half = chunk // half
    tril = (lax.broadcasted_iota(jnp.int32, (chunk, chunk), 0)
            >= lax.broadcasted_iota(jnp.int32, (chunk, chunk), 1))
    tril = jnp.where(tril, 1.0, 0.0).astype(BF16)
    tril3 = jnp.concatenate([tril, tril, tril], axis=1)
    masks = []
    for g in range(n_half):
        r = lax.broadcasted_iota(jnp.int32, (half, half * (g + 1)), 0)
        c = lax.broadcasted_iota(jnp.int32, (half, half * (g + 1)), 1)
        masks.append(r + g * half >= c)
    head_slices = [slice(h * HEAD_DIM, (h + 1) * HEAD_DIM) for h in range(heads)]

    def operands():
        for hs in head_slices:
            q = q_ref[rows, hs].astype(F32)
            lb = lb_ref[:, hs]
            c1 = 0.5 * (1.0 - lb)
            f = (lb + c1) + c1 * jnp.tanh(0.5 * f_ref[rows, hs].astype(F32))
            kk = 1.0 - f
            b = _cumsum_rows(jnp.log(f), tril3) * LOG2_E
            b_last = b[chunk - 1:chunk, :]
            dec_ref[0:1, hs] = jnp.exp2(b_last)
            mids, ks = [], []
            for g in range(n_half):
                rg = slice(g * half, (g + 1) * half)
                mid = b[g * half + half // 2 - 1:g * half + half // 2, :]
                qs_g = q[rg] * jnp.exp2(b[rg] - mid)
                ks_g = kk[rg] * jnp.exp2(mid - b[rg])
                mids.append(mid)
                ks.append(ks_g)
                qs_ref[rg, hs] = qs_g.astype(BF16)
                qe_ref[rg, hs] = (qs_g * jnp.exp2(mid)).astype(BF16)
                kd_ref[rg, hs] = (ks_g * jnp.exp2(b_last - mid)).astype(BF16)
                for gp in range(g):
                    key_refs[g][gp * half:(gp + 1) * half, hs] = (
                        ks[gp] * jnp.exp2(mid - mids[gp])).astype(BF16)
                key_refs[g][rg, hs] = ks_g.astype(BF16)

    def scores():
        for h, hs in enumerate(head_slices):
            for g in range(n_half):
                rg = slice(g * half, (g + 1) * half)
                a = _dot_nt(qs_ref[rg, hs], key_refs[g][:, hs])
                a_ref[h, rg, 0:half * (g + 1)] = jnp.where(masks[g], a, 0.0).astype(BF16)

    def outputs_and_state():
        for h, hs in enumerate(head_slices):
            v = i_ref[rows, hs]
            st = st_ref[h]
            if reset is not None:
                st = jnp.where(reset, 0.0, st)
            o_inter = _dot_nt(qe_ref[:, hs], st.astype(BF16))
            for g in range(n_half):
                rg = slice(g * half, (g + 1) * half)
                o_ref[rg, hs] = o_inter[rg] + _dot(a_ref[h, rg, 0:half * (g + 1)], v[0:half * (g + 1)])
            st_ref[h] = st * dec_ref[0:1, hs] + _dot_tn(v, kd_ref[:, hs])

    def normalise():
        for hs in head_slices:
            o = o_ref[:, hs]
            o_n = o * lax.rsqrt(jnp.mean(o * o, axis=-1, keepdims=True) + EPS) * (0.5 * gh_ref[:, hs])
            gate2 = jnp.tanh(0.5 * zo_ref[rows, hs].astype(F32)) + 1.0
            og_ref[og_rows, hs] = (o_n * gate2).astype(og_ref.dtype)

    return [operands, scores, outputs_and_state, normalise]


def _hgrn_body(q_ref, f_ref, i_ref, zo_ref, s0_ref, lb_ref, gh_ref, og_ref, sout_ref, *scratch,
               chunk, n_chunks, heads):
    ti = pl.program_id(2)
    st_ref = scratch[0]

    @pl.when(ti == 0)
    def _():
        for h in range(heads):
            st_ref[h] = s0_ref[0, h].T

    def chunk_step(ci, carry):
        rows = pl.ds(pl.multiple_of(ci * chunk, chunk), chunk)
        _hgrn_chunk(q_ref, f_ref, i_ref, zo_ref, lb_ref, gh_ref, og_ref, scratch, rows, rows, chunk, heads)
        return carry

    lax.fori_loop(0, n_chunks, chunk_step, 0)

    @pl.when(ti == pl.num_programs(2) - 1)
    def _():
        for h in range(heads):
            sout_ref[0, h] = st_ref[h].T


def _hgrn(z, s0, lb, g_head, *, row_base, batch, seq, d_model, name):
    n_heads = d_model // HEAD_DIM
    heads = min(16, n_heads)
    chunk = min(REC_CHUNK, seq)
    half = min(REC_HALF, chunk)
    assert seq % chunk == 0 and chunk % half == 0
    rt = max(_pick(seq, (512, 256, 128, 64, 32)), chunk)
    assert row_base % rt == 0
    tiles = seq // rt
    wblk = heads * HEAD_DIM
    cpb = d_model // wblk

    def zspec(section):
        return pl.BlockSpec((rt, wblk),
                            lambda b, hg, i: (row_base // rt + b * tiles + i, section * cpb + hg))

    return pl.pallas_call(
        functools.partial(_hgrn_body, chunk=chunk, n_chunks=rt // chunk, heads=heads),
        grid=(batch, n_heads // heads, tiles),
        in_specs=[zspec(0), zspec(1), zspec(2), zspec(3),
                  pl.BlockSpec((1, heads, HEAD_DIM, HEAD_DIM), lambda b, hg, i: (b, hg, 0, 0)),
                  pl.BlockSpec((1, wblk), lambda b, hg, i: (0, hg)),
                  pl.BlockSpec((1, wblk), lambda b, hg, i: (0, hg))],
        out_specs=[pl.BlockSpec((rt, wblk), lambda b, hg, i: (b * tiles + i, hg)),
                   pl.BlockSpec((1, heads, HEAD_DIM, HEAD_DIM), lambda b, hg, i: (b, hg, 0, 0))],
        out_shape=[jax.ShapeDtypeStruct((batch * seq, d_model), BF16),
                   jax.ShapeDtypeStruct((batch, n_heads, HEAD_DIM, HEAD_DIM), F32)],
        scratch_shapes=_hgrn_scratch(chunk, heads),
        compiler_params=pltpu.CompilerParams(dimension_semantics=("parallel", "parallel", "arbitrary"),
                                             vmem_limit_bytes=VMEM_LIMIT),
        name=name,
    )(z, z, z, z, s0, lb, g_head)


def _mix_body(zu_ref, zv_ref, zga_ref, zgb_ref, ogp_ref, ogs_ref, xp_ref, xs_ref, lng_ref, lnb_ref,
              wsp_ref, wss_ref, bsp_ref, bss_ref, wpa_hbm, wpb_hbm, wo_hbm,
              h1_ref, vnp_ref, vns_ref, sg_ref, wpa_ref, wpb_ref, wo_ref, stage_ref, wsem, ya_ref,
              *, n_prompt_tiles, dec_seq):
    @pl.when(pl.program_id(0) == 0)
    def _():
        _load_as_bf16(wpa_hbm, wpa_ref, stage_ref, wsem)
        _load_as_bf16(wpb_hbm, wpb_ref, stage_ref, wsem)
        _load_as_bf16(wo_hbm, wo_ref, stage_ref, wsem)

    is_p = pl.program_id(0) < n_prompt_tiles
    tm, width = zu_ref.shape
    gd = width // MLP_GROUPS
    og = jnp.where(is_p, ogp_ref[...].astype(F32), ogs_ref[...].astype(F32)).astype(BF16)
    n_chunks = tm // MLP_CHUNK
    d = h1_ref.shape[1]
    col = d // (2 * n_chunks)

    def ya_slice(j):
        cs = slice(j * col, (j + 1) * col)
        ya_ref[:, cs] = _dot(og, wpa_ref[:, cs])

    r = lax.broadcasted_iota(jnp.int32, (MLP_CHUNK, MLP_CHUNK), 0)
    c = lax.broadcasted_iota(jnp.int32, (MLP_CHUNK, MLP_CHUNK), 1)
    causal = r >= c
    same_stream = (r // dec_seq) == (c // dec_seq)
    w_mix, bias = [], []
    for g in range(MLP_GROUPS):
        w_p = jnp.where(causal, wsp_ref[g], 0.0)
        w_s = jnp.where(causal & same_stream, wss_ref[g], 0.0)
        w_mix.append(jnp.where(is_p, w_p, w_s).astype(BF16))
        bias.append(jnp.where(is_p, bsp_ref[g], bss_ref[g]))
    for cc in range(n_chunks):
        rows = slice(cc * MLP_CHUNK, (cc + 1) * MLP_CHUNK)
        ya_slice(2 * cc)
        u = _gelu(zu_ref[rows, :].astype(F32))
        gv = _gelu(zv_ref[rows, :].astype(F32))
        xc = gv - jnp.mean(gv, axis=-1, keepdims=True)
        vn = xc * lax.rsqrt(jnp.mean(xc * xc, axis=-1, keepdims=True) + EPS) * lng_ref[...] + lnb_ref[...]
        vnp_ref[rows, :] = vn
        vns_ref[rows, :] = vn
        ya_slice(2 * cc + 1)
        vnb = vn.astype(BF16)
        for g in range(MLP_GROUPS):
            cols = slice(g * gd, (g + 1) * gd)
            s = _dot(w_mix[g], vnb[:, cols]) + bias[g]
            sg_ref[rows, cols] = (u[:, cols] * s).astype(BF16)
    y_b = _dot(sg_ref[...], wpb_ref[...])
    m = _sigmoid(zga_ref[...].astype(F32)) * ya_ref[...] + _sigmoid(zgb_ref[...].astype(F32)) * y_b
    x = jnp.where(is_p, xp_ref[...], xs_ref[...])
    h1_ref[...] = x + _dot(m.astype(BF16), wo_ref[...])


def _mix(z, og_p, og_s, xp, xs, ln_g, ln_b, ws_p, ws_s, bs_p, bs_s, w_pa, w_pb, w_o, *, batch, dec_seq):
    (n_p, d), n_s = xp.shape, xs.shape[0]
    n = n_p + n_s
    width = w_pb.shape[0]
    tm = 256
    assert n_s % tm == 0 and (n_p // batch) % tm == 0 and tm % MLP_CHUNK == 0 and MLP_CHUNK % dec_seq == 0
    npt, nst = n_p // tm, n_s // tm
    tpb = npt // batch
    u_blk = 4 * d // width
    ga_blk = (4 * d + 2 * width) // d
    const2 = lambda i: (0, 0)
    const3 = lambda i: (0, 0, 0)
    p_map = lambda i: (jnp.minimum(i, npt - 1), 0)
    s_map = lambda i: (jnp.clip(i - npt, 0, nst - 1), 0)
    return pl.pallas_call(
        functools.partial(_mix_body, n_prompt_tiles=npt, dec_seq=dec_seq),
        grid=(n // tm,),
        in_specs=[pl.BlockSpec((tm, width), lambda i: (i, u_blk)),
                  pl.BlockSpec((tm, width), lambda i: (i, u_blk + 1)),
                  pl.BlockSpec((tm, d), lambda i: (i, ga_blk)),
                  pl.BlockSpec((tm, d), lambda i: (i, ga_blk + 1)),
                  pl.BlockSpec((tm, d), p_map),
                  pl.BlockSpec((tm, d), s_map),
                  pl.BlockSpec((tm, d), p_map),
                  pl.BlockSpec((tm, d), s_map),
                  pl.BlockSpec((1, width), const2),
                  pl.BlockSpec((1, width), const2),
                  pl.BlockSpec((MLP_GROUPS, MLP_CHUNK, MLP_CHUNK), const3),
                  pl.BlockSpec((MLP_GROUPS, MLP_CHUNK, MLP_CHUNK), const3),
                  pl.BlockSpec((MLP_GROUPS, MLP_CHUNK, 1), const3),
                  pl.BlockSpec((MLP_GROUPS, MLP_CHUNK, 1), const3),
                  pl.BlockSpec(memory_space=pl.ANY),
                  pl.BlockSpec(memory_space=pl.ANY),
                  pl.BlockSpec(memory_space=pl.ANY)],
        out_specs=[pl.BlockSpec((tm, d), lambda i: (i, 0)),
                   pl.BlockSpec((tm, width), lambda i: (jnp.where(i < npt, i // tpb, batch), 0)),
                   pl.BlockSpec((tm, width), lambda i: (jnp.where(i < npt, nst, i - npt), 0))],
        out_shape=[jax.ShapeDtypeStruct((n, d), F32),
                   jax.ShapeDtypeStruct(((batch + 1) * tm, width), F32),
                   jax.ShapeDtypeStruct(((nst + 1) * tm, width), F32)],
        scratch_shapes=[pltpu.VMEM((tm, width), BF16),
                        pltpu.VMEM((d, d), BF16), pltpu.VMEM((width, d), BF16), pltpu.VMEM((d, d), BF16),
                        pltpu.VMEM((WEIGHT_STAGE_ROWS, d), F32), pltpu.SemaphoreType.DMA((2,)),
                        pltpu.VMEM((tm, d), F32)],
        compiler_params=pltpu.CompilerParams(dimension_semantics=("arbitrary",),
                                             vmem_limit_bytes=VMEM_LIMIT),
        name="mix",
    )(z, z, z, z, og_p, og_s, xp, xs, ln_g, ln_b, ws_p, ws_s, bs_p, bs_s, w_pa, w_pb, w_o)


def _router_body(h_ref, g_ref, wr_ref, br_ref, c_ref, idx_ref, gate_ref, cnt_ref, carry_ref):
    @pl.when(pl.program_id(0) == 0)
    def _():
        carry_ref[...] = jnp.zeros_like(carry_ref)

    tm = h_ref.shape[0]
    c = _rms(h_ref[...], g_ref[...])
    wr = wr_ref[...]
    wr_hi = wr.astype(BF16)
    wr_lo = (wr - wr_hi.astype(F32)).astype(BF16)
    c_hi = c.astype(BF16)
    c_lo = (c - c_hi.astype(F32)).astype(BF16)
    _store_chunks(c_ref, _pack_bf16_pairs(c_hi.astype(F32)))
    both = _dot_nt(jnp.concatenate([wr_hi, wr_lo], axis=0), c_hi)
    lt = both[0:ROUTER_ROWS] + both[ROUTER_ROWS:2 * ROUTER_ROWS] + _dot_nt(wr_hi, c_lo) + br_ref[...]
    le = lt[0:N_EXPERTS]
    lg = lt[N_EXPERTS:N_EXPERTS + N_GROUPS]
    gmax = jnp.max(lg, axis=0, keepdims=True)
    p_sel = 1.0 / jnp.sum(jnp.exp(lg - gmax), axis=0, keepdims=True)
    best = lg[0:1]
    gi = jnp.zeros((1, tm), jnp.int32)
    for g in range(1, N_GROUPS):
        better = lg[g:g + 1] > best
        gi = jnp.where(better, g, gi)
        best = jnp.where(better, lg[g:g + 1], best)
    leg = jnp.zeros((E_PER_GROUP, tm), F32)
    for g in range(N_GROUPS):
        leg = jnp.where(gi == g, le[g * E_PER_GROUP:(g + 1) * E_PER_GROUP], leg)
    sub = lax.broadcasted_iota(jnp.int32, (E_PER_GROUP, tm), 0).astype(F32)
    v1 = jnp.max(leg, axis=0, keepdims=True)
    i1 = jnp.min(jnp.where(leg == v1, sub, float(E_PER_GROUP)), axis=0, keepdims=True)
    rest = jnp.where(sub == i1, -jnp.inf, leg)
    v2 = jnp.max(rest, axis=0, keepdims=True)
    i2 = jnp.min(jnp.where(rest == v2, sub, float(E_PER_GROUP)), axis=0, keepdims=True)
    e2 = jnp.exp(v2 - v1)
    den = 1.0 + e2
    gate0 = p_sel * (1.0 / den)
    gate1 = p_sel * (e2 / den)
    ex0 = gi * E_PER_GROUP + i1.astype(jnp.int32)
    ex1 = gi * E_PER_GROUP + i2.astype(jnp.int32)
    eid = lax.broadcasted_iota(jnp.int32, (N_EXPERTS, tm), 0)
    oh0 = eid == ex0
    oh1 = eid == ex1
    oh = jnp.where(oh0 | oh1, 1.0, 0.0)
    upper = jnp.where(lax.broadcasted_iota(jnp.int32, (tm, tm), 0) < lax.broadcasted_iota(jnp.int32, (tm, tm), 1),
                      1.0, 0.0).astype(BF16)
    before = _dot(oh.astype(BF16), upper) + carry_ref[:, 0:1]
    rank0 = jnp.sum(jnp.where(oh0, before, 0.0), axis=0, keepdims=True)
    rank1 = jnp.sum(jnp.where(oh1, before, 0.0), axis=0, keepdims=True)
    carry = carry_ref[...] + jnp.sum(oh, axis=1, keepdims=True)
    carry_ref[...] = carry
    cnt_ref[...] = carry.astype(jnp.int32)
    idx_ref[...] = jnp.zeros_like(idx_ref)
    idx_ref[0:1, :] = ex0
    idx_ref[1:2, :] = ex1
    idx_ref[2:3, :] = rank0.astype(jnp.int32)
    idx_ref[3:4, :] = rank1.astype(jnp.int32)
    gate_ref[...] = jnp.zeros_like(gate_ref)
    gate_ref[0:1, :] = gate0
    gate_ref[1:2, :] = gate1


def _router(h1, g_ffn, wr, br):
    n, d = h1.shape
    tm = _pick(n, (512, 256))
    return pl.pallas_call(
        _router_body,
        grid=(n // tm,),
        in_specs=[pl.BlockSpec((tm, d), lambda i: (i, 0)),
                  pl.BlockSpec((1, d), lambda i: (0, 0)),
                  pl.BlockSpec((ROUTER_ROWS, d), lambda i: (0, 0)),
                  pl.BlockSpec((ROUTER_ROWS, 1), lambda i: (0, 0))],
        out_specs=[pl.BlockSpec((d // (2 * LANES), tm, LANES), lambda i: (0, i, 0)),
                   pl.BlockSpec((SUBLANES, tm), lambda i: (0, i)),
                   pl.BlockSpec((SUBLANES, tm), lambda i: (0, i)),
                   pl.BlockSpec((N_EXPERTS, LANES), lambda i: (0, 0))],
        out_shape=[jax.ShapeDtypeStruct((d // (2 * LANES), n, LANES), jnp.uint32),
                   jax.ShapeDtypeStruct((SUBLANES, n), jnp.int32),
                   jax.ShapeDtypeStruct((SUBLANES, n), F32),
                   jax.ShapeDtypeStruct((N_EXPERTS, LANES), jnp.int32)],
        scratch_shapes=[pltpu.VMEM((N_EXPERTS, LANES), F32)],
        compiler_params=pltpu.CompilerParams(dimension_semantics=("arbitrary",),
                                             vmem_limit_bytes=VMEM_LIMIT),
        name="router",
    )(h1, g_ffn, wr, br)


def _dispatch_body(fill_lo_ref, fill_hi_ref, dest_ref, c_ref, xs_ref, zero_ref, sem, zsem):
    td = c_ref.shape[1]

    @pl.when(pl.program_id(0) == 0)
    def _():
        zero_ref[...] = jnp.zeros_like(zero_ref)

        def per_expert(e, carry):
            lo, hi = fill_lo_ref[e], fill_hi_ref[e]

            def start(r, c2):
                _row_copy(zero_ref, 0, xs_ref, r, zsem).start()
                return c2

            def wait(r, c2):
                _row_copy(zero_ref, 0, xs_ref, 0, zsem).wait()
                return c2

            lax.fori_loop(lo, hi, start, 0)
            lax.fori_loop(lo, hi, wait, 0)
            return carry

        lax.fori_loop(0, N_EXPERTS, per_expert, 0)

        def block_copy(j):
            return _rows_copy(zero_ref, xs_ref, j * MOE_BLOCK, MOE_BLOCK, zsem)

        first_unused = fill_hi_ref[N_EXPERTS - 1] // MOE_BLOCK
        n_blocks = xs_ref.shape[1] // MOE_BLOCK

        def start_block(j, carry):
            block_copy(j).start()
            return carry

        def wait_block(j, carry):
            block_copy(0).wait()
            return carry

        lax.fori_loop(first_unused, n_blocks, start_block, 0)
        lax.fori_loop(first_unused, n_blocks, wait_block, 0)

    for t in range(td):
        for k in range(TOP_K):
            _row_copy(c_ref, t, xs_ref, dest_ref[k, t], sem).start(priority=k % 2)
    for k in range(TOP_K):
        _rows_copy(c_ref, xs_ref, 0, td, sem).wait()


def _dispatch(fill_lo, fill_hi, dest, c, n_slots):
    nc, n, _ = c.shape
    td = _pick(n, (512, 256))
    grid_spec = pltpu.PrefetchScalarGridSpec(
        num_scalar_prefetch=2,
        grid=(n // td,),
        in_specs=[pl.BlockSpec((TOP_K, td), lambda i, lo, hi: (0, i), memory_space=pltpu.SMEM),
                  pl.BlockSpec((nc, td, LANES), lambda i, lo, hi: (0, i, 0))],
        out_specs=pl.BlockSpec(memory_space=pl.ANY),
        scratch_shapes=[pltpu.VMEM((nc, MOE_BLOCK, LANES), c.dtype),
                        pltpu.SemaphoreType.DMA(()), pltpu.SemaphoreType.DMA(())],
    )
    return pl.pallas_call(
        _dispatch_body,
        grid_spec=grid_spec,
        out_shape=jax.ShapeDtypeStruct((nc, n_slots, LANES), c.dtype),
        compiler_params=pltpu.CompilerParams(dimension_semantics=("arbitrary",),
                                             vmem_limit_bytes=VMEM_LIMIT),
        name="dispatch",
    )(fill_lo, fill_hi, dest, c)


def _expert_body(be_ref, na_ref, nxt_ref, xs_ref, wg_hbm, wu_hbm, wd_hbm, yb_ref,
                 wg_ref, wu_ref, wd_ref, sg_ref, su_ref, sd_ref, wsem):
    j = pl.program_id(0)
    active = j < na_ref[0]
    e = be_ref[j]
    first = (j == 0) | (be_ref[jnp.maximum(j - 1, 0)] != e)

    def weight_copies(expert):
        return (pltpu.make_async_copy(wg_hbm.at[expert], sg_ref, wsem.at[0]),
                pltpu.make_async_copy(wu_hbm.at[expert], su_ref, wsem.at[1]),
                pltpu.make_async_copy(wd_hbm.at[expert], sd_ref, wsem.at[2]))

    @pl.when(j == 0)
    def _():
        for cp in weight_copies(e):
            cp.start()

    @pl.when(active & first)
    def _():
        for cp, stage, dst in zip(weight_copies(e), (sg_ref, su_ref, sd_ref), (wg_ref, wu_ref, wd_ref)):
            cp.wait()

            def convert(r, carry, stage=stage, dst=dst):
                rows = pl.ds(pl.multiple_of(r * LANES, LANES), LANES)
                dst[rows, :] = stage[rows, :].astype(BF16)
                return carry

            lax.fori_loop(0, stage.shape[0] // LANES, convert, 0)

        @pl.when(nxt_ref[e] < N_EXPERTS)
        def _():
            for cp in weight_copies(nxt_ref[e]):
                cp.start()

    @pl.when(active)
    def _():
        x = _unpack_bf16_pairs(_load_chunks(xs_ref))
        a = _dot(x, wg_ref[...])
        b = _dot(x, wu_ref[...])
        hid = (a * _sigmoid(a)) * b
        _store_chunks(yb_ref, _dot(hid.astype(BF16), wd_ref[...]))

    @pl.when(jnp.logical_not(active))
    def _():
        yb_ref[...] = jnp.zeros_like(yb_ref)


def _experts(block_exp, n_active, next_exp, xs, w_gate, w_up, w_down):
    nc_in, n_slots, _ = xs.shape
    d, de = w_gate.shape[1:]
    nc = d // LANES
    nb = n_slots // MOE_BLOCK
    assert d % LANES == 0 and de % LANES == 0
    blk = lambda j, be, na, nxt: (0, jnp.minimum(j, na[0] - 1), 0)
    grid_spec = pltpu.PrefetchScalarGridSpec(
        num_scalar_prefetch=3,
        grid=(nb,),
        in_specs=[pl.BlockSpec((nc_in, MOE_BLOCK, LANES), blk),
                  pl.BlockSpec(memory_space=pl.ANY),
                  pl.BlockSpec(memory_space=pl.ANY),
                  pl.BlockSpec(memory_space=pl.ANY)],
        out_specs=pl.BlockSpec((nc, MOE_BLOCK, LANES), lambda j, be, na, nxt: (0, j, 0)),
        scratch_shapes=[pltpu.VMEM((d, de), BF16), pltpu.VMEM((d, de), BF16), pltpu.VMEM((de, d), BF16),
                        pltpu.VMEM((d, de), F32), pltpu.VMEM((d, de), F32), pltpu.VMEM((de, d), F32),
                        pltpu.SemaphoreType.DMA((3,))],
    )
    return pl.pallas_call(
        _expert_body,
        grid_spec=grid_spec,
        out_shape=jax.ShapeDtypeStruct((nc, n_slots, LANES), F32),
        compiler_params=pltpu.CompilerParams(dimension_semantics=("arbitrary",),
                                             vmem_limit_bytes=VMEM_LIMIT),
        name="experts",
    )(block_exp, n_active, next_exp, xs, w_gate, w_up, w_down)


def _combine_body(dcur_ref, dnxt_ref, gate_ref, h1_ref, p_ref, yb_ref, gple_ref, wpg_hbm, wpp_hbm, gfin_ref,
                  yp_ref, ys_ref, buf_ref, sem, wpg_ref, wpp_ref, stage_ref, wsem,
                  *, n_prompt_tiles, final_norm):
    i = pl.program_id(0)

    @pl.when(i == 0)
    def _():
        _load_as_bf16(wpg_hbm, wpg_ref, stage_ref, wsem)
        _load_as_bf16(wpp_hbm, wpp_ref, stage_ref, wsem)

    n_tiles = pl.num_programs(0)
    tc = h1_ref.shape[0]
    slot = lax.rem(i, 2)

    @pl.when(i == 0)
    def _():
        def body(t, carry):
            for k in range(TOP_K):
                _row_copy(yb_ref, dcur_ref[k, t], buf_ref.at[0, k], t, sem.at[0]).start()
            return carry

        lax.fori_loop(0, tc, body, 0)

    def wait_slot(s):
        for k in range(TOP_K):
            pltpu.make_async_copy(yb_ref.at[:, pl.ds(0, tc), :], buf_ref.at[s, k], sem.at[s]).wait()

    wait_slot(slot)
    gt = gate_ref[...].T
    h2 = h1_ref[...] + (gt[:, 0:1] * _load_chunks(buf_ref.at[slot, 0])
                        + gt[:, 1:2] * _load_chunks(buf_ref.at[slot, 1]))
    for t in range(tc):
        for k in range(TOP_K):
            _row_copy(yb_ref, dnxt_ref[k, t], buf_ref.at[1 - slot, k], t, sem.at[1 - slot]).start(priority=k % 2)
    a = _rms(h2, gple_ref[...]).astype(BF16)
    h3 = h2 + _sigmoid(_dot(a, wpg_ref[...])) * _dot(p_ref[...].astype(BF16), wpp_ref[...])
    if final_norm:
        h3 = _rms(h3, gfin_ref[...])

    @pl.when(i < n_prompt_tiles)
    def _():
        yp_ref[...] = h3

    @pl.when(i >= n_prompt_tiles)
    def _():
        ys_ref[...] = h3

    @pl.when(i == n_tiles - 1)
    def _():
        wait_slot(1 - slot)


def _combine(dest, gates, h1, p, yb, g_ple, w_pg, w_pp, g_final, *, n_prompt, final_norm):
    n, d = h1.shape
    ple = p.shape[1]
    tc = 256
    assert n_prompt % tc == 0 and (n - n_prompt) % tc == 0
    nt, npt = n // tc, n_prompt // tc
    nst = nt - npt
    const2 = lambda i: (0, 0)
    return pl.pallas_call(
        functools.partial(_combine_body, n_prompt_tiles=npt, final_norm=final_norm),
        grid=(nt,),
        in_specs=[pl.BlockSpec((TOP_K, tc), lambda i: (0, i), memory_space=pltpu.SMEM),
                  pl.BlockSpec((TOP_K, tc), lambda i: (0, jnp.minimum(i + 1, nt - 1)), memory_space=pltpu.SMEM),
                  pl.BlockSpec((SUBLANES, tc), lambda i: (0, i)),
                  pl.BlockSpec((tc, d), lambda i: (i, 0)),
                  pl.BlockSpec((tc, ple), lambda i: (i, 0)),
                  pl.BlockSpec(memory_space=pl.ANY),
                  pl.BlockSpec((1, d), const2),
                  pl.BlockSpec(memory_space=pl.ANY),
                  pl.BlockSpec(memory_space=pl.ANY),
                  pl.BlockSpec((1, d), const2)],
        out_specs=[pl.BlockSpec((tc, d), lambda i: (jnp.minimum(i, npt - 1), 0)),
                   pl.BlockSpec((tc, d), lambda i: (jnp.clip(i - npt, 0, nst - 1), 0))],
        out_shape=[jax.ShapeDtypeStruct((n_prompt, d), F32),
                   jax.ShapeDtypeStruct((n - n_prompt, d), F32)],
        scratch_shapes=[pltpu.VMEM((2, TOP_K, d // LANES, tc, LANES), F32), pltpu.SemaphoreType.DMA((2,)),
                        pltpu.VMEM((d, d), BF16), pltpu.VMEM((ple, d), BF16),
                        pltpu.VMEM((WEIGHT_STAGE_ROWS, d), F32), pltpu.SemaphoreType.DMA((2,))],
        compiler_params=pltpu.CompilerParams(dimension_semantics=("arbitrary",),
                                             vmem_limit_bytes=VMEM_LIMIT),
        name="combine",
    )(dest, dest, gates, h1, p, yb, g_ple, w_pg, w_pp, g_final)


def _layer(hp, hs, p, s0_sample, lb, batch, seq, dec_batch, dec_seq,
           g_mix, w_in, g_head, w_pa, ln_v_g, ln_v_b, w_s, b_s, w_pb, w_o,
           g_ffn, w_gr, b_gr, w_er, b_er, w_gate, w_up, w_down, g_ple, w_pg, w_pp, g_final, final_norm):
    n_prompt, d = hp.shape
    n = n_prompt + hs.shape[0]
    n_heads = d // HEAD_DIM
    row = lambda a: a.reshape(1, -1).astype(F32)

    z = _in_proj(hp, hs, row(g_mix), w_in)

    lb_row, gh_row = row(lb), row(g_head)
    s0_prompt = jnp.zeros((batch, n_heads, HEAD_DIM, HEAD_DIM), F32)
    og_p, st_p = _hgrn(z, s0_prompt, lb_row, gh_row, row_base=0, batch=batch, seq=seq,
                       d_model=d, name="hgrn_prompt")
    og_s, st_s = _hgrn(z, s0_sample.astype(F32), lb_row, gh_row, row_base=n_prompt, batch=dec_batch,
                       seq=dec_seq, d_model=d, name="hgrn_sample")

    start = PAST_LEN % MLP_CHUNK
    assert start + dec_seq <= MLP_CHUNK
    rep = MLP_CHUNK // dec_seq
    ws_s = jnp.tile(w_s[:, start:start + dec_seq, start:start + dec_seq], (1, rep, rep))
    bs_s = jnp.tile(b_s[:, start:start + dec_seq], (1, rep))
    h1, vn_p, vn_s = _mix(z, og_p, og_s, hp, hs, row(ln_v_g), row(ln_v_b), w_s, ws_s, b_s[..., None],
                          bs_s[..., None], w_pa, w_pb, w_o,
                          batch=batch, dec_seq=dec_seq)

    pad_rows = ROUTER_ROWS - N_EXPERTS - N_GROUPS
    wr = jnp.concatenate([w_er.T, w_gr.T, jnp.zeros((pad_rows, d), F32)], axis=0)
    br = jnp.concatenate([b_er, b_gr, jnp.zeros((pad_rows,), F32)]).reshape(ROUTER_ROWS, 1).astype(F32)
    c, idx, gates, cnt = _router(h1, row(g_ffn), wr, br)

    counts = cnt[:, 0]
    padded = (counts + MOE_BLOCK - 1) // MOE_BLOCK * MOE_BLOCK
    pad_end = jnp.cumsum(padded).astype(jnp.int32)
    pad_start = pad_end - padded
    n_blocks = -(-(n * TOP_K) // MOE_BLOCK) + N_EXPERTS
    block_first = jnp.arange(n_blocks, dtype=jnp.int32) * MOE_BLOCK
    block_exp = jnp.minimum(jnp.sum(pad_end[None, :] <= block_first[:, None], axis=1), N_EXPERTS - 1).astype(jnp.int32)
    n_active = jnp.maximum(pad_end[-1:] // MOE_BLOCK, 1)
    hit = idx[0:TOP_K, :, None] == jnp.arange(N_EXPERTS, dtype=jnp.int32)
    dest = jnp.sum(jnp.where(hit, pad_start, 0), axis=-1) + idx[TOP_K:2 * TOP_K]

    xs = _dispatch(pad_start + counts, pad_end, dest, c, n_blocks * MOE_BLOCK)
    used = jnp.where(counts > 0, jnp.arange(N_EXPERTS, dtype=jnp.int32), N_EXPERTS)
    next_exp = jnp.concatenate([lax.cummin(used, axis=0, reverse=True)[1:],
                                jnp.full((1,), N_EXPERTS, jnp.int32)]).astype(jnp.int32)
    yb = _experts(block_exp, n_active, next_exp, xs, w_gate, w_up, w_down)
    yp, ys = _combine(dest, gates, h1, p, yb, row(g_ple), w_pg, w_pp,
                      row(g_final), n_prompt=n_prompt, final_norm=final_norm)
    return yp, ys, st_p, st_s, vn_p, vn_s


def kernel(x_prompt, x_sample, p_prompt, p_sample, state_hgrn, g_mix, w_in, lb_logits, g_head, w_pa, ln_v_g, ln_v_b, w_s, b_s, w_pb, w_o, g_ffn, w_gr, b_gr, w_er, b_er, w_gate, w_up, w_down, g_ple, w_pg, w_pp, g_final):
    batch, seq, d = x_prompt.shape
    dec_batch, dec_seq, _ = x_sample.shape
    depth = g_mix.shape[0]
    n_prompt, n_sample = batch * seq, dec_batch * dec_seq
    width = w_pb.shape[1]
    lbs = jnp.cumsum(jax.nn.softmax(lb_logits.astype(F32), axis=0), axis=0)
    hp, hs = x_prompt.reshape(n_prompt, d), x_sample.reshape(n_sample, d)
    keep = min((seq - 1) % MLP_CHUNK + 1, seq)
    keep_s = min((PAST_LEN % MLP_CHUNK + dec_seq - 1) % MLP_CHUNK + 1, dec_seq)
    sp, ss, vp, vs = [], [], [], []
    for i in range(depth):
        p = jnp.concatenate([p_prompt[i].reshape(n_prompt, -1), p_sample[i].reshape(n_sample, -1)], axis=0)
        hp, hs, st_p, st_s, vn_p, vn_s = _layer(
            hp, hs, p, state_hgrn[i], lbs[i], batch, seq, dec_batch, dec_seq,
            g_mix[i], w_in[i], g_head[i], w_pa[i], ln_v_g[i], ln_v_b[i], w_s[i], b_s[i], w_pb[i], w_o[i],
            g_ffn[i], w_gr[i], b_gr[i], w_er[i], b_er[i], w_gate[i], w_up[i], w_down[i],
            g_ple[i], w_pg[i], w_pp[i], g_final, i == depth - 1)
        sp.append(st_p.astype(x_prompt.dtype))
        ss.append(st_s.astype(state_hgrn.dtype))
        vn_last = vn_p.reshape(batch + 1, -1, width)[:batch]
        vp.append(vn_last[:, vn_last.shape[1] - keep:])
        vs.append(vn_s[:n_sample].reshape(dec_batch, dec_seq, width)[:, dec_seq - keep_s:])
    y_prompt = hp.reshape(batch, seq, d)
    y_sample = hs.reshape(dec_batch, dec_seq, d)
    return (y_prompt, y_sample, jnp.stack(sp), jnp.stack(ss), jnp.stack(vp), jnp.stack(vs))
```

```python
import functools
import math

import jax
import jax.numpy as jnp
from jax import lax
from jax.experimental import pallas as pl
from jax.experimental.pallas import tpu as pltpu

F32 = jnp.float32
BF16 = jnp.bfloat16

EPS = 1e-6
HEAD_DIM = 128
REC_CHUNK = 64
REC_HALF = 32
MLP_CHUNK = 128
MLP_GROUPS = 4
N_GROUPS = 4
E_PER_GROUP = 8
N_EXPERTS = N_GROUPS * E_PER_GROUP
TOP_K = 2
MOE_BLOCK = 256
PAST_LEN = 2048
LANES = 128
SUBLANES = 8
ROUTER_ROWS = 48
VMEM_LIMIT = 56 * 1024 * 1024
LOG2_E = 1.4426950408889634
WEIGHT_STAGE_ROWS = 256


def _pick(n, candidates):
    for c in candidates:
        if n % c == 0:
            return c
    raise ValueError(f"no tile in {candidates} divides {n}")


def _dot(a, b):
    return jnp.dot(a, b, preferred_element_type=F32)


def _dot_nt(a, b, precision=None):
    return lax.dot_general(a, b, (((1,), (1,)), ((), ())), precision=precision, preferred_element_type=F32)


def _dot_tn(a, b):
    return lax.dot_general(a, b, (((0,), (0,)), ((), ())), preferred_element_type=F32)


def _rms(x, g):
    return x * lax.rsqrt(jnp.mean(x * x, axis=-1, keepdims=True) + EPS) * g


def _gelu(x):
    c = math.sqrt(2.0 / math.pi)
    return x * (0.5 * (1.0 + jnp.tanh(c * (x + 0.044715 * (x * x * x)))))


def _sigmoid(x):
    return 0.5 * jnp.tanh(0.5 * x) + 0.5


def _store_chunks(ref, x):
    for c in range(ref.shape[0]):
        ref[c] = x[:, c * LANES:(c + 1) * LANES].astype(ref.dtype)


def _load_chunks(ref):
    return jnp.concatenate([ref[c] for c in range(ref.shape[0])], axis=1)


HIGH_HALF = 0xFFFF0000


def _pack_bf16_pairs(x):
    half = x.shape[1] // 2
    lo = jnp.right_shift(pltpu.bitcast(x[:, :half], jnp.uint32), jnp.uint32(16))
    hi = jnp.bitwise_and(pltpu.bitcast(x[:, half:], jnp.uint32), jnp.uint32(HIGH_HALF))
    return jnp.bitwise_or(lo, hi)


def _unpack_bf16_pairs(p):
    lo = pltpu.bitcast(jnp.left_shift(p, jnp.uint32(16)), F32)
    hi = pltpu.bitcast(jnp.bitwise_and(p, jnp.uint32(HIGH_HALF)), F32)
    return jnp.concatenate([lo, hi], axis=1).astype(BF16)


def _row_copy(src_ref, src_row, dst_ref, dst_row, sem):
    return pltpu.make_async_copy(src_ref.at[:, pl.ds(src_row, 1), :], dst_ref.at[:, pl.ds(dst_row, 1), :], sem)


def _rows_copy(src_ref, dst_ref, dst_row, n_rows, sem):
    return pltpu.make_async_copy(src_ref, dst_ref.at[:, pl.ds(dst_row, n_rows), :], sem)


def _load_as_bf16(w_hbm, w_vmem, stage_ref, sem):
    half = stage_ref.shape[0] // 2
    rows, cols = min(half, w_hbm.shape[0]), min(stage_ref.shape[1], w_hbm.shape[1])
    assert w_hbm.shape[0] % rows == 0 and w_hbm.shape[1] % cols == 0
    chunks = [(r, c) for r in range(0, w_hbm.shape[0], rows) for c in range(0, w_hbm.shape[1], cols)]

    def slot(i):
        return stage_ref.at[pl.ds((i % 2) * half, rows), pl.ds(0, cols)]

    def copy(i):
        r, c = chunks[i]
        return pltpu.make_async_copy(w_hbm.at[pl.ds(r, rows), pl.ds(c, cols)], slot(i), sem.at[i % 2])

    copy(0).start()
    for i, (r, c) in enumerate(chunks):
        if i + 1 < len(chunks):
            copy(i + 1).start()
        copy(i).wait()
        w_vmem[pl.ds(r, rows), pl.ds(c, cols)] = slot(i)[...].astype(BF16)


def _in_proj_body(xp_ref, xs_ref, g_ref, w_ref, z_ref, xn_ref, *, n_prompt_tiles):
    first = pl.program_id(1) == 0
    is_p = pl.program_id(0) < n_prompt_tiles

    @pl.when(first & is_p)
    def _():
        xn_ref[...] = _rms(xp_ref[...], g_ref[...]).astype(BF16)

    @pl.when(first & jnp.logical_not(is_p))
    def _():
        xn_ref[...] = _rms(xs_ref[...], g_ref[...]).astype(BF16)

    z_ref[...] = _dot(xn_ref[...], w_ref[...].astype(BF16)).astype(z_ref.dtype)


def _in_proj(xp, xs, g, w):
    (n_p, d), n_s = xp.shape, xs.shape[0]
    width = w.shape[1]
    tm = _pick(math.gcd(n_p, n_s), (1024, 512, 256))
    tn = _pick(width, (1024, 512, 256, 128))
    npt, nst = n_p // tm, n_s // tm
    return pl.pallas_call(
        functools.partial(_in_proj_body, n_prompt_tiles=npt),
        grid=(npt + nst, width // tn),
        in_specs=[pl.BlockSpec((tm, d), lambda i, j: (jnp.minimum(i, npt - 1), 0)),
                  pl.BlockSpec((tm, d), lambda i, j: (jnp.clip(i - npt, 0, nst - 1), 0),
                               pipeline_mode=pl.Buffered(1)),
                  pl.BlockSpec((1, d), lambda i, j: (0, 0)),
                  pl.BlockSpec((d, tn), lambda i, j: (0, j))],
        out_specs=pl.BlockSpec((tm, tn), lambda i, j: (i, j)),
        out_shape=jax.ShapeDtypeStruct((n_p + n_s, width), BF16),
        scratch_shapes=[pltpu.VMEM((tm, d), BF16)],
        compiler_params=pltpu.CompilerParams(dimension_semantics=("parallel", "arbitrary"),
                                             vmem_limit_bytes=VMEM_LIMIT),
        name="in_proj",
    )(xp, xs, g, w)


def _cumsum_rows(x, tril3):
    hi = x.astype(BF16)
    r1 = x - hi.astype(F32)
    mid = r1.astype(BF16)
    lo = (r1 - mid.astype(F32)).astype(BF16)
    return _dot(tril3, jnp.concatenate([hi, mid, lo], axis=0))


def _hgrn_scratch(chunk, heads):
    half = min(REC_HALF, chunk)
    wblk = heads * HEAD_DIM
    per_chunk = ([pltpu.VMEM((chunk, wblk), BF16),
                  pltpu.VMEM((chunk, wblk), BF16),
                  pltpu.VMEM((chunk, wblk), BF16),
                  pltpu.VMEM((SUBLANES, wblk), F32),
                  pltpu.VMEM((heads, chunk, chunk), BF16),
                  pltpu.VMEM((chunk, wblk), F32)]
                 + [pltpu.VMEM((half * (g + 1), wblk), BF16) for g in range(chunk // half)])
    return [pltpu.VMEM((heads, HEAD_DIM, HEAD_DIM), F32)] + per_chunk + per_chunk


def _hgrn_passes(q_ref, f_ref, i_ref, zo_ref, lb_ref, gh_ref, og_ref, scratch, rows, og_rows, chunk, heads,
                 reset=None):
    st_ref, qs_ref, qe_ref, kd_ref, dec_ref, a_ref, o_ref = scratch[:7]
    key_refs = scratch[7:]
    half = min(REC_HALF, chunk)
    n_half = chunk // half
    tril = (lax.broadcasted_iota(jnp.int32, (chunk, chunk), 0)
            >= lax.broadcasted_iota(jnp.int32, (chunk, chunk), 1))
    tril = jnp.where(tril, 1.0, 0.0).astype(BF16)
    tril3 = jnp.concatenate([tril, tril, tril], axis=1)
    masks = []
    for g in range(n_half):
        r = lax.broadcasted_iota(jnp.int32, (half, half * (g + 1)), 0)
        c = lax.broadcasted_iota(jnp.int32, (half, half * (g + 1)), 1)
        masks.append(r + g * half >= c)
    head_slices = [slice(h * HEAD_DIM, (h + 1) * HEAD_DIM) for h in range(heads)]

    def operands():
        for hs in head_slices:
            q = q_ref[rows, hs].astype(F32)
            lb = lb_ref[:, hs]
            c1 = 0.5 * (1.0 - lb)
            f = (lb + c1) + c1 * jnp.tanh(0.5 * f_ref[rows, hs].astype(F32))
            kk = 1.0 - f
            b = _cumsum_rows(jnp.log(f), tril3) * LOG2_E
            b_last = b[chunk - 1:chunk, :]
            dec_ref[0:1, hs] = jnp.exp2(b_last)
            mids, ks = [], []
            for g in range(n_half):
                rg = slice(g * half, (g + 1) * half)
                mid = b[g * half + half // 2 - 1:g * half + half // 2, :]
                qs_g = q[rg] * jnp.exp2(b[rg] - mid)
                ks_g = kk[rg] * jnp.exp2(mid - b[rg])
                mids.append(mid)
                ks.append(ks_g)
                qs_ref[rg, hs] = qs_g.astype(BF16)
                qe_ref[rg, hs] = (qs_g * jnp.exp2(mid)).astype(BF16)
                kd_ref[rg, hs] = (ks_g * jnp.exp2(b_last - mid)).astype(BF16)
                for gp in range(g):
                    key_refs[g][gp * half:(gp + 1) * half, hs] = (
                        ks[gp] * jnp.exp2(mid - mids[gp])).astype(BF16)
                key_refs[g][rg, hs] = ks_g.astype(BF16)

    def scores():
        for h, hs in enumerate(head_slices):
            for g in range(n_half):
                rg = slice(g * half, (g + 1) * half)
                a = _dot_nt(qs_ref[rg, hs], key_refs[g][:, hs])
                a_ref[h, rg, 0:half * (g + 1)] = jnp.where(masks[g], a, 0.0).astype(BF16)

    def outputs_and_state():
        for h, hs in enumerate(head_slices):
            v = i_ref[rows, hs]
            st = st_ref[h]
            if reset is not None:
                st = jnp.where(reset, 0.0, st)
            o_inter = _dot_nt(qe_ref[:, hs], st.astype(BF16))
            for g in range(n_half):
                rg = slice(g * half, (g + 1) * half)
                o_ref[rg, hs] = o_inter[rg] + _dot(a_ref[h, rg, 0:half * (g + 1)], v[0:half * (g + 1)])
            st_ref[h] = st * dec_ref[0:1, hs] + _dot_tn(v, kd_ref[:, hs])

    def normalise():
        for hs in head_slices:
            o = o_ref[:, hs]
            o_n = o * lax.rsqrt(jnp.mean(o * o, axis=-1, keepdims=True) + EPS) * (0.5 * gh_ref[:, hs])
            gate2 = jnp.tanh(0.5 * zo_ref[rows, hs].astype(F32)) + 1.0
            og_ref[og_rows, hs] = (o_n * gate2).astype(og_ref.dtype)

    return [operands, scores, outputs_and_state, normalise]


def _hgrn_body(q_ref, f_ref, i_ref, zo_ref, s0_ref, lb_ref, gh_ref, og_ref, sout_ref, *scratch,
               chunk, n_chunks, heads):
    ti = pl.program_id(2)
    st_ref = scratch[0]

    @pl.when(ti == 0)
    def _():
        for h in range(heads):
            st_ref[h] = s0_ref[0, h].T

    per_step = 2 if n_chunks % 2 == 0 else 1
    n_set = (len(scratch) - 1) // 2
    sets = [(st_ref,) + tuple(scratch[1 + k * n_set:1 + (k + 1) * n_set]) for k in range(per_step)]

    def chunk_step(ci, carry):
        passes = []
        for k in range(per_step):
            rows = pl.ds(pl.multiple_of((ci * per_step + k) * chunk, chunk), chunk)
            passes.append(_hgrn_passes(q_ref, f_ref, i_ref, zo_ref, lb_ref, gh_ref, og_ref, sets[k], rows, rows,
                                       chunk, heads))
        for stage in zip(*passes):
            for run_pass in stage:
                run_pass()
        return carry

    lax.fori_loop(0, n_chunks // per_step, chunk_step, 0)

    @pl.when(ti == pl.num_programs(2) - 1)
    def _():
        for h in range(heads):
            sout_ref[0, h] = st_ref[h].T


def _hgrn(z, s0, lb, g_head, *, row_base, batch, seq, d_model, name):
    n_heads = d_model // HEAD_DIM
    heads = min(16, n_heads)
    chunk = min(REC_CHUNK, seq)
    half = min(REC_HALF, chunk)
    assert seq % chunk == 0 and chunk % half == 0
    rt = max(_pick(seq, (512, 256, 128, 64, 32)), chunk)
    assert row_base % rt == 0
    tiles = seq // rt
    wblk = heads * HEAD_DIM
    cpb = d_model // wblk

    def zspec(section):
        return pl.BlockSpec((rt, wblk),
                            lambda b, hg, i: (row_base // rt + b * tiles + i, section * cpb + hg))

    return pl.pallas_call(
        functools.partial(_hgrn_body, chunk=chunk, n_chunks=rt // chunk, heads=heads),
        grid=(batch, n_heads // heads, tiles),
        in_specs=[zspec(0), zspec(1), zspec(2), zspec(3),
                  pl.BlockSpec((1, heads, HEAD_DIM, HEAD_DIM), lambda b, hg, i: (b, hg, 0, 0)),
                  pl.BlockSpec((1, wblk), lambda b, hg, i: (0, hg)),
                  pl.BlockSpec((1, wblk), lambda b, hg, i: (0, hg))],
        out_specs=[pl.BlockSpec((rt, wblk), lambda b, hg, i: (b * tiles + i, hg)),
                   pl.BlockSpec((1, heads, HEAD_DIM, HEAD_DIM), lambda b, hg, i: (b, hg, 0, 0))],
        out_shape=[jax.ShapeDtypeStruct((batch * seq, d_model), BF16),
                   jax.ShapeDtypeStruct((batch, n_heads, HEAD_DIM, HEAD_DIM), F32)],
        scratch_shapes=_hgrn_scratch(chunk, heads),
        compiler_params=pltpu.CompilerParams(dimension_semantics=("parallel", "parallel", "arbitrary"),
                                             vmem_limit_bytes=VMEM_LIMIT),
        name=name,
    )(z, z, z, z, s0, lb, g_head)


def _mix_body(zu_ref, zv_ref, zga_ref, zgb_ref, ogp_ref, ogs_ref, xp_ref, xs_ref, lng_ref, lnb_ref,
              wsp_ref, wss_ref, bsp_ref, bss_ref, wpa_hbm, wpb_hbm, wo_hbm,
              h1_ref, vnp_ref, vns_ref, sg_ref, wpa_ref, wpb_ref, wo_ref, stage_ref, wsem, ya_ref,
              *, n_prompt_tiles, dec_seq):
    @pl.when(pl.program_id(0) == 0)
    def _():
        _load_as_bf16(wpa_hbm, wpa_ref, stage_ref, wsem)
        _load_as_bf16(wpb_hbm, wpb_ref, stage_ref, wsem)
        _load_as_bf16(wo_hbm, wo_ref, stage_ref, wsem)

    is_p = pl.program_id(0) < n_prompt_tiles
    tm, width = zu_ref.shape
    gd = width // MLP_GROUPS
    og = jnp.where(is_p, ogp_ref[...].astype(F32), ogs_ref[...].astype(F32)).astype(BF16)
    n_chunks = tm // MLP_CHUNK
    d = h1_ref.shape[1]
    col = d // (2 * n_chunks)

    def ya_slice(j):
        cs = slice(j * col, (j + 1) * col)
        ya_ref[:, cs] = _dot(og, wpa_ref[:, cs])

    r = lax.broadcasted_iota(jnp.int32, (MLP_CHUNK, MLP_CHUNK), 0)
    c = lax.broadcasted_iota(jnp.int32, (MLP_CHUNK, MLP_CHUNK), 1)
    causal = r >= c
    same_stream = (r // dec_seq) == (c // dec_seq)
    w_mix, bias = [], []
    for g in range(MLP_GROUPS):
        w_p = jnp.where(causal, wsp_ref[g], 0.0)
        w_s = jnp.where(causal & same_stream, wss_ref[g], 0.0)
        w_mix.append(jnp.where(is_p, w_p, w_s).astype(BF16))
        bias.append(jnp.where(is_p, bsp_ref[g], bss_ref[g]))
    for cc in range(n_chunks):
        rows = slice(cc * MLP_CHUNK, (cc + 1) * MLP_CHUNK)
        ya_slice(2 * cc)
        u = _gelu(zu_ref[rows, :].astype(F32))
        gv = _gelu(zv_ref[rows, :].astype(F32))
        xc = gv - jnp.mean(gv, axis=-1, keepdims=True)
        vn = xc * lax.rsqrt(jnp.mean(xc * xc, axis=-1, keepdims=True) + EPS) * lng_ref[...] + lnb_ref[...]
        vnp_ref[rows, :] = vn
        vns_ref[rows, :] = vn
        ya_slice(2 * cc + 1)
        vnb = vn.astype(BF16)
        for g in range(MLP_GROUPS):
            cols = slice(g * gd, (g + 1) * gd)
            s = _dot(w_mix[g], vnb[:, cols]) + bias[g]
            sg_ref[rows, cols] = (u[:, cols] * s).astype(BF16)
    y_b = _dot(sg_ref[...], wpb_ref[...])
    m = _sigmoid(zga_ref[...].astype(F32)) * ya_ref[...] + _sigmoid(zgb_ref[...].astype(F32)) * y_b
    x = jnp.where(is_p, xp_ref[...], xs_ref[...])
    h1_ref[...] = x + _dot(m.astype(BF16), wo_ref[...])


def _mix(z, og_p, og_s, xp, xs, ln_g, ln_b, ws_p, ws_s, bs_p, bs_s, w_pa, w_pb, w_o, *, batch, dec_seq):
    (n_p, d), n_s = xp.shape, xs.shape[0]
    n = n_p + n_s
    width = w_pb.shape[0]
    tm = 256
    assert n_s % tm == 0 and (n_p // batch) % tm == 0 and tm % MLP_CHUNK == 0 and MLP_CHUNK % dec_seq == 0
    npt, nst = n_p // tm, n_s // tm
    tpb = npt // batch
    u_blk = 4 * d // width
    ga_blk = (4 * d + 2 * width) // d
    const2 = lambda i: (0, 0)
    const3 = lambda i: (0, 0, 0)
    p_map = lambda i: (jnp.minimum(i, npt - 1), 0)
    s_map = lambda i: (jnp.clip(i - npt, 0, nst - 1), 0)
    return pl.pallas_call(
        functools.partial(_mix_body, n_prompt_tiles=npt, dec_seq=dec_seq),
        grid=(n // tm,),
        in_specs=[pl.BlockSpec((tm, width), lambda i: (i, u_blk)),
                  pl.BlockSpec((tm, width), lambda i: (i, u_blk + 1)),
                  pl.BlockSpec((tm, d), lambda i: (i, ga_blk)),
                  pl.BlockSpec((tm, d), lambda i: (i, ga_blk + 1)),
                  pl.BlockSpec((tm, d), p_map),
                  pl.BlockSpec((tm, d), s_map),
                  pl.BlockSpec((tm, d), p_map),
                  pl.BlockSpec((tm, d), s_map),
                  pl.BlockSpec((1, width), const2),
                  pl.BlockSpec((1, width), const2),
                  pl.BlockSpec((MLP_GROUPS, MLP_CHUNK, MLP_CHUNK), const3),
                  pl.BlockSpec((MLP_GROUPS, MLP_CHUNK, MLP_CHUNK), const3),
                  pl.BlockSpec((MLP_GROUPS, MLP_CHUNK, 1), const3),
                  pl.BlockSpec((MLP_GROUPS, MLP_CHUNK, 1), const3),
                  pl.BlockSpec(memory_space=pl.ANY),
                  pl.BlockSpec(memory_space=pl.ANY),
                  pl.BlockSpec(memory_space=pl.ANY)],
        out_specs=[pl.BlockSpec((tm, d), lambda i: (i, 0)),
                   pl.BlockSpec((tm, width), lambda i: (jnp.where(i < npt, i // tpb, batch), 0)),
                   pl.BlockSpec((tm, width), lambda i: (jnp.where(i < npt, nst, i - npt), 0))],
        out_shape=[jax.ShapeDtypeStruct((n, d), F32),
                   jax.ShapeDtypeStruct(((batch + 1) * tm, width), F32),
                   jax.ShapeDtypeStruct(((nst + 1) * tm, width), F32)],
        scratch_shapes=[pltpu.VMEM((tm, width), BF16),
                        pltpu.VMEM((d, d), BF16), pltpu.VMEM((width, d), BF16), pltpu.VMEM((d, d), BF16),
                        pltpu.VMEM((WEIGHT_STAGE_ROWS, d), F32), pltpu.SemaphoreType.DMA((2,)),
                        pltpu.VMEM((tm, d), F32)],
        compiler_params=pltpu.CompilerParams(dimension_semantics=("arbitrary",),
                                             vmem_limit_bytes=VMEM_LIMIT),
        name="mix",
    )(z, z, z, z, og_p, og_s, xp, xs, ln_g, ln_b, ws_p, ws_s, bs_p, bs_s, w_pa, w_pb, w_o)


def _router_body(h_ref, g_ref, wr_ref, br_ref, c_ref, idx_ref, gate_ref, cnt_ref, carry_ref):
    @pl.when(pl.program_id(0) == 0)
    def _():
        carry_ref[...] = jnp.zeros_like(carry_ref)

    tm = h_ref.shape[0]
    c = _rms(h_ref[...], g_ref[...])
    wr = wr_ref[...]
    wr_hi = wr.astype(BF16)
    wr_lo = (wr - wr_hi.astype(F32)).astype(BF16)
    c_hi = c.astype(BF16)
    c_lo = (c - c_hi.astype(F32)).astype(BF16)
    _store_chunks(c_ref, _pack_bf16_pairs(c_hi.astype(F32)))
    both = _dot_nt(jnp.concatenate([wr_hi, wr_lo], axis=0), c_hi)
    lt = both[0:ROUTER_ROWS] + both[ROUTER_ROWS:2 * ROUTER_ROWS] + _dot_nt(wr_hi, c_lo) + br_ref[...]
    le = lt[0:N_EXPERTS]
    lg = lt[N_EXPERTS:N_EXPERTS + N_GROUPS]
    gmax = jnp.max(lg, axis=0, keepdims=True)
    p_sel = 1.0 / jnp.sum(jnp.exp(lg - gmax), axis=0, keepdims=True)
    best = lg[0:1]
    gi = jnp.zeros((1, tm), jnp.int32)
    for g in range(1, N_GROUPS):
        better = lg[g:g + 1] > best
        gi = jnp.where(better, g, gi)
        best = jnp.where(better, lg[g:g + 1], best)
    leg = jnp.zeros((E_PER_GROUP, tm), F32)
    for g in range(N_GROUPS):
        leg = jnp.where(gi == g, le[g * E_PER_GROUP:(g + 1) * E_PER_GROUP], leg)
    sub = lax.broadcasted_iota(jnp.int32, (E_PER_GROUP, tm), 0).astype(F32)
    v1 = jnp.max(leg, axis=0, keepdims=True)
    i1 = jnp.min(jnp.where(leg == v1, sub, float(E_PER_GROUP)), axis=0, keepdims=True)
    rest = jnp.where(sub == i1, -jnp.inf, leg)
    v2 = jnp.max(rest, axis=0, keepdims=True)
    i2 = jnp.min(jnp.where(rest == v2, sub, float(E_PER_GROUP)), axis=0, keepdims=True)
    e2 = jnp.exp(v2 - v1)
    den = 1.0 + e2
    gate0 = p_sel * (1.0 / den)
    gate1 = p_sel * (e2 / den)
    ex0 = gi * E_PER_GROUP + i1.astype(jnp.int32)
    ex1 = gi * E_PER_GROUP + i2.astype(jnp.int32)
    eid = lax.broadcasted_iota(jnp.int32, (N_EXPERTS, tm), 0)
    oh0 = eid == ex0
    oh1 = eid == ex1
    oh = jnp.where(oh0 | oh1, 1.0, 0.0)
    upper = jnp.where(lax.broadcasted_iota(jnp.int32, (tm, tm), 0) < lax.broadcasted_iota(jnp.int32, (tm, tm), 1),
                      1.0, 0.0).astype(BF16)
    before = _dot(oh.astype(BF16), upper) + carry_ref[:, 0:1]
    rank0 = jnp.sum(jnp.where(oh0, before, 0.0), axis=0, keepdims=True)
    rank1 = jnp.sum(jnp.where(oh1, before, 0.0), axis=0, keepdims=True)
    carry = carry_ref[...] + jnp.sum(oh, axis=1, keepdims=True)
    carry_ref[...] = carry
    cnt_ref[...] = carry.astype(jnp.int32)
    idx_ref[...] = jnp.zeros_like(idx_ref)
    idx_ref[0:1, :] = ex0
    idx_ref[1:2, :] = ex1
    idx_ref[2:3, :] = rank0.astype(jnp.int32)
    idx_ref[3:4, :] = rank1.astype(jnp.int32)
    gate_ref[...] = jnp.zeros_like(gate_ref)
    gate_ref[0:1, :] = gate0
    gate_ref[1:2, :] = gate1


def _router(h1, g_ffn, wr, br):
    n, d = h1.shape
    tm = _pick(n, (512, 256))
    return pl.pallas_call(
        _router_body,
        grid=(n // tm,),
        in_specs=[pl.BlockSpec((tm, d), lambda i: (i, 0)),
                  pl.BlockSpec((1, d), lambda i: (0, 0)),
                  pl.BlockSpec((ROUTER_ROWS, d), lambda i: (0, 0)),
                  pl.BlockSpec((ROUTER_ROWS, 1), lambda i: (0, 0))],
        out_specs=[pl.BlockSpec((d // (2 * LANES), tm, LANES), lambda i: (0, i, 0)),
                   pl.BlockSpec((SUBLANES, tm), lambda i: (0, i)),
                   pl.BlockSpec((SUBLANES, tm), lambda i: (0, i)),
                   pl.BlockSpec((N_EXPERTS, LANES), lambda i: (0, 0))],
        out_shape=[jax.ShapeDtypeStruct((d // (2 * LANES), n, LANES), jnp.uint32),
                   jax.ShapeDtypeStruct((SUBLANES, n), jnp.int32),
                   jax.ShapeDtypeStruct((SUBLANES, n), F32),
                   jax.ShapeDtypeStruct((N_EXPERTS, LANES), jnp.int32)],
        scratch_shapes=[pltpu.VMEM((N_EXPERTS, LANES), F32)],
        compiler_params=pltpu.CompilerParams(dimension_semantics=("arbitrary",),
                                             vmem_limit_bytes=VMEM_LIMIT),
        name="router",
    )(h1, g_ffn, wr, br)


def _dispatch_body(fill_lo_ref, fill_hi_ref, dest_ref, c_ref, xs_ref, zero_ref, sem, zsem):
    td = c_ref.shape[1]

    @pl.when(pl.program_id(0) == 0)
    def _():
        zero_ref[...] = jnp.zeros_like(zero_ref)

        def per_expert(e, carry):
            lo, hi = fill_lo_ref[e], fill_hi_ref[e]

            def start(r, c2):
                _row_copy(zero_ref, 0, xs_ref, r, zsem).start()
                return c2

            def wait(r, c2):
                _row_copy(zero_ref, 0, xs_ref, 0, zsem).wait()
                return c2

            lax.fori_loop(lo, hi, start, 0)
            lax.fori_loop(lo, hi, wait, 0)
            return carry

        lax.fori_loop(0, N_EXPERTS, per_expert, 0)

        def block_copy(j):
            return _rows_copy(zero_ref, xs_ref, j * MOE_BLOCK, MOE_BLOCK, zsem)

        first_unused = fill_hi_ref[N_EXPERTS - 1] // MOE_BLOCK
        n_blocks = xs_ref.shape[1] // MOE_BLOCK

        def start_block(j, carry):
            block_copy(j).start()
            return carry

        def wait_block(j, carry):
            block_copy(0).wait()
            return carry

        lax.fori_loop(first_unused, n_blocks, start_block, 0)
        lax.fori_loop(first_unused, n_blocks, wait_block, 0)

    for t in range(td):
        for k in range(TOP_K):
            _row_copy(c_ref, t, xs_ref, dest_ref[k, t], sem).start(priority=k % 2)
    for k in range(TOP_K):
        _rows_copy(c_ref, xs_ref, 0, td, sem).wait()


def _dispatch(fill_lo, fill_hi, dest, c, n_slots):
    nc, n, _ = c.shape
    td = _pick(n, (512, 256))
    grid_spec = pltpu.PrefetchScalarGridSpec(
        num_scalar_prefetch=2,
        grid=(n // td,),
        in_specs=[pl.BlockSpec((TOP_K, td), lambda i, lo, hi: (0, i), memory_space=pltpu.SMEM),
                  pl.BlockSpec((nc, td, LANES), lambda i, lo, hi: (0, i, 0))],
        out_specs=pl.BlockSpec(memory_space=pl.ANY),
        scratch_shapes=[pltpu.VMEM((nc, MOE_BLOCK, LANES), c.dtype),
                        pltpu.SemaphoreType.DMA(()), pltpu.SemaphoreType.DMA(())],
    )
    return pl.pallas_call(
        _dispatch_body,
        grid_spec=grid_spec,
        out_shape=jax.ShapeDtypeStruct((nc, n_slots, LANES), c.dtype),
        compiler_params=pltpu.CompilerParams(dimension_semantics=("arbitrary",),
                                             vmem_limit_bytes=VMEM_LIMIT),
        name="dispatch",
    )(fill_lo, fill_hi, dest, c)


def _expert_body(be_ref, na_ref, nxt_ref, xs_ref, wg_hbm, wu_hbm, wd_hbm, yb_ref,
                 wg_ref, wu_ref, wd_ref, sg_ref, su_ref, sd_ref, wsem):
    j = pl.program_id(0)
    active = j < na_ref[0]
    e = be_ref[j]
    first = (j == 0) | (be_ref[jnp.maximum(j - 1, 0)] != e)

    def weight_copies(expert):
        return (pltpu.make_async_copy(wg_hbm.at[expert], sg_ref, wsem.at[0]),
                pltpu.make_async_copy(wu_hbm.at[expert], su_ref, wsem.at[1]),
                pltpu.make_async_copy(wd_hbm.at[expert], sd_ref, wsem.at[2]))

    @pl.when(j == 0)
    def _():
        for cp in weight_copies(e):
            cp.start()

    @pl.when(active & first)
    def _():
        for cp, stage, dst in zip(weight_copies(e), (sg_ref, su_ref, sd_ref), (wg_ref, wu_ref, wd_ref)):
            cp.wait()

            def convert(r, carry, stage=stage, dst=dst):
                rows = pl.ds(pl.multiple_of(r * LANES, LANES), LANES)
                dst[rows, :] = stage[rows, :].astype(BF16)
                return carry

            lax.fori_loop(0, stage.shape[0] // LANES, convert, 0)

        @pl.when(nxt_ref[e] < N_EXPERTS)
        def _():
            for cp in weight_copies(nxt_ref[e]):
                cp.start()

    @pl.when(active)
    def _():
        x = _unpack_bf16_pairs(_load_chunks(xs_ref))
        a = _dot(x, wg_ref[...])
        b = _dot(x, wu_ref[...])
        hid = (a * _sigmoid(a)) * b
        _store_chunks(yb_ref, _dot(hid.astype(BF16), wd_ref[...]))

    @pl.when(jnp.logical_not(active))
    def _():
        yb_ref[...] = jnp.zeros_like(yb_ref)


def _experts(block_exp, n_active, next_exp, xs, w_gate, w_up, w_down):
    nc_in, n_slots, _ = xs.shape
    d, de = w_gate.shape[1:]
    nc = d // LANES
    nb = n_slots // MOE_BLOCK
    assert d % LANES == 0 and de % LANES == 0
    blk = lambda j, be, na, nxt: (0, jnp.minimum(j, na[0] - 1), 0)
    grid_spec = pltpu.PrefetchScalarGridSpec(
        num_scalar_prefetch=3,
        grid=(nb,),
        in_specs=[pl.BlockSpec((nc_in, MOE_BLOCK, LANES), blk),
                  pl.BlockSpec(memory_space=pl.ANY),
                  pl.BlockSpec(memory_space=pl.ANY),
                  pl.BlockSpec(memory_space=pl.ANY)],
        out_specs=pl.BlockSpec((nc, MOE_BLOCK, LANES), lambda j, be, na, nxt: (0, j, 0)),
        scratch_shapes=[pltpu.VMEM((d, de), BF16), pltpu.VMEM((d, de), BF16), pltpu.VMEM((de, d), BF16),
                        pltpu.VMEM((d, de), F32), pltpu.VMEM((d, de), F32), pltpu.VMEM((de, d), F32),
                        pltpu.SemaphoreType.DMA((3,))],
    )
    return pl.pallas_call(
        _expert_body,
        grid_spec=grid_spec,
        out_shape=jax.ShapeDtypeStruct((nc, n_slots, LANES), F32),
        compiler_params=pltpu.CompilerParams(dimension_semantics=("arbitrary",),
                                             vmem_limit_bytes=VMEM_LIMIT),
        name="experts",
    )(block_exp, n_active, next_exp, xs, w_gate, w_up, w_down)


def _combine_body(dcur_ref, dnxt_ref, gate_ref, h1_ref, p_ref, yb_ref, gple_ref, wpg_hbm, wpp_hbm, gfin_ref,
                  yp_ref, ys_ref, buf_ref, sem, wpg_ref, wpp_ref, stage_ref, wsem,
                  *, n_prompt_tiles, final_norm):
    i = pl.program_id(0)

    @pl.when(i == 0)
    def _():
        _load_as_bf16(wpg_hbm, wpg_ref, stage_ref, wsem)
        _load_as_bf16(wpp_hbm, wpp_ref, stage_ref, wsem)

    n_tiles = pl.num_programs(0)
    tc = h1_ref.shape[0]
    slot = lax.rem(i, 2)

    @pl.when(i == 0)
    def _():
        def body(t, carry):
            for k in range(TOP_K):
                _row_copy(yb_ref, dcur_ref[k, t], buf_ref.at[0, k], t, sem.at[0]).start()
            return carry

        lax.fori_loop(0, tc, body, 0)

    def wait_slot(s):
        for k in range(TOP_K):
            pltpu.make_async_copy(yb_ref.at[:, pl.ds(0, tc), :], buf_ref.at[s, k], sem.at[s]).wait()

    wait_slot(slot)
    gt = gate_ref[...].T
    h2 = h1_ref[...] + (gt[:, 0:1] * _load_chunks(buf_ref.at[slot, 0])
                        + gt[:, 1:2] * _load_chunks(buf_ref.at[slot, 1]))
    for t in range(tc):
        for k in range(TOP_K):
            _row_copy(yb_ref, dnxt_ref[k, t], buf_ref.at[1 - slot, k], t, sem.at[1 - slot]).start(priority=k % 2)
    a = _rms(h2, gple_ref[...]).astype(BF16)
    h3 = h2 + _sigmoid(_dot(a, wpg_ref[...])) * _dot(p_ref[...].astype(BF16), wpp_ref[...])
    if final_norm:
        h3 = _rms(h3, gfin_ref[...])

    @pl.when(i < n_prompt_tiles)
    def _():
        yp_ref[...] = h3

    @pl.when(i >= n_prompt_tiles)
    def _():
        ys_ref[...] = h3

    @pl.when(i == n_tiles - 1)
    def _():
        wait_slot(1 - slot)


def _combine(dest, gates, h1, p, yb, g_ple, w_pg, w_pp, g_final, *, n_prompt, final_norm):
    n, d = h1.shape
    ple = p.shape[1]
    tc = 256
    assert n_prompt % tc == 0 and (n - n_prompt) % tc == 0
    nt, npt = n // tc, n_prompt // tc
    nst = nt - npt
    const2 = lambda i: (0, 0)
    return pl.pallas_call(
        functools.partial(_combine_body, n_prompt_tiles=npt, final_norm=final_norm),
        grid=(nt,),
        in_specs=[pl.BlockSpec((TOP_K, tc), lambda i: (0, i), memory_space=pltpu.SMEM),
                  pl.BlockSpec((TOP_K, tc), lambda i: (0, jnp.minimum(i + 1, nt - 1)), memory_space=pltpu.SMEM),
                  pl.BlockSpec((SUBLANES, tc), lambda i: (0, i)),
                  pl.BlockSpec((tc, d), lambda i: (i, 0)),
                  pl.BlockSpec((tc, ple), lambda i: (i, 0)),
                  pl.BlockSpec(memory_space=pl.ANY),
                  pl.BlockSpec((1, d), const2),
                  pl.BlockSpec(memory_space=pl.ANY),
                  pl.BlockSpec(memory_space=pl.ANY),
                  pl.BlockSpec((1, d), const2)],
        out_specs=[pl.BlockSpec((tc, d), lambda i: (jnp.minimum(i, npt - 1), 0)),
                   pl.BlockSpec((tc, d), lambda i: (jnp.clip(i - npt, 0, nst - 1), 0))],
        out_shape=[jax.ShapeDtypeStruct((n_prompt, d), F32),
                   jax.ShapeDtypeStruct((n - n_prompt, d), F32)],
        scratch_shapes=[pltpu.VMEM((2, TOP_K, d // LANES, tc, LANES), F32), pltpu.SemaphoreType.DMA((2,)),
                        pltpu.VMEM((d, d), BF16), pltpu.VMEM((ple, d), BF16),
                        pltpu.VMEM((WEIGHT_STAGE_ROWS, d), F32), pltpu.SemaphoreType.DMA((2,))],
        compiler_params=pltpu.CompilerParams(dimension_semantics=("arbitrary",),
                                             vmem_limit_bytes=VMEM_LIMIT),
        name="combine",
    )(dest, dest, gates, h1, p, yb, g_ple, w_pg, w_pp, g_final)


def _layer(hp, hs, p, s0_sample, lb, batch, seq, dec_batch, dec_seq,
           g_mix, w_in, g_head, w_pa, ln_v_g, ln_v_b, w_s, b_s, w_pb, w_o,
           g_ffn, w_gr, b_gr, w_er, b_er, w_gate, w_up, w_down, g_ple, w_pg, w_pp, g_final, final_norm):
    n_prompt, d = hp.shape
    n = n_prompt + hs.shape[0]
    n_heads = d // HEAD_DIM
    row = lambda a: a.reshape(1, -1).astype(F32)

    z = _in_proj(hp, hs, row(g_mix), w_in)

    lb_row, gh_row = row(lb), row(g_head)
    s0_prompt = jnp.zeros((batch, n_heads, HEAD_DIM, HEAD_DIM), F32)
    og_p, st_p = _hgrn(z, s0_prompt, lb_row, gh_row, row_base=0, batch=batch, seq=seq,
                       d_model=d, name="hgrn_prompt")
    og_s, st_s = _hgrn(z, s0_sample.astype(F32), lb_row, gh_row, row_base=n_prompt, batch=dec_batch,
                       seq=dec_seq, d_model=d, name="hgrn_sample")

    start = PAST_LEN % MLP_CHUNK
    assert start + dec_seq <= MLP_CHUNK
    rep = MLP_CHUNK // dec_seq
    ws_s = jnp.tile(w_s[:, start:start + dec_seq, start:start + dec_seq], (1, rep, rep))
    bs_s = jnp.tile(b_s[:, start:start + dec_seq], (1, rep))
    h1, vn_p, vn_s = _mix(z, og_p, og_s, hp, hs, row(ln_v_g), row(ln_v_b), w_s, ws_s, b_s[..., None],
                          bs_s[..., None], w_pa, w_pb, w_o,
                          batch=batch, dec_seq=dec_seq)

    pad_rows = ROUTER_ROWS - N_EXPERTS - N_GROUPS
    wr = jnp.concatenate([w_er.T, w_gr.T, jnp.zeros((pad_rows, d), F32)], axis=0)
    br = jnp.concatenate([b_er, b_gr, jnp.zeros((pad_rows,), F32)]).reshape(ROUTER_ROWS, 1).astype(F32)
    c, idx, gates, cnt = _router(h1, row(g_ffn), wr, br)

    counts = cnt[:, 0]
    padded = (counts + MOE_BLOCK - 1) // MOE_BLOCK * MOE_BLOCK
    pad_end = jnp.cumsum(padded).astype(jnp.int32)
    pad_start = pad_end - padded
    n_blocks = -(-(n * TOP_K) // MOE_BLOCK) + N_EXPERTS
    block_first = jnp.arange(n_blocks, dtype=jnp.int32) * MOE_BLOCK
    block_exp = jnp.minimum(jnp.sum(pad_end[None, :] <= block_first[:, None], axis=1), N_EXPERTS - 1).astype(jnp.int32)
    n_active = jnp.maximum(pad_end[-1:] // MOE_BLOCK, 1)
    hit = idx[0:TOP_K, :, None] == jnp.arange(N_EXPERTS, dtype=jnp.int32)
    dest = jnp.sum(jnp.where(hit, pad_start, 0), axis=-1) + idx[TOP_K:2 * TOP_K]

    xs = _dispatch(pad_start + counts, pad_end, dest, c, n_blocks * MOE_BLOCK)
    used = jnp.where(counts > 0, jnp.arange(N_EXPERTS, dtype=jnp.int32), N_EXPERTS)
    next_exp = jnp.concatenate([lax.cummin(used, axis=0, reverse=True)[1:],
                                jnp.full((1,), N_EXPERTS, jnp.int32)]).astype(jnp.int32)
    yb = _experts(block_exp, n_active, next_exp, xs, w_gate, w_up, w_down)
    yp, ys = _combine(dest, gates, h1, p, yb, row(g_ple), w_pg, w_pp,
                      row(g_final), n_prompt=n_prompt, final_norm=final_norm)
    return yp, ys, st_p, st_s, vn_p, vn_s


def kernel(x_prompt, x_sample, p_prompt, p_sample, state_hgrn, g_mix, w_in, lb_logits, g_head, w_pa, ln_v_g, ln_v_b, w_s, b_s, w_pb, w_o, g_ffn, w_gr, b_gr, w_er, b_er, w_gate, w_up, w_down, g_ple, w_pg, w_pp, g_final):
    batch, seq, d = x_prompt.shape
    dec_batch, dec_seq, _ = x_sample.shape
    depth = g_mix.shape[0]
    n_prompt, n_sample = batch * seq, dec_batch * dec_seq
    width = w_pb.shape[1]
    lbs = jnp.cumsum(jax.nn.softmax(lb_logits.astype(F32), axis=0), axis=0)
    hp, hs = x_prompt.reshape(n_prompt, d), x_sample.reshape(n_sample, d)
    keep = min((seq - 1) % MLP_CHUNK + 1, seq)
    keep_s = min((PAST_LEN % MLP_CHUNK + dec_seq - 1) % MLP_CHUNK + 1, dec_seq)
    sp, ss, vp, vs = [], [], [], []
    for i in range(depth):
        p = jnp.concatenate([p_prompt[i].reshape(n_prompt, -1), p_sample[i].reshape(n_sample, -1)], axis=0)
        hp, hs, st_p, st_s, vn_p, vn_s = _layer(
            hp, hs, p, state_hgrn[i], lbs[i], batch, seq, dec_batch, dec_seq,
            g_mix[i], w_in[i], g_head[i], w_pa[i], ln_v_g[i], ln_v_b[i], w_s[i], b_s[i], w_pb[i], w_o[i],
            g_ffn[i], w_gr[i], b_gr[i], w_er[i], b_er[i], w_gate[i], w_up[i], w_down[i],
            g_ple[i], w_pg[i], w_pp[i], g_final, i == depth - 1)
        sp.append(st_p.astype(x_prompt.dtype))
        ss.append(st_s.astype(state_hgrn.dtype))
        vn_last = vn_p.reshape(batch + 1, -1, width)[:batch]
        vp.append(vn_last[:, vn_last.shape[1] - keep:])
        vs.append(vn_s[:n_sample].reshape(dec_batch, dec_seq, width)[:, dec_seq - keep_s:])
    y_prompt = hp.reshape(batch, seq, d)
    y_sample = hs.reshape(dec_batch, dec_seq, d)
    return (y_prompt, y_sample, jnp.stack(sp), jnp.stack(ss), jnp.stack(vp), jnp.stack(vs))
```

```python
import functools
import math

import jax
import jax.numpy as jnp
from jax import lax
from jax.experimental import pallas as pl
from jax.experimental.pallas import tpu as pltpu

F32 = jnp.float32
BF16 = jnp.bfloat16

EPS = 1e-6
HEAD_DIM = 128
REC_CHUNK = 64
REC_HALF = 32
REC_CHUNKS_PER_STEP = 4
MLP_CHUNK = 128
MLP_GROUPS = 4
N_GROUPS = 4
E_PER_GROUP = 8
N_EXPERTS = N_GROUPS * E_PER_GROUP
TOP_K = 2
MOE_BLOCK = 256
PAST_LEN = 2048
LANES = 128
SUBLANES = 8
ROUTER_ROWS = 48
VMEM_LIMIT = 56 * 1024 * 1024
LOG2_E = 1.4426950408889634
WEIGHT_STAGE_ROWS = 256


def _pick(n, candidates):
    for c in candidates:
        if n % c == 0:
            return c
    raise ValueError(f"no tile in {candidates} divides {n}")


def _dot(a, b):
    return jnp.dot(a, b, preferred_element_type=F32)


def _dot_nt(a, b, precision=None):
    return lax.dot_general(a, b, (((1,), (1,)), ((), ())), precision=precision, preferred_element_type=F32)


def _dot_tn(a, b):
    return lax.dot_general(a, b, (((0,), (0,)), ((), ())), preferred_element_type=F32)


def _rms(x, g):
    return x * lax.rsqrt(jnp.mean(x * x, axis=-1, keepdims=True) + EPS) * g


def _gelu(x):
    c = math.sqrt(2.0 / math.pi)
    return x * (0.5 * (1.0 + jnp.tanh(c * (x + 0.044715 * (x * x * x)))))


def _sigmoid(x):
    return 0.5 * jnp.tanh(0.5 * x) + 0.5


def _store_chunks(ref, x):
    for c in range(ref.shape[0]):
        ref[c] = x[:, c * LANES:(c + 1) * LANES].astype(ref.dtype)


def _load_chunks(ref):
    return jnp.concatenate([ref[c] for c in range(ref.shape[0])], axis=1)


HIGH_HALF = 0xFFFF0000


def _pack_bf16_pairs(x):
    half = x.shape[1] // 2
    lo = jnp.right_shift(pltpu.bitcast(x[:, :half], jnp.uint32), jnp.uint32(16))
    hi = jnp.bitwise_and(pltpu.bitcast(x[:, half:], jnp.uint32), jnp.uint32(HIGH_HALF))
    return jnp.bitwise_or(lo, hi)


def _unpack_bf16_pairs(p):
    lo = pltpu.bitcast(jnp.left_shift(p, jnp.uint32(16)), F32)
    hi = pltpu.bitcast(jnp.bitwise_and(p, jnp.uint32(HIGH_HALF)), F32)
    return jnp.concatenate([lo, hi], axis=1).astype(BF16)


def _row_copy(src_ref, src_row, dst_ref, dst_row, sem):
    return pltpu.make_async_copy(src_ref.at[:, pl.ds(src_row, 1), :], dst_ref.at[:, pl.ds(dst_row, 1), :], sem)


def _rows_copy(src_ref, dst_ref, dst_row, n_rows, sem):
    return pltpu.make_async_copy(src_ref, dst_ref.at[:, pl.ds(dst_row, n_rows), :], sem)


def _load_as_bf16(w_hbm, w_vmem, stage_ref, sem):
    half = stage_ref.shape[0] // 2
    rows, cols = min(half, w_hbm.shape[0]), min(stage_ref.shape[1], w_hbm.shape[1])
    assert w_hbm.shape[0] % rows == 0 and w_hbm.shape[1] % cols == 0
    chunks = [(r, c) for r in range(0, w_hbm.shape[0], rows) for c in range(0, w_hbm.shape[1], cols)]

    def slot(i):
        return stage_ref.at[pl.ds((i % 2) * half, rows), pl.ds(0, cols)]

    def copy(i):
        r, c = chunks[i]
        return pltpu.make_async_copy(w_hbm.at[pl.ds(r, rows), pl.ds(c, cols)], slot(i), sem.at[i % 2])

    copy(0).start()
    for i, (r, c) in enumerate(chunks):
        if i + 1 < len(chunks):
            copy(i + 1).start()
        copy(i).wait()
        w_vmem[pl.ds(r, rows), pl.ds(c, cols)] = slot(i)[...].astype(BF16)


def _in_proj_body(xp_ref, xs_ref, g_ref, w_ref, z_ref, xn_ref, *, n_prompt_tiles):
    first = pl.program_id(1) == 0
    is_p = pl.program_id(0) < n_prompt_tiles

    @pl.when(first & is_p)
    def _():
        xn_ref[...] = _rms(xp_ref[...], g_ref[...]).astype(BF16)

    @pl.when(first & jnp.logical_not(is_p))
    def _():
        xn_ref[...] = _rms(xs_ref[...], g_ref[...]).astype(BF16)

    z_ref[...] = _dot(xn_ref[...], w_ref[...].astype(BF16)).astype(z_ref.dtype)


def _in_proj(xp, xs, g, w):
    (n_p, d), n_s = xp.shape, xs.shape[0]
    width = w.shape[1]
    tm = _pick(math.gcd(n_p, n_s), (1024, 512, 256))
    tn = _pick(width, (1024, 512, 256, 128))
    npt, nst = n_p // tm, n_s // tm
    return pl.pallas_call(
        functools.partial(_in_proj_body, n_prompt_tiles=npt),
        grid=(npt + nst, width // tn),
        in_specs=[pl.BlockSpec((tm, d), lambda i, j: (jnp.minimum(i, npt - 1), 0)),
                  pl.BlockSpec((tm, d), lambda i, j: (jnp.clip(i - npt, 0, nst - 1), 0),
                               pipeline_mode=pl.Buffered(1)),
                  pl.BlockSpec((1, d), lambda i, j: (0, 0)),
                  pl.BlockSpec((d, tn), lambda i, j: (0, j))],
        out_specs=pl.BlockSpec((tm, tn), lambda i, j: (i, j)),
        out_shape=jax.ShapeDtypeStruct((n_p + n_s, width), BF16),
        scratch_shapes=[pltpu.VMEM((tm, d), BF16)],
        compiler_params=pltpu.CompilerParams(dimension_semantics=("parallel", "arbitrary"),
                                             vmem_limit_bytes=VMEM_LIMIT),
        name="in_proj",
    )(xp, xs, g, w)


def _cumsum_rows(x, tril3):
    hi = x.astype(BF16)
    r1 = x - hi.astype(F32)
    mid = r1.astype(BF16)
    lo = (r1 - mid.astype(F32)).astype(BF16)
    return _dot(tril3, jnp.concatenate([hi, mid, lo], axis=0))


def _hgrn_scratch(chunk, heads):
    half = min(REC_HALF, chunk)
    wblk = heads * HEAD_DIM
    per_chunk = ([pltpu.VMEM((chunk, wblk), BF16),
                  pltpu.VMEM((chunk, wblk), BF16),
                  pltpu.VMEM((chunk, wblk), BF16),
                  pltpu.VMEM((SUBLANES, wblk), F32),
                  pltpu.VMEM((heads, chunk, chunk), BF16),
                  pltpu.VMEM((chunk, wblk), F32)]
                 + [pltpu.VMEM((half * (g + 1), wblk), BF16) for g in range(chunk // half)])
    return [pltpu.VMEM((heads, HEAD_DIM, HEAD_DIM), F32)] + per_chunk * REC_CHUNKS_PER_STEP


def _hgrn_passes(q_ref, f_ref, i_ref, zo_ref, lb_ref, gh_ref, og_ref, scratch, rows, og_rows, chunk, heads,
                 reset=None):
    st_ref, qs_ref, qe_ref, kd_ref, dec_ref, a_ref, o_ref = scratch[:7]
    key_refs = scratch[7:]
    half = min(REC_HALF, chunk)
    n_half = chunk // half
    tril = (lax.broadcasted_iota(jnp.int32, (chunk, chunk), 0)
            >= lax.broadcasted_iota(jnp.int32, (chunk, chunk), 1))
    tril = jnp.where(tril, 1.0, 0.0).astype(BF16)
    tril3 = jnp.concatenate([tril, tril, tril], axis=1)
    masks = []
    for g in range(n_half):
        r = lax.broadcasted_iota(jnp.int32, (half, half * (g + 1)), 0)
        c = lax.broadcasted_iota(jnp.int32, (half, half * (g + 1)), 1)
        masks.append(r + g * half >= c)
    head_slices = [slice(h * HEAD_DIM, (h + 1) * HEAD_DIM) for h in range(heads)]

    def operands():
        for hs in head_slices:
            q = q_ref[rows, hs].astype(F32)
            lb = lb_ref[:, hs]
            c1 = 0.5 * (1.0 - lb)
            f = (lb + c1) + c1 * jnp.tanh(0.5 * f_ref[rows, hs].astype(F32))
            kk = 1.0 - f
            b = _cumsum_rows(jnp.log(f), tril3) * LOG2_E
            b_last = b[chunk - 1:chunk, :]
            dec_ref[0:1, hs] = jnp.exp2(b_last)
            mids, ks = [], []
            for g in range(n_half):
                rg = slice(g * half, (g + 1) * half)
                mid = b[g * half + half // 2 - 1:g * half + half // 2, :]
                qs_g = q[rg] * jnp.exp2(b[rg] - mid)
                ks_g = kk[rg] * jnp.exp2(mid - b[rg])
                mids.append(mid)
                ks.append(ks_g)
                qs_ref[rg, hs] = qs_g.astype(BF16)
                qe_ref[rg, hs] = (qs_g * jnp.exp2(mid)).astype(BF16)
                kd_ref[rg, hs] = (ks_g * jnp.exp2(b_last - mid)).astype(BF16)
                for gp in range(g):
                    key_refs[g][gp * half:(gp + 1) * half, hs] = (
                        ks[gp] * jnp.exp2(mid - mids[gp])).astype(BF16)
                key_refs[g][rg, hs] = ks_g.astype(BF16)

    def scores():
        for h, hs in enumerate(head_slices):
            for g in range(n_half):
                rg = slice(g * half, (g + 1) * half)
                a = _dot_nt(qs_ref[rg, hs], key_refs[g][:, hs])
                a_ref[h, rg, 0:half * (g + 1)] = jnp.where(masks[g], a, 0.0).astype(BF16)

    def outputs_and_state():
        for h, hs in enumerate(head_slices):
            v = i_ref[rows, hs]
            st = st_ref[h]
            if reset is not None:
                st = jnp.where(reset, 0.0, st)
            o_inter = _dot_nt(qe_ref[:, hs], st.astype(BF16))
            for g in range(n_half):
                rg = slice(g * half, (g + 1) * half)
                o_ref[rg, hs] = o_inter[rg] + _dot(a_ref[h, rg, 0:half * (g + 1)], v[0:half * (g + 1)])
            st_ref[h] = st * dec_ref[0:1, hs] + _dot_tn(v, kd_ref[:, hs])

    def normalise():
        for hs in head_slices:
            o = o_ref[:, hs]
            o_n = o * lax.rsqrt(jnp.mean(o * o, axis=-1, keepdims=True) + EPS) * (0.5 * gh_ref[:, hs])
            gate2 = jnp.tanh(0.5 * zo_ref[rows, hs].astype(F32)) + 1.0
            og_ref[og_rows, hs] = (o_n * gate2).astype(og_ref.dtype)

    return [operands, scores, outputs_and_state, normalise]


def _hgrn_body(q_ref, f_ref, i_ref, zo_ref, s0_ref, lb_ref, gh_ref, og_ref, sout_ref, *scratch,
               chunk, n_chunks, heads):
    ti = pl.program_id(2)
    st_ref = scratch[0]

    @pl.when(ti == 0)
    def _():
        for h in range(heads):
            st_ref[h] = s0_ref[0, h].T

    per_step = math.gcd(n_chunks, REC_CHUNKS_PER_STEP)
    n_set = (len(scratch) - 1) // REC_CHUNKS_PER_STEP
    sets = [(st_ref,) + tuple(scratch[1 + k * n_set:1 + (k + 1) * n_set]) for k in range(per_step)]

    def chunk_step(ci, carry):
        passes = []
        for k in range(per_step):
            rows = pl.ds(pl.multiple_of((ci * per_step + k) * chunk, chunk), chunk)
            passes.append(_hgrn_passes(q_ref, f_ref, i_ref, zo_ref, lb_ref, gh_ref, og_ref, sets[k], rows, rows,
                                       chunk, heads))
        for stage in zip(*passes):
            for run_pass in stage:
                run_pass()
        return carry

    lax.fori_loop(0, n_chunks // per_step, chunk_step, 0)

    @pl.when(ti == pl.num_programs(2) - 1)
    def _():
        for h in range(heads):
            sout_ref[0, h] = st_ref[h].T


def _hgrn(z, s0, lb, g_head, *, row_base, batch, seq, d_model, name):
    n_heads = d_model // HEAD_DIM
    heads = min(16, n_heads)
    chunk = min(REC_CHUNK, seq)
    half = min(REC_HALF, chunk)
    assert seq % chunk == 0 and chunk % half == 0
    rt = max(_pick(seq, (512, 256, 128, 64, 32)), chunk)
    assert row_base % rt == 0
    tiles = seq // rt
    wblk = heads * HEAD_DIM
    cpb = d_model // wblk

    def zspec(section):
        return pl.BlockSpec((rt, wblk),
                            lambda b, hg, i: (row_base // rt + b * tiles + i, section * cpb + hg))

    return pl.pallas_call(
        functools.partial(_hgrn_body, chunk=chunk, n_chunks=rt // chunk, heads=heads),
        grid=(batch, n_heads // heads, tiles),
        in_specs=[zspec(0), zspec(1), zspec(2), zspec(3),
                  pl.BlockSpec((1, heads, HEAD_DIM, HEAD_DIM), lambda b, hg, i: (b, hg, 0, 0)),
                  pl.BlockSpec((1, wblk), lambda b, hg, i: (0, hg)),
                  pl.BlockSpec((1, wblk), lambda b, hg, i: (0, hg))],
        out_specs=[pl.BlockSpec((rt, wblk), lambda b, hg, i: (b * tiles + i, hg)),
                   pl.BlockSpec((1, heads, HEAD_DIM, HEAD_DIM), lambda b, hg, i: (b, hg, 0, 0))],
        out_shape=[jax.ShapeDtypeStruct((batch * seq, d_model), BF16),
                   jax.ShapeDtypeStruct((batch, n_heads, HEAD_DIM, HEAD_DIM), F32)],
        scratch_shapes=_hgrn_scratch(chunk, heads),
        compiler_params=pltpu.CompilerParams(dimension_semantics=("parallel", "parallel", "arbitrary"),
                                             vmem_limit_bytes=VMEM_LIMIT),
        name=name,
    )(z, z, z, z, s0, lb, g_head)


def _mix_body(zu_ref, zv_ref, zga_ref, zgb_ref, ogp_ref, ogs_ref, xp_ref, xs_ref, lng_ref, lnb_ref,
              wsp_ref, wss_ref, bsp_ref, bss_ref, wpa_hbm, wpb_hbm, wo_hbm,
              h1_ref, vnp_ref, vns_ref, sg_ref, wpa_ref, wpb_ref, wo_ref, stage_ref, wsem, ya_ref,
              *, n_prompt_tiles, dec_seq):
    @pl.when(pl.program_id(0) == 0)
    def _():
        _load_as_bf16(wpa_hbm, wpa_ref, stage_ref, wsem)
        _load_as_bf16(wpb_hbm, wpb_ref, stage_ref, wsem)
        _load_as_bf16(wo_hbm, wo_ref, stage_ref, wsem)

    is_p = pl.program_id(0) < n_prompt_tiles
    tm, width = zu_ref.shape
    gd = width // MLP_GROUPS
    og = jnp.where(is_p, ogp_ref[...].astype(F32), ogs_ref[...].astype(F32)).astype(BF16)
    n_chunks = tm // MLP_CHUNK
    d = h1_ref.shape[1]
    col = d // (2 * n_chunks)

    def ya_slice(j):
        cs = slice(j * col, (j + 1) * col)
        ya_ref[:, cs] = _dot(og, wpa_ref[:, cs])

    r = lax.broadcasted_iota(jnp.int32, (MLP_CHUNK, MLP_CHUNK), 0)
    c = lax.broadcasted_iota(jnp.int32, (MLP_CHUNK, MLP_CHUNK), 1)
    causal = r >= c
    same_stream = (r // dec_seq) == (c // dec_seq)
    w_mix, bias = [], []
    for g in range(MLP_GROUPS):
        w_p = jnp.where(causal, wsp_ref[g], 0.0)
        w_s = jnp.where(causal & same_stream, wss_ref[g], 0.0)
        w_mix.append(jnp.where(is_p, w_p, w_s).astype(BF16))
        bias.append(jnp.where(is_p, bsp_ref[g], bss_ref[g]))
    for cc in range(n_chunks):
        rows = slice(cc * MLP_CHUNK, (cc + 1) * MLP_CHUNK)
        ya_slice(2 * cc)
        u = _gelu(zu_ref[rows, :].astype(F32))
        gv = _gelu(zv_ref[rows, :].astype(F32))
        xc = gv - jnp.mean(gv, axis=-1, keepdims=True)
        vn = xc * lax.rsqrt(jnp.mean(xc * xc, axis=-1, keepdims=True) + EPS) * lng_ref[...] + lnb_ref[...]
        vnp_ref[rows, :] = vn
        vns_ref[rows, :] = vn
        ya_slice(2 * cc + 1)
        vnb = vn.astype(BF16)
        for g in range(MLP_GROUPS):
            cols = slice(g * gd, (g + 1) * gd)
            s = _dot(w_mix[g], vnb[:, cols]) + bias[g]
            sg_ref[rows, cols] = (u[:, cols] * s).astype(BF16)
    y_b = _dot(sg_ref[...], wpb_ref[...])
    m = _sigmoid(zga_ref[...].astype(F32)) * ya_ref[...] + _sigmoid(zgb_ref[...].astype(F32)) * y_b
    x = jnp.where(is_p, xp_ref[...], xs_ref[...])
    h1_ref[...] = x + _dot(m.astype(BF16), wo_ref[...])


def _mix(z, og_p, og_s, xp, xs, ln_g, ln_b, ws_p, ws_s, bs_p, bs_s, w_pa, w_pb, w_o, *, batch, dec_seq):
    (n_p, d), n_s = xp.shape, xs.shape[0]
    n = n_p + n_s
    width = w_pb.shape[0]
    tm = 256
    assert n_s % tm == 0 and (n_p // batch) % tm == 0 and tm % MLP_CHUNK == 0 and MLP_CHUNK % dec_seq == 0
    npt, nst = n_p // tm, n_s // tm
    tpb = npt // batch
    u_blk = 4 * d // width
    ga_blk = (4 * d + 2 * width) // d
    const2 = lambda i: (0, 0)
    const3 = lambda i: (0, 0, 0)
    p_map = lambda i: (jnp.minimum(i, npt - 1), 0)
    s_map = lambda i: (jnp.clip(i - npt, 0, nst - 1), 0)
    return pl.pallas_call(
        functools.partial(_mix_body, n_prompt_tiles=npt, dec_seq=dec_seq),
        grid=(n // tm,),
        in_specs=[pl.BlockSpec((tm, width), lambda i: (i, u_blk)),
                  pl.BlockSpec((tm, width), lambda i: (i, u_blk + 1)),
                  pl.BlockSpec((tm, d), lambda i: (i, ga_blk)),
                  pl.BlockSpec((tm, d), lambda i: (i, ga_blk + 1)),
                  pl.BlockSpec((tm, d), p_map),
                  pl.BlockSpec((tm, d), s_map),
                  pl.BlockSpec((tm, d), p_map),
                  pl.BlockSpec((tm, d), s_map),
                  pl.BlockSpec((1, width), const2),
                  pl.BlockSpec((1, width), const2),
                  pl.BlockSpec((MLP_GROUPS, MLP_CHUNK, MLP_CHUNK), const3),
                  pl.BlockSpec((MLP_GROUPS, MLP_CHUNK, MLP_CHUNK), const3),
                  pl.BlockSpec((MLP_GROUPS, MLP_CHUNK, 1), const3),
                  pl.BlockSpec((MLP_GROUPS, MLP_CHUNK, 1), const3),
                  pl.BlockSpec(memory_space=pl.ANY),
                  pl.BlockSpec(memory_space=pl.ANY),
                  pl.BlockSpec(memory_space=pl.ANY)],
        out_specs=[pl.BlockSpec((tm, d), lambda i: (i, 0)),
                   pl.BlockSpec((tm, width), lambda i: (jnp.where(i < npt, i // tpb, batch), 0)),
                   pl.BlockSpec((tm, width), lambda i: (jnp.where(i < npt, nst, i - npt), 0))],
        out_shape=[jax.ShapeDtypeStruct((n, d), F32),
                   jax.ShapeDtypeStruct(((batch + 1) * tm, width), F32),
                   jax.ShapeDtypeStruct(((nst + 1) * tm, width), F32)],
        scratch_shapes=[pltpu.VMEM((tm, width), BF16),
                        pltpu.VMEM((d, d), BF16), pltpu.VMEM((width, d), BF16), pltpu.VMEM((d, d), BF16),
                        pltpu.VMEM((WEIGHT_STAGE_ROWS, d), F32), pltpu.SemaphoreType.DMA((2,)),
                        pltpu.VMEM((tm, d), F32)],
        compiler_params=pltpu.CompilerParams(dimension_semantics=("arbitrary",),
                                             vmem_limit_bytes=VMEM_LIMIT),
        name="mix",
    )(z, z, z, z, og_p, og_s, xp, xs, ln_g, ln_b, ws_p, ws_s, bs_p, bs_s, w_pa, w_pb, w_o)


def _router_body(h_ref, g_ref, wr_ref, br_ref, c_ref, idx_ref, gate_ref, cnt_ref, carry_ref):
    @pl.when(pl.program_id(0) == 0)
    def _():
        carry_ref[...] = jnp.zeros_like(carry_ref)

    tm = h_ref.shape[0]
    c = _rms(h_ref[...], g_ref[...])
    wr = wr_ref[...]
    wr_hi = wr.astype(BF16)
    wr_lo = (wr - wr_hi.astype(F32)).astype(BF16)
    c_hi = c.astype(BF16)
    c_lo = (c - c_hi.astype(F32)).astype(BF16)
    _store_chunks(c_ref, _pack_bf16_pairs(c_hi.astype(F32)))
    both = _dot_nt(jnp.concatenate([wr_hi, wr_lo], axis=0), c_hi)
    lt = both[0:ROUTER_ROWS] + both[ROUTER_ROWS:2 * ROUTER_ROWS] + _dot_nt(wr_hi, c_lo) + br_ref[...]
    le = lt[0:N_EXPERTS]
    lg = lt[N_EXPERTS:N_EXPERTS + N_GROUPS]
    gmax = jnp.max(lg, axis=0, keepdims=True)
    p_sel = 1.0 / jnp.sum(jnp.exp(lg - gmax), axis=0, keepdims=True)
    best = lg[0:1]
    gi = jnp.zeros((1, tm), jnp.int32)
    for g in range(1, N_GROUPS):
        better = lg[g:g + 1] > best
        gi = jnp.where(better, g, gi)
        best = jnp.where(better, lg[g:g + 1], best)
    leg = jnp.zeros((E_PER_GROUP, tm), F32)
    for g in range(N_GROUPS):
        leg = jnp.where(gi == g, le[g * E_PER_GROUP:(g + 1) * E_PER_GROUP], leg)
    sub = lax.broadcasted_iota(jnp.int32, (E_PER_GROUP, tm), 0).astype(F32)
    v1 = jnp.max(leg, axis=0, keepdims=True)
    i1 = jnp.min(jnp.where(leg == v1, sub, float(E_PER_GROUP)), axis=0, keepdims=True)
    rest = jnp.where(sub == i1, -jnp.inf, leg)
    v2 = jnp.max(rest, axis=0, keepdims=True)
    i2 = jnp.min(jnp.where(rest == v2, sub, float(E_PER_GROUP)), axis=0, keepdims=True)
    e2 = jnp.exp(v2 - v1)
    den = 1.0 + e2
    gate0 = p_sel * (1.0 / den)
    gate1 = p_sel * (e2 / den)
    ex0 = gi * E_PER_GROUP + i1.astype(jnp.int32)
    ex1 = gi * E_PER_GROUP + i2.astype(jnp.int32)
    eid = lax.broadcasted_iota(jnp.int32, (N_EXPERTS, tm), 0)
    oh0 = eid == ex0
    oh1 = eid == ex1
    oh = jnp.where(oh0 | oh1, 1.0, 0.0)
    upper = jnp.where(lax.broadcasted_iota(jnp.int32, (tm, tm), 0) < lax.broadcasted_iota(jnp.int32, (tm, tm), 1),
                      1.0, 0.0).astype(BF16)
    before = _dot(oh.astype(BF16), upper) + carry_ref[:, 0:1]
    rank0 = jnp.sum(jnp.where(oh0, before, 0.0), axis=0, keepdims=True)
    rank1 = jnp.sum(jnp.where(oh1, before, 0.0), axis=0, keepdims=True)
    carry = carry_ref[...] + jnp.sum(oh, axis=1, keepdims=True)
    carry_ref[...] = carry
    cnt_ref[...] = carry.astype(jnp.int32)
    idx_ref[...] = jnp.zeros_like(idx_ref)
    idx_ref[0:1, :] = ex0
    idx_ref[1:2, :] = ex1
    idx_ref[2:3, :] = rank0.astype(jnp.int32)
    idx_ref[3:4, :] = rank1.astype(jnp.int32)
    gate_ref[...] = jnp.zeros_like(gate_ref)
    gate_ref[0:1, :] = gate0
    gate_ref[1:2, :] = gate1


def _router(h1, g_ffn, wr, br):
    n, d = h1.shape
    tm = _pick(n, (512, 256))
    return pl.pallas_call(
        _router_body,
        grid=(n // tm,),
        in_specs=[pl.BlockSpec((tm, d), lambda i: (i, 0)),
                  pl.BlockSpec((1, d), lambda i: (0, 0)),
                  pl.BlockSpec((ROUTER_ROWS, d), lambda i: (0, 0)),
                  pl.BlockSpec((ROUTER_ROWS, 1), lambda i: (0, 0))],
        out_specs=[pl.BlockSpec((d // (2 * LANES), tm, LANES), lambda i: (0, i, 0)),
                   pl.BlockSpec((SUBLANES, tm), lambda i: (0, i)),
                   pl.BlockSpec((SUBLANES, tm), lambda i: (0, i)),
                   pl.BlockSpec((N_EXPERTS, LANES), lambda i: (0, 0))],
        out_shape=[jax.ShapeDtypeStruct((d // (2 * LANES), n, LANES), jnp.uint32),
                   jax.ShapeDtypeStruct((SUBLANES, n), jnp.int32),
                   jax.ShapeDtypeStruct((SUBLANES, n), F32),
                   jax.ShapeDtypeStruct((N_EXPERTS, LANES), jnp.int32)],
        scratch_shapes=[pltpu.VMEM((N_EXPERTS, LANES), F32)],
        compiler_params=pltpu.CompilerParams(dimension_semantics=("arbitrary",),
                                             vmem_limit_bytes=VMEM_LIMIT),
        name="router",
    )(h1, g_ffn, wr, br)


def _dispatch_body(fill_lo_ref, fill_hi_ref, dest_ref, c_ref, xs_ref, zero_ref, sem, zsem):
    td = c_ref.shape[1]

    @pl.when(pl.program_id(0) == 0)
    def _():
        zero_ref[...] = jnp.zeros_like(zero_ref)

        def per_expert(e, carry):
            lo, hi = fill_lo_ref[e], fill_hi_ref[e]

            def start(r, c2):
                _row_copy(zero_ref, 0, xs_ref, r, zsem).start()
                return c2

            def wait(r, c2):
                _row_copy(zero_ref, 0, xs_ref, 0, zsem).wait()
                return c2

            lax.fori_loop(lo, hi, start, 0)
            lax.fori_loop(lo, hi, wait, 0)
            return carry

        lax.fori_loop(0, N_EXPERTS, per_expert, 0)

        def block_copy(j):
            return _rows_copy(zero_ref, xs_ref, j * MOE_BLOCK, MOE_BLOCK, zsem)

        first_unused = fill_hi_ref[N_EXPERTS - 1] // MOE_BLOCK
        n_blocks = xs_ref.shape[1] // MOE_BLOCK

        def start_block(j, carry):
            block_copy(j).start()
            return carry

        def wait_block(j, carry):
            block_copy(0).wait()
            return carry

        lax.fori_loop(first_unused, n_blocks, start_block, 0)
        lax.fori_loop(first_unused, n_blocks, wait_block, 0)

    for t in range(td):
        for k in range(TOP_K):
            _row_copy(c_ref, t, xs_ref, dest_ref[k, t], sem).start(priority=k % 2)
    for k in range(TOP_K):
        _rows_copy(c_ref, xs_ref, 0, td, sem).wait()


def _dispatch(fill_lo, fill_hi, dest, c, n_slots):
    nc, n, _ = c.shape
    td = _pick(n, (512, 256))
    grid_spec = pltpu.PrefetchScalarGridSpec(
        num_scalar_prefetch=2,
        grid=(n // td,),
        in_specs=[pl.BlockSpec((TOP_K, td), lambda i, lo, hi: (0, i), memory_space=pltpu.SMEM),
                  pl.BlockSpec((nc, td, LANES), lambda i, lo, hi: (0, i, 0))],
        out_specs=pl.BlockSpec(memory_space=pl.ANY),
        scratch_shapes=[pltpu.VMEM((nc, MOE_BLOCK, LANES), c.dtype),
                        pltpu.SemaphoreType.DMA(()), pltpu.SemaphoreType.DMA(())],
    )
    return pl.pallas_call(
        _dispatch_body,
        grid_spec=grid_spec,
        out_shape=jax.ShapeDtypeStruct((nc, n_slots, LANES), c.dtype),
        compiler_params=pltpu.CompilerParams(dimension_semantics=("arbitrary",),
                                             vmem_limit_bytes=VMEM_LIMIT),
        name="dispatch",
    )(fill_lo, fill_hi, dest, c)


def _expert_body(be_ref, na_ref, nxt_ref, xs_ref, wg_hbm, wu_hbm, wd_hbm, yb_ref,
                 wg_ref, wu_ref, wd_ref, sg_ref, su_ref, sd_ref, wsem):
    j = pl.program_id(0)
    active = j < na_ref[0]
    e = be_ref[j]
    first = (j == 0) | (be_ref[jnp.maximum(j - 1, 0)] != e)

    def weight_copies(expert):
        return (pltpu.make_async_copy(wg_hbm.at[expert], sg_ref, wsem.at[0]),
                pltpu.make_async_copy(wu_hbm.at[expert], su_ref, wsem.at[1]),
                pltpu.make_async_copy(wd_hbm.at[expert], sd_ref, wsem.at[2]))

    @pl.when(j == 0)
    def _():
        for cp in weight_copies(e):
            cp.start()

    @pl.when(active & first)
    def _():
        for cp, stage, dst in zip(weight_copies(e), (sg_ref, su_ref, sd_ref), (wg_ref, wu_ref, wd_ref)):
            cp.wait()

            def convert(r, carry, stage=stage, dst=dst):
                rows = pl.ds(pl.multiple_of(r * LANES, LANES), LANES)
                dst[rows, :] = stage[rows, :].astype(BF16)
                return carry

            lax.fori_loop(0, stage.shape[0] // LANES, convert, 0)

        @pl.when(nxt_ref[e] < N_EXPERTS)
        def _():
            for cp in weight_copies(nxt_ref[e]):
                cp.start()

    @pl.when(active)
    def _():
        x = _unpack_bf16_pairs(_load_chunks(xs_ref))
        a = _dot(x, wg_ref[...])
        b = _dot(x, wu_ref[...])
        hid = (a * _sigmoid(a)) * b
        _store_chunks(yb_ref, _dot(hid.astype(BF16), wd_ref[...]))

    @pl.when(jnp.logical_not(active))
    def _():
        yb_ref[...] = jnp.zeros_like(yb_ref)


def _experts(block_exp, n_active, next_exp, xs, w_gate, w_up, w_down):
    nc_in, n_slots, _ = xs.shape
    d, de = w_gate.shape[1:]
    nc = d // LANES
    nb = n_slots // MOE_BLOCK
    assert d % LANES == 0 and de % LANES == 0
    blk = lambda j, be, na, nxt: (0, jnp.minimum(j, na[0] - 1), 0)
    grid_spec = pltpu.PrefetchScalarGridSpec(
        num_scalar_prefetch=3,
        grid=(nb,),
        in_specs=[pl.BlockSpec((nc_in, MOE_BLOCK, LANES), blk),
                  pl.BlockSpec(memory_space=pl.ANY),
                  pl.BlockSpec(memory_space=pl.ANY),
                  pl.BlockSpec(memory_space=pl.ANY)],
        out_specs=pl.BlockSpec((nc, MOE_BLOCK, LANES), lambda j, be, na, nxt: (0, j, 0)),
        scratch_shapes=[pltpu.VMEM((d, de), BF16), pltpu.VMEM((d, de), BF16), pltpu.VMEM((de, d), BF16),
                        pltpu.VMEM((d, de), F32), pltpu.VMEM((d, de), F32), pltpu.VMEM((de, d), F32),
                        pltpu.SemaphoreType.DMA((3,))],
    )
    return pl.pallas_call(
        _expert_body,
        grid_spec=grid_spec,
        out_shape=jax.ShapeDtypeStruct((nc, n_slots, LANES), F32),
        compiler_params=pltpu.CompilerParams(dimension_semantics=("arbitrary",),
                                             vmem_limit_bytes=VMEM_LIMIT),
        name="experts",
    )(block_exp, n_active, next_exp, xs, w_gate, w_up, w_down)


def _combine_body(dcur_ref, dnxt_ref, gate_ref, h1_ref, p_ref, yb_ref, gple_ref, wpg_hbm, wpp_hbm, gfin_ref,
                  yp_ref, ys_ref, buf_ref, sem, wpg_ref, wpp_ref, stage_ref, wsem,
                  *, n_prompt_tiles, final_norm):
    i = pl.program_id(0)

    @pl.when(i == 0)
    def _():
        _load_as_bf16(wpg_hbm, wpg_ref, stage_ref, wsem)
        _load_as_bf16(wpp_hbm, wpp_ref, stage_ref, wsem)

    n_tiles = pl.num_programs(0)
    tc = h1_ref.shape[0]
    slot = lax.rem(i, 2)

    @pl.when(i == 0)
    def _():
        def body(t, carry):
            for k in range(TOP_K):
                _row_copy(yb_ref, dcur_ref[k, t], buf_ref.at[0, k], t, sem.at[0]).start()
            return carry

        lax.fori_loop(0, tc, body, 0)

    def wait_slot(s):
        for k in range(TOP_K):
            pltpu.make_async_copy(yb_ref.at[:, pl.ds(0, tc), :], buf_ref.at[s, k], sem.at[s]).wait()

    wait_slot(slot)
    gt = gate_ref[...].T
    h2 = h1_ref[...] + (gt[:, 0:1] * _load_chunks(buf_ref.at[slot, 0])
                        + gt[:, 1:2] * _load_chunks(buf_ref.at[slot, 1]))
    for t in range(tc):
        for k in range(TOP_K):
            _row_copy(yb_ref, dnxt_ref[k, t], buf_ref.at[1 - slot, k], t, sem.at[1 - slot]).start(priority=k % 2)
    a = _rms(h2, gple_ref[...]).astype(BF16)
    h3 = h2 + _sigmoid(_dot(a, wpg_ref[...])) * _dot(p_ref[...].astype(BF16), wpp_ref[...])
    if final_norm:
        h3 = _rms(h3, gfin_ref[...])

    @pl.when(i < n_prompt_tiles)
    def _():
        yp_ref[...] = h3

    @pl.when(i >= n_prompt_tiles)
    def _():
        ys_ref[...] = h3

    @pl.when(i == n_tiles - 1)
    def _():
        wait_slot(1 - slot)


def _combine(dest, gates, h1, p, yb, g_ple, w_pg, w_pp, g_final, *, n_prompt, final_norm):
    n, d = h1.shape
    ple = p.shape[1]
    tc = 256
    assert n_prompt % tc == 0 and (n - n_prompt) % tc == 0
    nt, npt = n // tc, n_prompt // tc
    nst = nt - npt
    const2 = lambda i: (0, 0)
    return pl.pallas_call(
        functools.partial(_combine_body, n_prompt_tiles=npt, final_norm=final_norm),
        grid=(nt,),
        in_specs=[pl.BlockSpec((TOP_K, tc), lambda i: (0, i), memory_space=pltpu.SMEM),
                  pl.BlockSpec((TOP_K, tc), lambda i: (0, jnp.minimum(i + 1, nt - 1)), memory_space=pltpu.SMEM),
                  pl.BlockSpec((SUBLANES, tc), lambda i: (0, i)),
                  pl.BlockSpec((tc, d), lambda i: (i, 0)),
                  pl.BlockSpec((tc, ple), lambda i: (i, 0)),
                  pl.BlockSpec(memory_space=pl.ANY),
                  pl.BlockSpec((1, d), const2),
                  pl.BlockSpec(memory_space=pl.ANY),
                  pl.BlockSpec(memory_space=pl.ANY),
                  pl.BlockSpec((1, d), const2)],
        out_specs=[pl.BlockSpec((tc, d), lambda i: (jnp.minimum(i, npt - 1), 0)),
                   pl.BlockSpec((tc, d), lambda i: (jnp.clip(i - npt, 0, nst - 1), 0))],
        out_shape=[jax.ShapeDtypeStruct((n_prompt, d), F32),
                   jax.ShapeDtypeStruct((n - n_prompt, d), F32)],
        scratch_shapes=[pltpu.VMEM((2, TOP_K, d // LANES, tc, LANES), F32), pltpu.SemaphoreType.DMA((2,)),
                        pltpu.VMEM((d, d), BF16), pltpu.VMEM((ple, d), BF16),
                        pltpu.VMEM((WEIGHT_STAGE_ROWS, d), F32), pltpu.SemaphoreType.DMA((2,))],
        compiler_params=pltpu.CompilerParams(dimension_semantics=("arbitrary",),
                                             vmem_limit_bytes=VMEM_LIMIT),
        name="combine",
    )(dest, dest, gates, h1, p, yb, g_ple, w_pg, w_pp, g_final)


def _layer(hp, hs, p, s0_sample, lb, batch, seq, dec_batch, dec_seq,
           g_mix, w_in, g_head, w_pa, ln_v_g, ln_v_b, w_s, b_s, w_pb, w_o,
           g_ffn, w_gr, b_gr, w_er, b_er, w_gate, w_up, w_down, g_ple, w_pg, w_pp, g_final, final_norm):
    n_prompt, d = hp.shape
    n = n_prompt + hs.shape[0]
    n_heads = d // HEAD_DIM
    row = lambda a: a.reshape(1, -1).astype(F32)

    z = _in_proj(hp, hs, row(g_mix), w_in)

    lb_row, gh_row = row(lb), row(g_head)
    s0_prompt = jnp.zeros((batch, n_heads, HEAD_DIM, HEAD_DIM), F32)
    og_p, st_p = _hgrn(z, s0_prompt, lb_row, gh_row, row_base=0, batch=batch, seq=seq,
                       d_model=d, name="hgrn_prompt")
    og_s, st_s = _hgrn(z, s0_sample.astype(F32), lb_row, gh_row, row_base=n_prompt, batch=dec_batch,
                       seq=dec_seq, d_model=d, name="hgrn_sample")

    start = PAST_LEN % MLP_CHUNK
    assert start + dec_seq <= MLP_CHUNK
    rep = MLP_CHUNK // dec_seq
    ws_s = jnp.tile(w_s[:, start:start + dec_seq, start:start + dec_seq], (1, rep, rep))
    bs_s = jnp.tile(b_s[:, start:start + dec_seq], (1, rep))
    h1, vn_p, vn_s = _mix(z, og_p, og_s, hp, hs, row(ln_v_g), row(ln_v_b), w_s, ws_s, b_s[..., None],
                          bs_s[..., None], w_pa, w_pb, w_o,
                          batch=batch, dec_seq=dec_seq)

    pad_rows = ROUTER_ROWS - N_EXPERTS - N_GROUPS
    wr = jnp.concatenate([w_er.T, w_gr.T, jnp.zeros((pad_rows, d), F32)], axis=0)
    br = jnp.concatenate([b_er, b_gr, jnp.zeros((pad_rows,), F32)]).reshape(ROUTER_ROWS, 1).astype(F32)
    c, idx, gates, cnt = _router(h1, row(g_ffn), wr, br)

    counts = cnt[:, 0]
    padded = (counts + MOE_BLOCK - 1) // MOE_BLOCK * MOE_BLOCK
    pad_end = jnp.cumsum(padded).astype(jnp.int32)
    pad_start = pad_end - padded
    n_blocks = -(-(n * TOP_K) // MOE_BLOCK) + N_EXPERTS
    block_first = jnp.arange(n_blocks, dtype=jnp.int32) * MOE_BLOCK
    block_exp = jnp.minimum(jnp.sum(pad_end[None, :] <= block_first[:, None], axis=1), N_EXPERTS - 1).astype(jnp.int32)
    n_active = jnp.maximum(pad_end[-1:] // MOE_BLOCK, 1)
    hit = idx[0:TOP_K, :, None] == jnp.arange(N_EXPERTS, dtype=jnp.int32)
    dest = jnp.sum(jnp.where(hit, pad_start, 0), axis=-1) + idx[TOP_K:2 * TOP_K]

    xs = _dispatch(pad_start + counts, pad_end, dest, c, n_blocks * MOE_BLOCK)
    used = jnp.where(counts > 0, jnp.arange(N_EXPERTS, dtype=jnp.int32), N_EXPERTS)
    next_exp = jnp.concatenate([lax.cummin(used, axis=0, reverse=True)[1:],
                                jnp.full((1,), N_EXPERTS, jnp.int32)]).astype(jnp.int32)
    yb = _experts(block_exp, n_active, next_exp, xs, w_gate, w_up, w_down)
    yp, ys = _combine(dest, gates, h1, p, yb, row(g_ple), w_pg, w_pp,
                      row(g_final), n_prompt=n_prompt, final_norm=final_norm)
    return yp, ys, st_p, st_s, vn_p, vn_s


def kernel(x_prompt, x_sample, p_prompt, p_sample, state_hgrn, g_mix, w_in, lb_logits, g_head, w_pa, ln_v_g, ln_v_b, w_s, b_s, w_pb, w_o, g_ffn, w_gr, b_gr, w_er, b_er, w_gate, w_up, w_down, g_ple, w_pg, w_pp, g_final):
    batch, seq, d = x_prompt.shape
    dec_batch, dec_seq, _ = x_sample.shape
    depth = g_mix.shape[0]
    n_prompt, n_sample = batch * seq, dec_batch * dec_seq
    width = w_pb.shape[1]
    lbs = jnp.cumsum(jax.nn.softmax(lb_logits.astype(F32), axis=0), axis=0)
    hp, hs = x_prompt.reshape(n_prompt, d), x_sample.reshape(n_sample, d)
    keep = min((seq - 1) % MLP_CHUNK + 1, seq)
    keep_s = min((PAST_LEN % MLP_CHUNK + dec_seq - 1) % MLP_CHUNK + 1, dec_seq)
    sp, ss, vp, vs = [], [], [], []
    for i in range(depth):
        p = jnp.concatenate([p_prompt[i].reshape(n_prompt, -1), p_sample[i].reshape(n_sample, -1)], axis=0)
        hp, hs, st_p, st_s, vn_p, vn_s = _layer(
            hp, hs, p, state_hgrn[i], lbs[i], batch, seq, dec_batch, dec_seq,
            g_mix[i], w_in[i], g_head[i], w_pa[i], ln_v_g[i], ln_v_b[i], w_s[i], b_s[i], w_pb[i], w_o[i],
            g_ffn[i], w_gr[i], b_gr[i], w_er[i], b_er[i], w_gate[i], w_up[i], w_down[i],
            g_ple[i], w_pg[i], w_pp[i], g_final, i == depth - 1)
        sp.append(st_p.astype(x_prompt.dtype))
        ss.append(st_s.astype(state_hgrn.dtype))
        vn_last = vn_p.reshape(batch + 1, -1, width)[:batch]
        vp.append(vn_last[:, vn_last.shape[1] - keep:])
        vs.append(vn_s[:n_sample].reshape(dec_batch, dec_seq, width)[:, dec_seq - keep_s:])
    y_prompt = hp.reshape(batch, seq, d)
    y_sample = hs.reshape(dec_batch, dec_seq, d)
    return (y_prompt, y_sample, jnp.stack(sp), jnp.stack(ss), jnp.stack(vp), jnp.stack(vs))
```

```python
import functools
import math

import jax
import jax.numpy as jnp
from jax import lax
from jax.experimental import pallas as pl
from jax.experimental.pallas import tpu as pltpu

F32 = jnp.float32
BF16 = jnp.bfloat16

EPS = 1e-6
HEAD_DIM = 128
REC_CHUNK = 64
REC_HALF = 32
REC_CHUNKS_PER_STEP = 4
MLP_CHUNK = 128
MLP_GROUPS = 4
N_GROUPS = 4
E_PER_GROUP = 8
N_EXPERTS = N_GROUPS * E_PER_GROUP
TOP_K = 2
MOE_BLOCK = 256
PAST_LEN = 2048
LANES = 128
SUBLANES = 8
ROUTER_ROWS = 48
VMEM_LIMIT = 56 * 1024 * 1024
LOG2_E = 1.4426950408889634
WEIGHT_STAGE_ROWS = 256


def _pick(n, candidates):
    for c in candidates:
        if n % c == 0:
            return c
    raise ValueError(f"no tile in {candidates} divides {n}")


def _dot(a, b):
    return jnp.dot(a, b, preferred_element_type=F32)


def _dot_nt(a, b, precision=None):
    return lax.dot_general(a, b, (((1,), (1,)), ((), ())), precision=precision, preferred_element_type=F32)


def _dot_tn(a, b):
    return lax.dot_general(a, b, (((0,), (0,)), ((), ())), preferred_element_type=F32)


def _rms(x, g):
    return x * lax.rsqrt(jnp.mean(x * x, axis=-1, keepdims=True) + EPS) * g


def _gelu(x):
    c = math.sqrt(2.0 / math.pi)
    return x * (0.5 * (1.0 + jnp.tanh(c * (x + 0.044715 * (x * x * x)))))


def _sigmoid(x):
    return 0.5 * jnp.tanh(0.5 * x) + 0.5


def _store_chunks(ref, x):
    for c in range(ref.shape[0]):
        ref[c] = x[:, c * LANES:(c + 1) * LANES].astype(ref.dtype)


def _load_chunks(ref):
    return jnp.concatenate([ref[c] for c in range(ref.shape[0])], axis=1)


HIGH_HALF = 0xFFFF0000


def _pack_bf16_pairs(x):
    half = x.shape[1] // 2
    lo = jnp.right_shift(pltpu.bitcast(x[:, :half], jnp.uint32), jnp.uint32(16))
    hi = jnp.bitwise_and(pltpu.bitcast(x[:, half:], jnp.uint32), jnp.uint32(HIGH_HALF))
    return jnp.bitwise_or(lo, hi)


def _unpack_bf16_pairs(p):
    lo = pltpu.bitcast(jnp.left_shift(p, jnp.uint32(16)), F32)
    hi = pltpu.bitcast(jnp.bitwise_and(p, jnp.uint32(HIGH_HALF)), F32)
    return jnp.concatenate([lo, hi], axis=1).astype(BF16)


def _row_copy(src_ref, src_row, dst_ref, dst_row, sem):
    return pltpu.make_async_copy(src_ref.at[:, pl.ds(src_row, 1), :], dst_ref.at[:, pl.ds(dst_row, 1), :], sem)


def _rows_copy(src_ref, dst_ref, dst_row, n_rows, sem):
    return pltpu.make_async_copy(src_ref, dst_ref.at[:, pl.ds(dst_row, n_rows), :], sem)


def _load_as_bf16(w_hbm, w_vmem, stage_ref, sem):
    half = stage_ref.shape[0] // 2
    rows, cols = min(half, w_hbm.shape[0]), min(stage_ref.shape[1], w_hbm.shape[1])
    assert w_hbm.shape[0] % rows == 0 and w_hbm.shape[1] % cols == 0
    chunks = [(r, c) for r in range(0, w_hbm.shape[0], rows) for c in range(0, w_hbm.shape[1], cols)]

    def slot(i):
        return stage_ref.at[pl.ds((i % 2) * half, rows), pl.ds(0, cols)]

    def copy(i):
        r, c = chunks[i]
        return pltpu.make_async_copy(w_hbm.at[pl.ds(r, rows), pl.ds(c, cols)], slot(i), sem.at[i % 2])

    copy(0).start()
    for i, (r, c) in enumerate(chunks):
        if i + 1 < len(chunks):
            copy(i + 1).start()
        copy(i).wait()
        w_vmem[pl.ds(r, rows), pl.ds(c, cols)] = slot(i)[...].astype(BF16)


def _in_proj_body(xp_ref, xs_ref, g_ref, w_ref, z_ref, xn_ref, *, n_prompt_tiles):
    first = pl.program_id(1) == 0
    is_p = pl.program_id(0) < n_prompt_tiles

    @pl.when(first & is_p)
    def _():
        xn_ref[...] = _rms(xp_ref[...], g_ref[...]).astype(BF16)

    @pl.when(first & jnp.logical_not(is_p))
    def _():
        xn_ref[...] = _rms(xs_ref[...], g_ref[...]).astype(BF16)

    z_ref[...] = _dot(xn_ref[...], w_ref[...].astype(BF16)).astype(z_ref.dtype)


def _in_proj(xp, xs, g, w):
    (n_p, d), n_s = xp.shape, xs.shape[0]
    width = w.shape[1]
    tm = _pick(math.gcd(n_p, n_s), (1024, 512, 256))
    tn = _pick(width, (1024, 512, 256, 128))
    npt, nst = n_p // tm, n_s // tm
    return pl.pallas_call(
        functools.partial(_in_proj_body, n_prompt_tiles=npt),
        grid=(npt + nst, width // tn),
        in_specs=[pl.BlockSpec((tm, d), lambda i, j: (jnp.minimum(i, npt - 1), 0)),
                  pl.BlockSpec((tm, d), lambda i, j: (jnp.clip(i - npt, 0, nst - 1), 0),
                               pipeline_mode=pl.Buffered(1)),
                  pl.BlockSpec((1, d), lambda i, j: (0, 0)),
                  pl.BlockSpec((d, tn), lambda i, j: (0, j))],
        out_specs=pl.BlockSpec((tm, tn), lambda i, j: (i, j)),
        out_shape=jax.ShapeDtypeStruct((n_p + n_s, width), BF16),
        scratch_shapes=[pltpu.VMEM((tm, d), BF16)],
        compiler_params=pltpu.CompilerParams(dimension_semantics=("parallel", "arbitrary"),
                                             vmem_limit_bytes=VMEM_LIMIT),
        name="in_proj",
    )(xp, xs, g, w)


def _cumsum_rows(x, tril3):
    hi = x.astype(BF16)
    r1 = x - hi.astype(F32)
    mid = r1.astype(BF16)
    lo = (r1 - mid.astype(F32)).astype(BF16)
    return _dot(tril3, jnp.concatenate([hi, mid, lo], axis=0))


def _hgrn_scratch(chunk, heads, streams):
    half = min(REC_HALF, chunk)
    wblk = heads * HEAD_DIM
    per_chunk = ([pltpu.VMEM((chunk, wblk), BF16),
                  pltpu.VMEM((chunk, wblk), BF16),
                  pltpu.VMEM((chunk, wblk), BF16),
                  pltpu.VMEM((SUBLANES, wblk), F32),
                  pltpu.VMEM((heads, chunk, chunk), BF16),
                  pltpu.VMEM((chunk, wblk), F32)]
                 + [pltpu.VMEM((half * (g + 1), wblk), BF16) for g in range(chunk // half)])
    return [pltpu.VMEM((streams * heads, HEAD_DIM, HEAD_DIM), F32)] + per_chunk * REC_CHUNKS_PER_STEP


def _hgrn_passes(q_ref, f_ref, i_ref, zo_ref, lb_ref, gh_ref, og_ref, scratch, rows, og_rows, chunk, heads,
                 reset=None):
    st_ref, qs_ref, qe_ref, kd_ref, dec_ref, a_ref, o_ref = scratch[:7]
    key_refs = scratch[7:]
    half = min(REC_HALF, chunk)
    n_half = chunk // half
    tril = (lax.broadcasted_iota(jnp.int32, (chunk, chunk), 0)
            >= lax.broadcasted_iota(jnp.int32, (chunk, chunk), 1))
    tril = jnp.where(tril, 1.0, 0.0).astype(BF16)
    tril3 = jnp.concatenate([tril, tril, tril], axis=1)
    masks = []
    for g in range(n_half):
        r = lax.broadcasted_iota(jnp.int32, (half, half * (g + 1)), 0)
        c = lax.broadcasted_iota(jnp.int32, (half, half * (g + 1)), 1)
        masks.append(r + g * half >= c)
    head_slices = [slice(h * HEAD_DIM, (h + 1) * HEAD_DIM) for h in range(heads)]

    def operands():
        for hs in head_slices:
            q = q_ref[rows, hs].astype(F32)
            lb = lb_ref[:, hs]
            c1 = 0.5 * (1.0 - lb)
            f = (lb + c1) + c1 * jnp.tanh(0.5 * f_ref[rows, hs].astype(F32))
            kk = 1.0 - f
            b = _cumsum_rows(jnp.log(f), tril3) * LOG2_E
            b_last = b[chunk - 1:chunk, :]
            dec_ref[0:1, hs] = jnp.exp2(b_last)
            mids, ks = [], []
            for g in range(n_half):
                rg = slice(g * half, (g + 1) * half)
                mid = b[g * half + half // 2 - 1:g * half + half // 2, :]
                qs_g = q[rg] * jnp.exp2(b[rg] - mid)
                ks_g = kk[rg] * jnp.exp2(mid - b[rg])
                mids.append(mid)
                ks.append(ks_g)
                qs_ref[rg, hs] = qs_g.astype(BF16)
                qe_ref[rg, hs] = (qs_g * jnp.exp2(mid)).astype(BF16)
                kd_ref[rg, hs] = (ks_g * jnp.exp2(b_last - mid)).astype(BF16)
                for gp in range(g):
                    key_refs[g][gp * half:(gp + 1) * half, hs] = (
                        ks[gp] * jnp.exp2(mid - mids[gp])).astype(BF16)
                key_refs[g][rg, hs] = ks_g.astype(BF16)

    def scores():
        for h, hs in enumerate(head_slices):
            for g in range(n_half):
                rg = slice(g * half, (g + 1) * half)
                a = _dot_nt(qs_ref[rg, hs], key_refs[g][:, hs])
                a_ref[h, rg, 0:half * (g + 1)] = jnp.where(masks[g], a, 0.0).astype(BF16)

    def outputs_and_state():
        for h, hs in enumerate(head_slices):
            v = i_ref[rows, hs]
            st = st_ref[h]
            if reset is not None:
                st = jnp.where(reset, 0.0, st)
            o_inter = _dot_nt(qe_ref[:, hs], st.astype(BF16))
            for g in range(n_half):
                rg = slice(g * half, (g + 1) * half)
                o_ref[rg, hs] = o_inter[rg] + _dot(a_ref[h, rg, 0:half * (g + 1)], v[0:half * (g + 1)])
            st_ref[h] = st * dec_ref[0:1, hs] + _dot_tn(v, kd_ref[:, hs])

    def normalise():
        for hs in head_slices:
            o = o_ref[:, hs]
            o_n = o * lax.rsqrt(jnp.mean(o * o, axis=-1, keepdims=True) + EPS) * (0.5 * gh_ref[:, hs])
            gate2 = jnp.tanh(0.5 * zo_ref[rows, hs].astype(F32)) + 1.0
            og_ref[og_rows, hs] = (o_n * gate2).astype(og_ref.dtype)

    return [operands, scores, outputs_and_state, normalise]


def _hgrn_body(q_ref, f_ref, i_ref, zo_ref, s0_ref, lb_ref, gh_ref, og_ref, sout_ref, *scratch,
               chunk, n_chunks, heads, streams):
    ti = pl.program_id(2)
    st_ref = scratch[0]

    @pl.when(ti == 0)
    def _():
        for k in range(streams):
            for h in range(heads):
                st_ref[k * heads + h] = s0_ref[k, h].T

    n_set = (len(scratch) - 1) // REC_CHUNKS_PER_STEP
    staging = [tuple(scratch[1 + k * n_set:1 + (k + 1) * n_set]) for k in range(REC_CHUNKS_PER_STEP)]

    def run_alternated(passes):
        for stage in zip(*passes):
            for run_pass in stage:
                run_pass()

    def passes_for(state_ref, rows, k):
        return _hgrn_passes(q_ref, f_ref, i_ref, zo_ref, lb_ref, gh_ref, og_ref, (state_ref,) + staging[k],
                            rows, rows, chunk, heads)

    if streams > 1:
        assert n_chunks == 1 and streams <= REC_CHUNKS_PER_STEP
        run_alternated([passes_for(st_ref.at[pl.ds(k * heads, heads)], slice(k * chunk, (k + 1) * chunk), k)
                        for k in range(streams)])
    else:
        per_step = math.gcd(n_chunks, REC_CHUNKS_PER_STEP)

        def chunk_step(ci, carry):
            run_alternated([passes_for(st_ref, pl.ds(pl.multiple_of((ci * per_step + k) * chunk, chunk), chunk), k)
                            for k in range(per_step)])
            return carry

        lax.fori_loop(0, n_chunks // per_step, chunk_step, 0)

    @pl.when(ti == pl.num_programs(2) - 1)
    def _():
        for k in range(streams):
            for h in range(heads):
                sout_ref[k, h] = st_ref[k * heads + h].T


def _hgrn(z, s0, lb, g_head, *, row_base, batch, seq, d_model, name):
    n_heads = d_model // HEAD_DIM
    heads = min(16, n_heads)
    chunk = min(REC_CHUNK, seq)
    half = min(REC_HALF, chunk)
    assert seq % chunk == 0 and chunk % half == 0
    rt = max(_pick(seq, (512, 256, 128, 64, 32)), chunk)
    tiles = seq // rt
    streams = math.gcd(batch, REC_CHUNKS_PER_STEP) if (tiles == 1 and rt == chunk) else 1
    blk_rows = streams * rt
    assert row_base % blk_rows == 0
    wblk = heads * HEAD_DIM
    cpb = d_model // wblk

    def zspec(section):
        return pl.BlockSpec((blk_rows, wblk),
                            lambda b, hg, i: (row_base // blk_rows + b * tiles + i, section * cpb + hg))

    return pl.pallas_call(
        functools.partial(_hgrn_body, chunk=chunk, n_chunks=rt // chunk, heads=heads, streams=streams),
        grid=(batch // streams, n_heads // heads, tiles),
        in_specs=[zspec(0), zspec(1), zspec(2), zspec(3),
                  pl.BlockSpec((streams, heads, HEAD_DIM, HEAD_DIM), lambda b, hg, i: (b, hg, 0, 0)),
                  pl.BlockSpec((1, wblk), lambda b, hg, i: (0, hg)),
                  pl.BlockSpec((1, wblk), lambda b, hg, i: (0, hg))],
        out_specs=[pl.BlockSpec((blk_rows, wblk), lambda b, hg, i: (b * tiles + i, hg)),
                   pl.BlockSpec((streams, heads, HEAD_DIM, HEAD_DIM), lambda b, hg, i: (b, hg, 0, 0))],
        out_shape=[jax.ShapeDtypeStruct((batch * seq, d_model), BF16),
                   jax.ShapeDtypeStruct((batch, n_heads, HEAD_DIM, HEAD_DIM), F32)],
        scratch_shapes=_hgrn_scratch(chunk, heads, streams),
        compiler_params=pltpu.CompilerParams(dimension_semantics=("parallel", "parallel", "arbitrary"),
                                             vmem_limit_bytes=VMEM_LIMIT),
        name=name,
    )(z, z, z, z, s0, lb, g_head)


def _mix_body(zu_ref, zv_ref, zga_ref, zgb_ref, ogp_ref, ogs_ref, xp_ref, xs_ref, lng_ref, lnb_ref,
              wsp_ref, wss_ref, bsp_ref, bss_ref, wpa_hbm, wpb_hbm, wo_hbm,
              h1_ref, vnp_ref, vns_ref, sg_ref, wpa_ref, wpb_ref, wo_ref, stage_ref, wsem, ya_ref,
              *, n_prompt_tiles, dec_seq):
    @pl.when(pl.program_id(0) == 0)
    def _():
        _load_as_bf16(wpa_hbm, wpa_ref, stage_ref, wsem)
        _load_as_bf16(wpb_hbm, wpb_ref, stage_ref, wsem)
        _load_as_bf16(wo_hbm, wo_ref, stage_ref, wsem)

    is_p = pl.program_id(0) < n_prompt_tiles
    tm, width = zu_ref.shape
    gd = width // MLP_GROUPS
    og = jnp.where(is_p, ogp_ref[...].astype(F32), ogs_ref[...].astype(F32)).astype(BF16)
    n_chunks = tm // MLP_CHUNK
    d = h1_ref.shape[1]
    col = d // (2 * n_chunks)

    def ya_slice(j):
        cs = slice(j * col, (j + 1) * col)
        ya_ref[:, cs] = _dot(og, wpa_ref[:, cs])

    r = lax.broadcasted_iota(jnp.int32, (MLP_CHUNK, MLP_CHUNK), 0)
    c = lax.broadcasted_iota(jnp.int32, (MLP_CHUNK, MLP_CHUNK), 1)
    causal = r >= c
    same_stream = (r // dec_seq) == (c // dec_seq)
    w_mix, bias = [], []
    for g in range(MLP_GROUPS):
        w_p = jnp.where(causal, wsp_ref[g], 0.0)
        w_s = jnp.where(causal & same_stream, wss_ref[g], 0.0)
        w_mix.append(jnp.where(is_p, w_p, w_s).astype(BF16))
        bias.append(jnp.where(is_p, bsp_ref[g], bss_ref[g]))
    for cc in range(n_chunks):
        rows = slice(cc * MLP_CHUNK, (cc + 1) * MLP_CHUNK)
        ya_slice(2 * cc)
        u = _gelu(zu_ref[rows, :].astype(F32))
        gv = _gelu(zv_ref[rows, :].astype(F32))
        xc = gv - jnp.mean(gv, axis=-1, keepdims=True)
        vn = xc * lax.rsqrt(jnp.mean(xc * xc, axis=-1, keepdims=True) + EPS) * lng_ref[...] + lnb_ref[...]
        vnp_ref[rows, :] = vn
        vns_ref[rows, :] = vn
        ya_slice(2 * cc + 1)
        vnb = vn.astype(BF16)
        for g in range(MLP_GROUPS):
            cols = slice(g * gd, (g + 1) * gd)
            s = _dot(w_mix[g], vnb[:, cols]) + bias[g]
            sg_ref[rows, cols] = (u[:, cols] * s).astype(BF16)
    y_b = _dot(sg_ref[...], wpb_ref[...])
    m = _sigmoid(zga_ref[...].astype(F32)) * ya_ref[...] + _sigmoid(zgb_ref[...].astype(F32)) * y_b
    x = jnp.where(is_p, xp_ref[...], xs_ref[...])
    h1_ref[...] = x + _dot(m.astype(BF16), wo_ref[...])


def _mix(z, og_p, og_s, xp, xs, ln_g, ln_b, ws_p, ws_s, bs_p, bs_s, w_pa, w_pb, w_o, *, batch, dec_seq):
    (n_p, d), n_s = xp.shape, xs.shape[0]
    n = n_p + n_s
    width = w_pb.shape[0]
    tm = 256
    assert n_s % tm == 0 and (n_p // batch) % tm == 0 and tm % MLP_CHUNK == 0 and MLP_CHUNK % dec_seq == 0
    npt, nst = n_p // tm, n_s // tm
    tpb = npt // batch
    u_blk = 4 * d // width
    ga_blk = (4 * d + 2 * width) // d
    const2 = lambda i: (0, 0)
    const3 = lambda i: (0, 0, 0)
    p_map = lambda i: (jnp.minimum(i, npt - 1), 0)
    s_map = lambda i: (jnp.clip(i - npt, 0, nst - 1), 0)
    return pl.pallas_call(
        functools.partial(_mix_body, n_prompt_tiles=npt, dec_seq=dec_seq),
        grid=(n // tm,),
        in_specs=[pl.BlockSpec((tm, width), lambda i: (i, u_blk)),
                  pl.BlockSpec((tm, width), lambda i: (i, u_blk + 1)),
                  pl.BlockSpec((tm, d), lambda i: (i, ga_blk)),
                  pl.BlockSpec((tm, d), lambda i: (i, ga_blk + 1)),
                  pl.BlockSpec((tm, d), p_map),
                  pl.BlockSpec((tm, d), s_map),
                  pl.BlockSpec((tm, d), p_map),
                  pl.BlockSpec((tm, d), s_map),
                  pl.BlockSpec((1, width), const2),
                  pl.BlockSpec((1, width), const2),
                  pl.BlockSpec((MLP_GROUPS, MLP_CHUNK, MLP_CHUNK), const3),
                  pl.BlockSpec((MLP_GROUPS, MLP_CHUNK, MLP_CHUNK), const3),
                  pl.BlockSpec((MLP_GROUPS, MLP_CHUNK, 1), const3),
                  pl.BlockSpec((MLP_GROUPS, MLP_CHUNK, 1), const3),
                  pl.BlockSpec(memory_space=pl.ANY),
                  pl.BlockSpec(memory_space=pl.ANY),
                  pl.BlockSpec(memory_space=pl.ANY)],
        out_specs=[pl.BlockSpec((tm, d), lambda i: (i, 0)),
                   pl.BlockSpec((tm, width), lambda i: (jnp.where(i < npt, i // tpb, batch), 0)),
                   pl.BlockSpec((tm, width), lambda i: (jnp.where(i < npt, nst, i - npt), 0))],
        out_shape=[jax.ShapeDtypeStruct((n, d), F32),
                   jax.ShapeDtypeStruct(((batch + 1) * tm, width), F32),
                   jax.ShapeDtypeStruct(((nst + 1) * tm, width), F32)],
        scratch_shapes=[pltpu.VMEM((tm, width), BF16),
                        pltpu.VMEM((d, d), BF16), pltpu.VMEM((width, d), BF16), pltpu.VMEM((d, d), BF16),
                        pltpu.VMEM((WEIGHT_STAGE_ROWS, d), F32), pltpu.SemaphoreType.DMA((2,)),
                        pltpu.VMEM((tm, d), F32)],
        compiler_params=pltpu.CompilerParams(dimension_semantics=("arbitrary",),
                                             vmem_limit_bytes=VMEM_LIMIT),
        name="mix",
    )(z, z, z, z, og_p, og_s, xp, xs, ln_g, ln_b, ws_p, ws_s, bs_p, bs_s, w_pa, w_pb, w_o)


def _router_body(h_ref, g_ref, wr_ref, br_ref, c_ref, idx_ref, gate_ref, cnt_ref, carry_ref):
    @pl.when(pl.program_id(0) == 0)
    def _():
        carry_ref[...] = jnp.zeros_like(carry_ref)

    tm = h_ref.shape[0]
    c = _rms(h_ref[...], g_ref[...])
    wr = wr_ref[...]
    wr_hi = wr.astype(BF16)
    wr_lo = (wr - wr_hi.astype(F32)).astype(BF16)
    c_hi = c.astype(BF16)
    c_lo = (c - c_hi.astype(F32)).astype(BF16)
    _store_chunks(c_ref, _pack_bf16_pairs(c_hi.astype(F32)))
    both = _dot_nt(jnp.concatenate([wr_hi, wr_lo], axis=0), c_hi)
    lt = both[0:ROUTER_ROWS] + both[ROUTER_ROWS:2 * ROUTER_ROWS] + _dot_nt(wr_hi, c_lo) + br_ref[...]
    le = lt[0:N_EXPERTS]
    lg = lt[N_EXPERTS:N_EXPERTS + N_GROUPS]
    gmax = jnp.max(lg, axis=0, keepdims=True)
    p_sel = 1.0 / jnp.sum(jnp.exp(lg - gmax), axis=0, keepdims=True)
    best = lg[0:1]
    gi = jnp.zeros((1, tm), jnp.int32)
    for g in range(1, N_GROUPS):
        better = lg[g:g + 1] > best
        gi = jnp.where(better, g, gi)
        best = jnp.where(better, lg[g:g + 1], best)
    leg = jnp.zeros((E_PER_GROUP, tm), F32)
    for g in range(N_GROUPS):
        leg = jnp.where(gi == g, le[g * E_PER_GROUP:(g + 1) * E_PER_GROUP], leg)
    sub = lax.broadcasted_iota(jnp.int32, (E_PER_GROUP, tm), 0).astype(F32)
    v1 = jnp.max(leg, axis=0, keepdims=True)
    i1 = jnp.min(jnp.where(leg == v1, sub, float(E_PER_GROUP)), axis=0, keepdims=True)
    rest = jnp.where(sub == i1, -jnp.inf, leg)
    v2 = jnp.max(rest, axis=0, keepdims=True)
    i2 = jnp.min(jnp.where(rest == v2, sub, float(E_PER_GROUP)), axis=0, keepdims=True)
    e2 = jnp.exp(v2 - v1)
    den = 1.0 + e2
    gate0 = p_sel * (1.0 / den)
    gate1 = p_sel * (e2 / den)
    ex0 = gi * E_PER_GROUP + i1.astype(jnp.int32)
    ex1 = gi * E_PER_GROUP + i2.astype(jnp.int32)
    eid = lax.broadcasted_iota(jnp.int32, (N_EXPERTS, tm), 0)
    oh0 = eid == ex0
    oh1 = eid == ex1
    oh = jnp.where(oh0 | oh1, 1.0, 0.0)
    upper = jnp.where(lax.broadcasted_iota(jnp.int32, (tm, tm), 0) < lax.broadcasted_iota(jnp.int32, (tm, tm), 1),
                      1.0, 0.0).astype(BF16)
    before = _dot(oh.astype(BF16), upper) + carry_ref[:, 0:1]
    rank0 = jnp.sum(jnp.where(oh0, before, 0.0), axis=0, keepdims=True)
    rank1 = jnp.sum(jnp.where(oh1, before, 0.0), axis=0, keepdims=True)
    carry = carry_ref[...] + jnp.sum(oh, axis=1, keepdims=True)
    carry_ref[...] = carry
    cnt_ref[...] = carry.astype(jnp.int32)
    idx_ref[...] = jnp.zeros_like(idx_ref)
    idx_ref[0:1, :] = ex0
    idx_ref[1:2, :] = ex1
    idx_ref[2:3, :] = rank0.astype(jnp.int32)
    idx_ref[3:4, :] = rank1.astype(jnp.int32)
    gate_ref[...] = jnp.zeros_like(gate_ref)
    gate_ref[0:1, :] = gate0
    gate_ref[1:2, :] = gate1


def _router(h1, g_ffn, wr, br):
    n, d = h1.shape
    tm = _pick(n, (512, 256))
    return pl.pallas_call(
        _router_body,
        grid=(n // tm,),
        in_specs=[pl.BlockSpec((tm, d), lambda i: (i, 0)),
                  pl.BlockSpec((1, d), lambda i: (0, 0)),
                  pl.BlockSpec((ROUTER_ROWS, d), lambda i: (0, 0)),
                  pl.BlockSpec((ROUTER_ROWS, 1), lambda i: (0, 0))],
        out_specs=[pl.BlockSpec((d // (2 * LANES), tm, LANES), lambda i: (0, i, 0)),
                   pl.BlockSpec((SUBLANES, tm), lambda i: (0, i)),
                   pl.BlockSpec((SUBLANES, tm), lambda i: (0, i)),
                   pl.BlockSpec((N_EXPERTS, LANES), lambda i: (0, 0))],
        out_shape=[jax.ShapeDtypeStruct((d // (2 * LANES), n, LANES), jnp.uint32),
                   jax.ShapeDtypeStruct((SUBLANES, n), jnp.int32),
                   jax.ShapeDtypeStruct((SUBLANES, n), F32),
                   jax.ShapeDtypeStruct((N_EXPERTS, LANES), jnp.int32)],
        scratch_shapes=[pltpu.VMEM((N_EXPERTS, LANES), F32)],
        compiler_params=pltpu.CompilerParams(dimension_semantics=("arbitrary",),
                                             vmem_limit_bytes=VMEM_LIMIT),
        name="router",
    )(h1, g_ffn, wr, br)


def _dispatch_body(fill_lo_ref, fill_hi_ref, dest_ref, c_ref, xs_ref, zero_ref, sem, zsem):
    td = c_ref.shape[1]

    @pl.when(pl.program_id(0) == 0)
    def _():
        zero_ref[...] = jnp.zeros_like(zero_ref)

        def per_expert(e, carry):
            lo, hi = fill_lo_ref[e], fill_hi_ref[e]

            def start(r, c2):
                _row_copy(zero_ref, 0, xs_ref, r, zsem).start()
                return c2

            def wait(r, c2):
                _row_copy(zero_ref, 0, xs_ref, 0, zsem).wait()
                return c2

            lax.fori_loop(lo, hi, start, 0)
            lax.fori_loop(lo, hi, wait, 0)
            return carry

        lax.fori_loop(0, N_EXPERTS, per_expert, 0)

        def block_copy(j):
            return _rows_copy(zero_ref, xs_ref, j * MOE_BLOCK, MOE_BLOCK, zsem)

        first_unused = fill_hi_ref[N_EXPERTS - 1] // MOE_BLOCK
        n_blocks = xs_ref.shape[1] // MOE_BLOCK

        def start_block(j, carry):
            block_copy(j).start()
            return carry

        def wait_block(j, carry):
            block_copy(0).wait()
            return carry

        lax.fori_loop(first_unused, n_blocks, start_block, 0)
        lax.fori_loop(first_unused, n_blocks, wait_block, 0)

    for t in range(td):
        for k in range(TOP_K):
            _row_copy(c_ref, t, xs_ref, dest_ref[k, t], sem).start(priority=k % 2)
    for k in range(TOP_K):
        _rows_copy(c_ref, xs_ref, 0, td, sem).wait()


def _dispatch(fill_lo, fill_hi, dest, c, n_slots):
    nc, n, _ = c.shape
    td = _pick(n, (512, 256))
    grid_spec = pltpu.PrefetchScalarGridSpec(
        num_scalar_prefetch=2,
        grid=(n // td,),
        in_specs=[pl.BlockSpec((TOP_K, td), lambda i, lo, hi: (0, i), memory_space=pltpu.SMEM),
                  pl.BlockSpec((nc, td, LANES), lambda i, lo, hi: (0, i, 0))],
        out_specs=pl.BlockSpec(memory_space=pl.ANY),
        scratch_shapes=[pltpu.VMEM((nc, MOE_BLOCK, LANES), c.dtype),
                        pltpu.SemaphoreType.DMA(()), pltpu.SemaphoreType.DMA(())],
    )
    return pl.pallas_call(
        _dispatch_body,
        grid_spec=grid_spec,
        out_shape=jax.ShapeDtypeStruct((nc, n_slots, LANES), c.dtype),
        compiler_params=pltpu.CompilerParams(dimension_semantics=("arbitrary",),
                                             vmem_limit_bytes=VMEM_LIMIT),
        name="dispatch",
    )(fill_lo, fill_hi, dest, c)


def _expert_body(be_ref, na_ref, nxt_ref, xs_ref, wg_hbm, wu_hbm, wd_hbm, yb_ref,
                 wg_ref, wu_ref, wd_ref, sg_ref, su_ref, sd_ref, wsem):
    j = pl.program_id(0)
    active = j < na_ref[0]
    e = be_ref[j]
    first = (j == 0) | (be_ref[jnp.maximum(j - 1, 0)] != e)

    def weight_copies(expert):
        return (pltpu.make_async_copy(wg_hbm.at[expert], sg_ref, wsem.at[0]),
                pltpu.make_async_copy(wu_hbm.at[expert], su_ref, wsem.at[1]),
                pltpu.make_async_copy(wd_hbm.at[expert], sd_ref, wsem.at[2]))

    @pl.when(j == 0)
    def _():
        for cp in weight_copies(e):
            cp.start()

    @pl.when(active & first)
    def _():
        for cp, stage, dst in zip(weight_copies(e), (sg_ref, su_ref, sd_ref), (wg_ref, wu_ref, wd_ref)):
            cp.wait()

            def convert(r, carry, stage=stage, dst=dst):
                rows = pl.ds(pl.multiple_of(r * LANES, LANES), LANES)
                dst[rows, :] = stage[rows, :].astype(BF16)
                return carry

            lax.fori_loop(0, stage.shape[0] // LANES, convert, 0)

        @pl.when(nxt_ref[e] < N_EXPERTS)
        def _():
            for cp in weight_copies(nxt_ref[e]):
                cp.start()

    @pl.when(active)
    def _():
        x = _unpack_bf16_pairs(_load_chunks(xs_ref))
        a = _dot(x, wg_ref[...])
        b = _dot(x, wu_ref[...])
        hid = (a * _sigmoid(a)) * b
        _store_chunks(yb_ref, _dot(hid.astype(BF16), wd_ref[...]))

    @pl.when(jnp.logical_not(active))
    def _():
        yb_ref[...] = jnp.zeros_like(yb_ref)


def _experts(block_exp, n_active, next_exp, xs, w_gate, w_up, w_down):
    nc_in, n_slots, _ = xs.shape
    d, de = w_gate.shape[1:]
    nc = d // LANES
    nb = n_slots // MOE_BLOCK
    assert d % LANES == 0 and de % LANES == 0
    blk = lambda j, be, na, nxt: (0, jnp.minimum(j, na[0] - 1), 0)
    grid_spec = pltpu.PrefetchScalarGridSpec(
        num_scalar_prefetch=3,
        grid=(nb,),
        in_specs=[pl.BlockSpec((nc_in, MOE_BLOCK, LANES), blk),
                  pl.BlockSpec(memory_space=pl.ANY),
                  pl.BlockSpec(memory_space=pl.ANY),
                  pl.BlockSpec(memory_space=pl.ANY)],
        out_specs=pl.BlockSpec((nc, MOE_BLOCK, LANES), lambda j, be, na, nxt: (0, j, 0)),
        scratch_shapes=[pltpu.VMEM((d, de), BF16), pltpu.VMEM((d, de), BF16), pltpu.VMEM((de, d), BF16),
                        pltpu.VMEM((d, de), F32), pltpu.VMEM((d, de), F32), pltpu.VMEM((de, d), F32),
                        pltpu.SemaphoreType.DMA((3,))],
    )
    return pl.pallas_call(
        _expert_body,
        grid_spec=grid_spec,
        out_shape=jax.ShapeDtypeStruct((nc, n_slots, LANES), F32),
        compiler_params=pltpu.CompilerParams(dimension_semantics=("arbitrary",),
                                             vmem_limit_bytes=VMEM_LIMIT),
        name="experts",
    )(block_exp, n_active, next_exp, xs, w_gate, w_up, w_down)


def _combine_body(dcur_ref, dnxt_ref, gate_ref, h1_ref, p_ref, yb_ref, gple_ref, wpg_hbm, wpp_hbm, gfin_ref,
                  yp_ref, ys_ref, buf_ref, sem, wpg_ref, wpp_ref, stage_ref, wsem,
                  *, n_prompt_tiles, final_norm):
    i = pl.program_id(0)

    @pl.when(i == 0)
    def _():
        _load_as_bf16(wpg_hbm, wpg_ref, stage_ref, wsem)
        _load_as_bf16(wpp_hbm, wpp_ref, stage_ref, wsem)

    n_tiles = pl.num_programs(0)
    tc = h1_ref.shape[0]
    slot = lax.rem(i, 2)

    @pl.when(i == 0)
    def _():
        def body(t, carry):
            for k in range(TOP_K):
                _row_copy(yb_ref, dcur_ref[k, t], buf_ref.at[0, k], t, sem.at[0]).start()
            return carry

        lax.fori_loop(0, tc, body, 0)

    def wait_slot(s):
        for k in range(TOP_K):
            pltpu.make_async_copy(yb_ref.at[:, pl.ds(0, tc), :], buf_ref.at[s, k], sem.at[s]).wait()

    wait_slot(slot)
    gt = gate_ref[...].T
    h2 = h1_ref[...] + (gt[:, 0:1] * _load_chunks(buf_ref.at[slot, 0])
                        + gt[:, 1:2] * _load_chunks(buf_ref.at[slot, 1]))
    for t in range(tc):
        for k in range(TOP_K):
            _row_copy(yb_ref, dnxt_ref[k, t], buf_ref.at[1 - slot, k], t, sem.at[1 - slot]).start(priority=k % 2)
    a = _rms(h2, gple_ref[...]).astype(BF16)
    h3 = h2 + _sigmoid(_dot(a, wpg_ref[...])) * _dot(p_ref[...].astype(BF16), wpp_ref[...])
    if final_norm:
        h3 = _rms(h3, gfin_ref[...])

    @pl.when(i < n_prompt_tiles)
    def _():
        yp_ref[...] = h3

    @pl.when(i >= n_prompt_tiles)
    def _():
        ys_ref[...] = h3

    @pl.when(i == n_tiles - 1)
    def _():
        wait_slot(1 - slot)


def _combine(dest, gates, h1, p, yb, g_ple, w_pg, w_pp, g_final, *, n_prompt, final_norm):
    n, d = h1.shape
    ple = p.shape[1]
    tc = 256
    assert n_prompt % tc == 0 and (n - n_prompt) % tc == 0
    nt, npt = n // tc, n_prompt // tc
    nst = nt - npt
    const2 = lambda i: (0, 0)
    return pl.pallas_call(
        functools.partial(_combine_body, n_prompt_tiles=npt, final_norm=final_norm),
        grid=(nt,),
        in_specs=[pl.BlockSpec((TOP_K, tc), lambda i: (0, i), memory_space=pltpu.SMEM),
                  pl.BlockSpec((TOP_K, tc), lambda i: (0, jnp.minimum(i + 1, nt - 1)), memory_space=pltpu.SMEM),
                  pl.BlockSpec((SUBLANES, tc), lambda i: (0, i)),
                  pl.BlockSpec((tc, d), lambda i: (i, 0)),
                  pl.BlockSpec((tc, ple), lambda i: (i, 0)),
                  pl.BlockSpec(memory_space=pl.ANY),
                  pl.BlockSpec((1, d), const2),
                  pl.BlockSpec(memory_space=pl.ANY),
                  pl.BlockSpec(memory_space=pl.ANY),
                  pl.BlockSpec((1, d), const2)],
        out_specs=[pl.BlockSpec((tc, d), lambda i: (jnp.minimum(i, npt - 1), 0)),
                   pl.BlockSpec((tc, d), lambda i: (jnp.clip(i - npt, 0, nst - 1), 0))],
        out_shape=[jax.ShapeDtypeStruct((n_prompt, d), F32),
                   jax.ShapeDtypeStruct((n - n_prompt, d), F32)],
        scratch_shapes=[pltpu.VMEM((2, TOP_K, d // LANES, tc, LANES), F32), pltpu.SemaphoreType.DMA((2,)),
                        pltpu.VMEM((d, d), BF16), pltpu.VMEM((ple, d), BF16),
                        pltpu.VMEM((WEIGHT_STAGE_ROWS, d), F32), pltpu.SemaphoreType.DMA((2,))],
        compiler_params=pltpu.CompilerParams(dimension_semantics=("arbitrary",),
                                             vmem_limit_bytes=VMEM_LIMIT),
        name="combine",
    )(dest, dest, gates, h1, p, yb, g_ple, w_pg, w_pp, g_final)


def _layer(hp, hs, p, s0_sample, lb, batch, seq, dec_batch, dec_seq,
           g_mix, w_in, g_head, w_pa, ln_v_g, ln_v_b, w_s, b_s, w_pb, w_o,
           g_ffn, w_gr, b_gr, w_er, b_er, w_gate, w_up, w_down, g_ple, w_pg, w_pp, g_final, final_norm):
    n_prompt, d = hp.shape
    n = n_prompt + hs.shape[0]
    n_heads = d // HEAD_DIM
    row = lambda a: a.reshape(1, -1).astype(F32)

    z = _in_proj(hp, hs, row(g_mix), w_in)

    lb_row, gh_row = row(lb), row(g_head)
    s0_prompt = jnp.zeros((batch, n_heads, HEAD_DIM, HEAD_DIM), F32)
    og_p, st_p = _hgrn(z, s0_prompt, lb_row, gh_row, row_base=0, batch=batch, seq=seq,
                       d_model=d, name="hgrn_prompt")
    og_s, st_s = _hgrn(z, s0_sample.astype(F32), lb_row, gh_row, row_base=n_prompt, batch=dec_batch,
                       seq=dec_seq, d_model=d, name="hgrn_sample")

    start = PAST_LEN % MLP_CHUNK
    assert start + dec_seq <= MLP_CHUNK
    rep = MLP_CHUNK // dec_seq
    ws_s = jnp.tile(w_s[:, start:start + dec_seq, start:start + dec_seq], (1, rep, rep))
    bs_s = jnp.tile(b_s[:, start:start + dec_seq], (1, rep))
    h1, vn_p, vn_s = _mix(z, og_p, og_s, hp, hs, row(ln_v_g), row(ln_v_b), w_s, ws_s, b_s[..., None],
                          bs_s[..., None], w_pa, w_pb, w_o,
                          batch=batch, dec_seq=dec_seq)

    pad_rows = ROUTER_ROWS - N_EXPERTS - N_GROUPS
    wr = jnp.concatenate([w_er.T, w_gr.T, jnp.zeros((pad_rows, d), F32)], axis=0)
    br = jnp.concatenate([b_er, b_gr, jnp.zeros((pad_rows,), F32)]).reshape(ROUTER_ROWS, 1).astype(F32)
    c, idx, gates, cnt = _router(h1, row(g_ffn), wr, br)

    counts = cnt[:, 0]
    padded = (counts + MOE_BLOCK - 1) // MOE_BLOCK * MOE_BLOCK
    pad_end = jnp.cumsum(padded).astype(jnp.int32)
    pad_start = pad_end - padded
    n_blocks = -(-(n * TOP_K) // MOE_BLOCK) + N_EXPERTS
    block_first = jnp.arange(n_blocks, dtype=jnp.int32) * MOE_BLOCK
    block_exp = jnp.minimum(jnp.sum(pad_end[None, :] <= block_first[:, None], axis=1), N_EXPERTS - 1).astype(jnp.int32)
    n_active = jnp.maximum(pad_end[-1:] // MOE_BLOCK, 1)
    hit = idx[0:TOP_K, :, None] == jnp.arange(N_EXPERTS, dtype=jnp.int32)
    dest = jnp.sum(jnp.where(hit, pad_start, 0), axis=-1) + idx[TOP_K:2 * TOP_K]

    xs = _dispatch(pad_start + counts, pad_end, dest, c, n_blocks * MOE_BLOCK)
    used = jnp.where(counts > 0, jnp.arange(N_EXPERTS, dtype=jnp.int32), N_EXPERTS)
    next_exp = jnp.concatenate([lax.cummin(used, axis=0, reverse=True)[1:],
                                jnp.full((1,), N_EXPERTS, jnp.int32)]).astype(jnp.int32)
    yb = _experts(block_exp, n_active, next_exp, xs, w_gate, w_up, w_down)
    yp, ys = _combine(dest, gates, h1, p, yb, row(g_ple), w_pg, w_pp,
                      row(g_final), n_prompt=n_prompt, final_norm=final_norm)
    return yp, ys, st_p, st_s, vn_p, vn_s


def kernel(x_prompt, x_sample, p_prompt, p_sample, state_hgrn, g_mix, w_in, lb_logits, g_head, w_pa, ln_v_g, ln_v_b, w_s, b_s, w_pb, w_o, g_ffn, w_gr, b_gr, w_er, b_er, w_gate, w_up, w_down, g_ple, w_pg, w_pp, g_final):
    batch, seq, d = x_prompt.shape
    dec_batch, dec_seq, _ = x_sample.shape
    depth = g_mix.shape[0]
    n_prompt, n_sample = batch * seq, dec_batch * dec_seq
    width = w_pb.shape[1]
    lbs = jnp.cumsum(jax.nn.softmax(lb_logits.astype(F32), axis=0), axis=0)
    hp, hs = x_prompt.reshape(n_prompt, d), x_sample.reshape(n_sample, d)
    keep = min((seq - 1) % MLP_CHUNK + 1, seq)
    keep_s = min((PAST_LEN % MLP_CHUNK + dec_seq - 1) % MLP_CHUNK + 1, dec_seq)
    sp, ss, vp, vs = [], [], [], []
    for i in range(depth):
        p = jnp.concatenate([p_prompt[i].reshape(n_prompt, -1), p_sample[i].reshape(n_sample, -1)], axis=0)
        hp, hs, st_p, st_s, vn_p, vn_s = _layer(
            hp, hs, p, state_hgrn[i], lbs[i], batch, seq, dec_batch, dec_seq,
            g_mix[i], w_in[i], g_head[i], w_pa[i], ln_v_g[i], ln_v_b[i], w_s[i], b_s[i], w_pb[i], w_o[i],
            g_ffn[i], w_gr[i], b_gr[i], w_er[i], b_er[i], w_gate[i], w_up[i], w_down[i],
            g_ple[i], w_pg[i], w_pp[i], g_final, i == depth - 1)
        sp.append(st_p.astype(x_prompt.dtype))
        ss.append(st_s.astype(state_hgrn.dtype))
        vn_last = vn_p.reshape(batch + 1, -1, width)[:batch]
        vp.append(vn_last[:, vn_last.shape[1] - keep:])
        vs.append(vn_s[:n_sample].reshape(dec_batch, dec_seq, width)[:, dec_seq - keep_s:])
    y_prompt = hp.reshape(batch, seq, d)
    y_sample = hs.reshape(dec_batch, dec_seq, d)
    return (y_prompt, y_sample, jnp.stack(sp), jnp.stack(ss), jnp.stack(vp), jnp.stack(vs))
```

```python
import functools
import math

import jax
import jax.numpy as jnp
from jax import lax
from jax.experimental import pallas as pl
from jax.experimental.pallas import tpu as pltpu

F32 = jnp.float32
BF16 = jnp.bfloat16

EPS = 1e-6
HEAD_DIM = 128
REC_CHUNK = 64
REC_HALF = 32
REC_CHUNKS_PER_STEP = 4
MLP_CHUNK = 128
MLP_GROUPS = 4
N_GROUPS = 4
E_PER_GROUP = 8
N_EXPERTS = N_GROUPS * E_PER_GROUP
TOP_K = 2
MOE_BLOCK = 256
PAST_LEN = 2048
LANES = 128
SUBLANES = 8
ROUTER_ROWS = 48
VMEM_LIMIT = 56 * 1024 * 1024
LOG2_E = 1.4426950408889634
WEIGHT_STAGE_ROWS = 256


def _pick(n, candidates):
    for c in candidates:
        if n % c == 0:
            return c
    raise ValueError(f"no tile in {candidates} divides {n}")


def _dot(a, b):
    return jnp.dot(a, b, preferred_element_type=F32)


def _dot_nt(a, b, precision=None):
    return lax.dot_general(a, b, (((1,), (1,)), ((), ())), precision=precision, preferred_element_type=F32)


def _dot_tn(a, b):
    return lax.dot_general(a, b, (((0,), (0,)), ((), ())), preferred_element_type=F32)


def _rms(x, g):
    return x * lax.rsqrt(jnp.mean(x * x, axis=-1, keepdims=True) + EPS) * g


def _gelu(x):
    c = math.sqrt(2.0 / math.pi)
    return x * (0.5 * (1.0 + jnp.tanh(c * (x + 0.044715 * (x * x * x)))))


def _sigmoid(x):
    return 0.5 * jnp.tanh(0.5 * x) + 0.5


def _store_chunks(ref, x):
    for c in range(ref.shape[0]):
        ref[c] = x[:, c * LANES:(c + 1) * LANES].astype(ref.dtype)


def _load_chunks(ref):
    return jnp.concatenate([ref[c] for c in range(ref.shape[0])], axis=1)


HIGH_HALF = 0xFFFF0000


def _pack_bf16_pairs(x):
    half = x.shape[1] // 2
    lo = jnp.right_shift(pltpu.bitcast(x[:, :half], jnp.uint32), jnp.uint32(16))
    hi = jnp.bitwise_and(pltpu.bitcast(x[:, half:], jnp.uint32), jnp.uint32(HIGH_HALF))
    return jnp.bitwise_or(lo, hi)


def _unpack_bf16_pairs(p):
    lo = pltpu.bitcast(jnp.left_shift(p, jnp.uint32(16)), F32)
    hi = pltpu.bitcast(jnp.bitwise_and(p, jnp.uint32(HIGH_HALF)), F32)
    return jnp.concatenate([lo, hi], axis=1).astype(BF16)


def _row_copy(src_ref, src_row, dst_ref, dst_row, sem):
    return pltpu.make_async_copy(src_ref.at[:, pl.ds(src_row, 1), :], dst_ref.at[:, pl.ds(dst_row, 1), :], sem)


def _rows_copy(src_ref, dst_ref, dst_row, n_rows, sem):
    return pltpu.make_async_copy(src_ref, dst_ref.at[:, pl.ds(dst_row, n_rows), :], sem)


def _load_as_bf16(w_hbm, w_vmem, stage_ref, sem):
    half = stage_ref.shape[0] // 2
    rows, cols = min(half, w_hbm.shape[0]), min(stage_ref.shape[1], w_hbm.shape[1])
    assert w_hbm.shape[0] % rows == 0 and w_hbm.shape[1] % cols == 0
    chunks = [(r, c) for r in range(0, w_hbm.shape[0], rows) for c in range(0, w_hbm.shape[1], cols)]

    def slot(i):
        return stage_ref.at[pl.ds((i % 2) * half, rows), pl.ds(0, cols)]

    def copy(i):
        r, c = chunks[i]
        return pltpu.make_async_copy(w_hbm.at[pl.ds(r, rows), pl.ds(c, cols)], slot(i), sem.at[i % 2])

    copy(0).start()
    for i, (r, c) in enumerate(chunks):
        if i + 1 < len(chunks):
            copy(i + 1).start()
        copy(i).wait()
        w_vmem[pl.ds(r, rows), pl.ds(c, cols)] = slot(i)[...].astype(BF16)


def _in_proj_body(xp_ref, xs_ref, g_ref, w_ref, z_ref, xn_ref, *, n_prompt_tiles):
    first = pl.program_id(1) == 0
    is_p = pl.program_id(0) < n_prompt_tiles

    @pl.when(first & is_p)
    def _():
        xn_ref[...] = _rms(xp_ref[...], g_ref[...]).astype(BF16)

    @pl.when(first & jnp.logical_not(is_p))
    def _():
        xn_ref[...] = _rms(xs_ref[...], g_ref[...]).astype(BF16)

    z_ref[...] = _dot(xn_ref[...], w_ref[...].astype(BF16)).astype(z_ref.dtype)


def _in_proj(xp, xs, g, w):
    (n_p, d), n_s = xp.shape, xs.shape[0]
    width = w.shape[1]
    tm = _pick(math.gcd(n_p, n_s), (1024, 512, 256))
    tn = _pick(width, (1024, 512, 256, 128))
    npt, nst = n_p // tm, n_s // tm
    return pl.pallas_call(
        functools.partial(_in_proj_body, n_prompt_tiles=npt),
        grid=(npt + nst, width // tn),
        in_specs=[pl.BlockSpec((tm, d), lambda i, j: (jnp.minimum(i, npt - 1), 0)),
                  pl.BlockSpec((tm, d), lambda i, j: (jnp.clip(i - npt, 0, nst - 1), 0),
                               pipeline_mode=pl.Buffered(1)),
                  pl.BlockSpec((1, d), lambda i, j: (0, 0)),
                  pl.BlockSpec((d, tn), lambda i, j: (0, j))],
        out_specs=pl.BlockSpec((tm, tn), lambda i, j: (i, j)),
        out_shape=jax.ShapeDtypeStruct((n_p + n_s, width), BF16),
        scratch_shapes=[pltpu.VMEM((tm, d), BF16)],
        compiler_params=pltpu.CompilerParams(dimension_semantics=("parallel", "arbitrary"),
                                             vmem_limit_bytes=VMEM_LIMIT),
        name="in_proj",
    )(xp, xs, g, w)


def _cumsum_rows(x, tril3):
    hi = x.astype(BF16)
    r1 = x - hi.astype(F32)
    mid = r1.astype(BF16)
    lo = (r1 - mid.astype(F32)).astype(BF16)
    return _dot(tril3, jnp.concatenate([hi, mid, lo], axis=0))


def _hgrn_scratch(chunk, heads, streams):
    half = min(REC_HALF, chunk)
    wblk = heads * HEAD_DIM
    per_chunk = ([pltpu.VMEM((chunk, wblk), BF16),
                  pltpu.VMEM((chunk, wblk), BF16),
                  pltpu.VMEM((chunk, wblk), BF16),
                  pltpu.VMEM((SUBLANES, wblk), F32),
                  pltpu.VMEM((heads, chunk, chunk), BF16),
                  pltpu.VMEM((chunk, wblk), F32)]
                 + [pltpu.VMEM((half * (g + 1), wblk), BF16) for g in range(chunk // half)])
    return [pltpu.VMEM((streams * heads, HEAD_DIM, HEAD_DIM), F32)] + per_chunk * REC_CHUNKS_PER_STEP


def _hgrn_passes(q_ref, f_ref, i_ref, zo_ref, lb_ref, gh_ref, og_ref, scratch, rows, og_rows, chunk, heads,
                 reset=None):
    st_ref, qs_ref, qe_ref, kd_ref, dec_ref, a_ref, o_ref = scratch[:7]
    key_refs = scratch[7:]
    half = min(REC_HALF, chunk)
    n_half = chunk // half
    tril = (lax.broadcasted_iota(jnp.int32, (chunk, chunk), 0)
            >= lax.broadcasted_iota(jnp.int32, (chunk, chunk), 1))
    tril = jnp.where(tril, 1.0, 0.0).astype(BF16)
    tril3 = jnp.concatenate([tril, tril, tril], axis=1)
    masks = []
    for g in range(n_half):
        r = lax.broadcasted_iota(jnp.int32, (half, half * (g + 1)), 0)
        c = lax.broadcasted_iota(jnp.int32, (half, half * (g + 1)), 1)
        masks.append(r + g * half >= c)
    head_slices = [slice(h * HEAD_DIM, (h + 1) * HEAD_DIM) for h in range(heads)]

    def operands():
        for hs in head_slices:
            q = q_ref[rows, hs].astype(F32)
            lb = lb_ref[:, hs]
            c1 = 0.5 * (1.0 - lb)
            f = (lb + c1) + c1 * jnp.tanh(0.5 * f_ref[rows, hs].astype(F32))
            kk = 1.0 - f
            b = _cumsum_rows(jnp.log(f), tril3) * LOG2_E
            b_last = b[chunk - 1:chunk, :]
            dec_ref[0:1, hs] = jnp.exp2(b_last)
            mids, ks = [], []
            for g in range(n_half):
                rg = slice(g * half, (g + 1) * half)
                mid = b[g * half + half // 2 - 1:g * half + half // 2, :]
                qs_g = q[rg] * jnp.exp2(b[rg] - mid)
                ks_g = kk[rg] * jnp.exp2(mid - b[rg])
                mids.append(mid)
                ks.append(ks_g)
                qs_ref[rg, hs] = qs_g.astype(BF16)
                qe_ref[rg, hs] = (qs_g * jnp.exp2(mid)).astype(BF16)
                kd_ref[rg, hs] = (ks_g * jnp.exp2(b_last - mid)).astype(BF16)
                for gp in range(g):
                    key_refs[g][gp * half:(gp + 1) * half, hs] = (
                        ks[gp] * jnp.exp2(mid - mids[gp])).astype(BF16)
                key_refs[g][rg, hs] = ks_g.astype(BF16)

    def scores():
        for h, hs in enumerate(head_slices):
            for g in range(n_half):
                rg = slice(g * half, (g + 1) * half)
                a = _dot_nt(qs_ref[rg, hs], key_refs[g][:, hs])
                a_ref[h, rg, 0:half * (g + 1)] = jnp.where(masks[g], a, 0.0).astype(BF16)

    def outputs_and_state():
        for h, hs in enumerate(head_slices):
            v = i_ref[rows, hs]
            st = st_ref[h]
            if reset is not None:
                st = jnp.where(reset, 0.0, st)
            o_inter = _dot_nt(qe_ref[:, hs], st.astype(BF16))
            for g in range(n_half):
                rg = slice(g * half, (g + 1) * half)
                o_ref[rg, hs] = o_inter[rg] + _dot(a_ref[h, rg, 0:half * (g + 1)], v[0:half * (g + 1)])
            st_ref[h] = st * dec_ref[0:1, hs] + _dot_tn(v, kd_ref[:, hs])

    def normalise():
        for hs in head_slices:
            o = o_ref[:, hs]
            o_n = o * lax.rsqrt(jnp.mean(o * o, axis=-1, keepdims=True) + EPS) * (0.5 * gh_ref[:, hs])
            gate2 = jnp.tanh(0.5 * zo_ref[rows, hs].astype(F32)) + 1.0
            og_ref[og_rows, hs] = (o_n * gate2).astype(og_ref.dtype)

    return [operands, scores, outputs_and_state, normalise]


def _hgrn_body(q_ref, f_ref, i_ref, zo_ref, s0_ref, lb_ref, gh_ref, og_ref, sout_ref, *scratch,
               chunk, n_chunks, heads, streams):
    ti = pl.program_id(2)
    st_ref = scratch[0]

    @pl.when(ti == 0)
    def _():
        for k in range(streams):
            for h in range(heads):
                st_ref[k * heads + h] = s0_ref[k, h].T

    n_set = (len(scratch) - 1) // REC_CHUNKS_PER_STEP
    staging = [tuple(scratch[1 + k * n_set:1 + (k + 1) * n_set]) for k in range(REC_CHUNKS_PER_STEP)]

    def run_alternated(passes):
        for stage in zip(*passes):
            for run_pass in stage:
                run_pass()

    def passes_for(state_ref, rows, k):
        return _hgrn_passes(q_ref, f_ref, i_ref, zo_ref, lb_ref, gh_ref, og_ref, (state_ref,) + staging[k],
                            rows, rows, chunk, heads)

    if streams > 1:
        assert n_chunks == 1 and streams <= REC_CHUNKS_PER_STEP
        run_alternated([passes_for(st_ref.at[pl.ds(k * heads, heads)], slice(k * chunk, (k + 1) * chunk), k)
                        for k in range(streams)])
    else:
        per_step = math.gcd(n_chunks, REC_CHUNKS_PER_STEP)

        def chunk_step(ci, carry):
            run_alternated([passes_for(st_ref, pl.ds(pl.multiple_of((ci * per_step + k) * chunk, chunk), chunk), k)
                            for k in range(per_step)])
            return carry

        lax.fori_loop(0, n_chunks // per_step, chunk_step, 0)

    @pl.when(ti == pl.num_programs(2) - 1)
    def _():
        for k in range(streams):
            for h in range(heads):
                sout_ref[k, h] = st_ref[k * heads + h].T


def _hgrn(z, s0, lb, g_head, *, row_base, batch, seq, d_model, name):
    n_heads = d_model // HEAD_DIM
    heads = min(16, n_heads)
    chunk = min(REC_CHUNK, seq)
    half = min(REC_HALF, chunk)
    assert seq % chunk == 0 and chunk % half == 0
    rt = max(_pick(seq, (512, 256, 128, 64, 32)), chunk)
    tiles = seq // rt
    streams = math.gcd(batch, REC_CHUNKS_PER_STEP) if (tiles == 1 and rt == chunk) else 1
    blk_rows = streams * rt
    assert row_base % blk_rows == 0
    wblk = heads * HEAD_DIM
    cpb = d_model // wblk

    def zspec(section):
        return pl.BlockSpec((blk_rows, wblk),
                            lambda b, hg, i: (row_base // blk_rows + b * tiles + i, section * cpb + hg))

    return pl.pallas_call(
        functools.partial(_hgrn_body, chunk=chunk, n_chunks=rt // chunk, heads=heads, streams=streams),
        grid=(batch // streams, n_heads // heads, tiles),
        in_specs=[zspec(0), zspec(1), zspec(2), zspec(3),
                  pl.BlockSpec((streams, heads, HEAD_DIM, HEAD_DIM), lambda b, hg, i: (b, hg, 0, 0)),
                  pl.BlockSpec((1, wblk), lambda b, hg, i: (0, hg)),
                  pl.BlockSpec((1, wblk), lambda b, hg, i: (0, hg))],
        out_specs=[pl.BlockSpec((blk_rows, wblk), lambda b, hg, i: (b * tiles + i, hg)),
                   pl.BlockSpec((streams, heads, HEAD_DIM, HEAD_DIM), lambda b, hg, i: (b, hg, 0, 0))],
        out_shape=[jax.ShapeDtypeStruct((batch * seq, d_model), BF16),
                   jax.ShapeDtypeStruct((batch, n_heads, HEAD_DIM, HEAD_DIM), F32)],
        scratch_shapes=_hgrn_scratch(chunk, heads, streams),
        compiler_params=pltpu.CompilerParams(dimension_semantics=("parallel", "parallel", "arbitrary"),
                                             vmem_limit_bytes=VMEM_LIMIT),
        name=name,
    )(z, z, z, z, s0, lb, g_head)


def _mix_body(zu_ref, zv_ref, zga_ref, zgb_ref, ogp_ref, ogs_ref, xp_ref, xs_ref, lng_ref, lnb_ref,
              wsp_ref, wss_ref, bsp_ref, bss_ref, wpa_hbm, wpb_hbm, wo_hbm,
              h1_ref, vnp_ref, vns_ref, sg_ref, wpa_ref, wpb_ref, wo_ref, stage_ref, wsem, ya_ref,
              *, n_prompt_tiles, dec_seq):
    @pl.when(pl.program_id(0) == 0)
    def _():
        _load_as_bf16(wpa_hbm, wpa_ref, stage_ref, wsem)
        _load_as_bf16(wpb_hbm, wpb_ref, stage_ref, wsem)
        _load_as_bf16(wo_hbm, wo_ref, stage_ref, wsem)

    is_p = pl.program_id(0) < n_prompt_tiles
    tm, width = zu_ref.shape
    gd = width // MLP_GROUPS
    og = jnp.where(is_p, ogp_ref[...].astype(F32), ogs_ref[...].astype(F32)).astype(BF16)
    n_chunks = tm // MLP_CHUNK
    d = h1_ref.shape[1]
    col = d // (2 * n_chunks)

    def ya_slice(j):
        cs = slice(j * col, (j + 1) * col)
        ya_ref[:, cs] = _dot(og, wpa_ref[:, cs])

    r = lax.broadcasted_iota(jnp.int32, (MLP_CHUNK, MLP_CHUNK), 0)
    c = lax.broadcasted_iota(jnp.int32, (MLP_CHUNK, MLP_CHUNK), 1)
    causal = r >= c
    same_stream = (r // dec_seq) == (c // dec_seq)
    w_mix, bias = [], []
    for g in range(MLP_GROUPS):
        w_p = jnp.where(causal, wsp_ref[g], 0.0)
        w_s = jnp.where(causal & same_stream, wss_ref[g], 0.0)
        w_mix.append(jnp.where(is_p, w_p, w_s).astype(BF16))
        bias.append(jnp.where(is_p, bsp_ref[g], bss_ref[g]))
    for cc in range(n_chunks):
        rows = slice(cc * MLP_CHUNK, (cc + 1) * MLP_CHUNK)
        ya_slice(2 * cc)
        u = _gelu(zu_ref[rows, :].astype(F32))
        gv = _gelu(zv_ref[rows, :].astype(F32))
        xc = gv - jnp.mean(gv, axis=-1, keepdims=True)
        vn = xc * lax.rsqrt(jnp.mean(xc * xc, axis=-1, keepdims=True) + EPS) * lng_ref[...] + lnb_ref[...]
        vnp_ref[rows, :] = vn
        vns_ref[rows, :] = vn
        ya_slice(2 * cc + 1)
        vnb = vn.astype(BF16)
        for g in range(MLP_GROUPS):
            cols = slice(g * gd, (g + 1) * gd)
            s = _dot(w_mix[g], vnb[:, cols]) + bias[g]
            sg_ref[rows, cols] = (u[:, cols] * s).astype(BF16)
    y_b = _dot(sg_ref[...], wpb_ref[...])
    m = _sigmoid(zga_ref[...].astype(F32)) * ya_ref[...] + _sigmoid(zgb_ref[...].astype(F32)) * y_b
    x = jnp.where(is_p, xp_ref[...], xs_ref[...])
    h1_ref[...] = x + _dot(m.astype(BF16), wo_ref[...])


def _mix(z, og_p, og_s, xp, xs, ln_g, ln_b, ws_p, ws_s, bs_p, bs_s, w_pa, w_pb, w_o, *, batch, dec_seq):
    (n_p, d), n_s = xp.shape, xs.shape[0]
    n = n_p + n_s
    width = w_pb.shape[0]
    tm = 256
    assert n_s % tm == 0 and (n_p // batch) % tm == 0 and tm % MLP_CHUNK == 0 and MLP_CHUNK % dec_seq == 0
    npt, nst = n_p // tm, n_s // tm
    tpb = npt // batch
    u_blk = 4 * d // width
    ga_blk = (4 * d + 2 * width) // d
    const2 = lambda i: (0, 0)
    const3 = lambda i: (0, 0, 0)
    p_map = lambda i: (jnp.minimum(i, npt - 1), 0)
    s_map = lambda i: (jnp.clip(i - npt, 0, nst - 1), 0)
    return pl.pallas_call(
        functools.partial(_mix_body, n_prompt_tiles=npt, dec_seq=dec_seq),
        grid=(n // tm,),
        in_specs=[pl.BlockSpec((tm, width), lambda i: (i, u_blk)),
                  pl.BlockSpec((tm, width), lambda i: (i, u_blk + 1)),
                  pl.BlockSpec((tm, d), lambda i: (i, ga_blk)),
                  pl.BlockSpec((tm, d), lambda i: (i, ga_blk + 1)),
                  pl.BlockSpec((tm, d), p_map),
                  pl.BlockSpec((tm, d), s_map),
                  pl.BlockSpec((tm, d), p_map),
                  pl.BlockSpec((tm, d), s_map),
                  pl.BlockSpec((1, width), const2),
                  pl.BlockSpec((1, width), const2),
                  pl.BlockSpec((MLP_GROUPS, MLP_CHUNK, MLP_CHUNK), const3),
                  pl.BlockSpec((MLP_GROUPS, MLP_CHUNK, MLP_CHUNK), const3),
                  pl.BlockSpec((MLP_GROUPS, MLP_CHUNK, 1), const3),
                  pl.BlockSpec((MLP_GROUPS, MLP_CHUNK, 1), const3),
                  pl.BlockSpec(memory_space=pl.ANY),
                  pl.BlockSpec(memory_space=pl.ANY),
                  pl.BlockSpec(memory_space=pl.ANY)],
        out_specs=[pl.BlockSpec((tm, d), lambda i: (i, 0)),
                   pl.BlockSpec((tm, width), lambda i: (jnp.where(i < npt, i // tpb, batch), 0)),
                   pl.BlockSpec((tm, width), lambda i: (jnp.where(i < npt, nst, i - npt), 0))],
        out_shape=[jax.ShapeDtypeStruct((n, d), F32),
                   jax.ShapeDtypeStruct(((batch + 1) * tm, width), F32),
                   jax.ShapeDtypeStruct(((nst + 1) * tm, width), F32)],
        scratch_shapes=[pltpu.VMEM((tm, width), BF16),
                        pltpu.VMEM((d, d), BF16), pltpu.VMEM((width, d), BF16), pltpu.VMEM((d, d), BF16),
                        pltpu.VMEM((WEIGHT_STAGE_ROWS, d), F32), pltpu.SemaphoreType.DMA((2,)),
                        pltpu.VMEM((tm, d), F32)],
        compiler_params=pltpu.CompilerParams(dimension_semantics=("arbitrary",),
                                             vmem_limit_bytes=VMEM_LIMIT),
        name="mix",
    )(z, z, z, z, og_p, og_s, xp, xs, ln_g, ln_b, ws_p, ws_s, bs_p, bs_s, w_pa, w_pb, w_o)


def _router_body(h_ref, g_ref, wr_ref, br_ref, c_ref, idx_ref, gate_ref, cnt_ref, carry_ref):
    @pl.when(pl.program_id(0) == 0)
    def _():
        carry_ref[...] = jnp.zeros_like(carry_ref)

    tm = h_ref.shape[0]
    c = _rms(h_ref[...], g_ref[...])
    wr = wr_ref[...]
    wr_hi = wr.astype(BF16)
    wr_lo = (wr - wr_hi.astype(F32)).astype(BF16)
    c_hi = c.astype(BF16)
    c_lo = (c - c_hi.astype(F32)).astype(BF16)
    _store_chunks(c_ref, _pack_bf16_pairs(c_hi.astype(F32)))
    both = _dot_nt(jnp.concatenate([wr_hi, wr_lo], axis=0), c_hi)
    lt = both[0:ROUTER_ROWS] + both[ROUTER_ROWS:2 * ROUTER_ROWS] + _dot_nt(wr_hi, c_lo) + br_ref[...]
    le = lt[0:N_EXPERTS]
    lg = lt[N_EXPERTS:N_EXPERTS + N_GROUPS]
    gmax = jnp.max(lg, axis=0, keepdims=True)
    p_sel = 1.0 / jnp.sum(jnp.exp(lg - gmax), axis=0, keepdims=True)
    best = lg[0:1]
    gi = jnp.zeros((1, tm), jnp.int32)
    for g in range(1, N_GROUPS):
        better = lg[g:g + 1] > best
        gi = jnp.where(better, g, gi)
        best = jnp.where(better, lg[g:g + 1], best)
    leg = jnp.zeros((E_PER_GROUP, tm), F32)
    for g in range(N_GROUPS):
        leg = jnp.where(gi == g, le[g * E_PER_GROUP:(g + 1) * E_PER_GROUP], leg)
    sub = lax.broadcasted_iota(jnp.int32, (E_PER_GROUP, tm), 0).astype(F32)
    v1 = jnp.max(leg, axis=0, keepdims=True)
    i1 = jnp.min(jnp.where(leg == v1, sub, float(E_PER_GROUP)), axis=0, keepdims=True)
    rest = jnp.where(sub == i1, -jnp.inf, leg)
    v2 = jnp.max(rest, axis=0, keepdims=True)
    i2 = jnp.min(jnp.where(rest == v2, sub, float(E_PER_GROUP)), axis=0, keepdims=True)
    e2 = jnp.exp(v2 - v1)
    den = 1.0 + e2
    gate0 = p_sel * (1.0 / den)
    gate1 = p_sel * (e2 / den)
    ex0 = gi * E_PER_GROUP + i1.astype(jnp.int32)
    ex1 = gi * E_PER_GROUP + i2.astype(jnp.int32)
    eid = lax.broadcasted_iota(jnp.int32, (N_EXPERTS, tm), 0)
    oh0 = eid == ex0
    oh1 = eid == ex1
    oh = jnp.where(oh0 | oh1, 1.0, 0.0)
    upper = jnp.where(lax.broadcasted_iota(jnp.int32, (tm, tm), 0) < lax.broadcasted_iota(jnp.int32, (tm, tm), 1),
                      1.0, 0.0).astype(BF16)
    before = _dot(oh.astype(BF16), upper) + carry_ref[:, 0:1]
    rank0 = jnp.sum(jnp.where(oh0, before, 0.0), axis=0, keepdims=True)
    rank1 = jnp.sum(jnp.where(oh1, before, 0.0), axis=0, keepdims=True)
    carry = carry_ref[...] + jnp.sum(oh, axis=1, keepdims=True)
    carry_ref[...] = carry
    cnt_ref[...] = carry.astype(jnp.int32)
    idx_ref[...] = jnp.zeros_like(idx_ref)
    idx_ref[0:1, :] = ex0
    idx_ref[1:2, :] = ex1
    idx_ref[2:3, :] = rank0.astype(jnp.int32)
    idx_ref[3:4, :] = rank1.astype(jnp.int32)
    gate_ref[...] = jnp.zeros_like(gate_ref)
    gate_ref[0:1, :] = gate0
    gate_ref[1:2, :] = gate1


def _router(h1, g_ffn, wr, br):
    n, d = h1.shape
    tm = _pick(n, (512, 256))
    return pl.pallas_call(
        _router_body,
        grid=(n // tm,),
        in_specs=[pl.BlockSpec((tm, d), lambda i: (i, 0)),
                  pl.BlockSpec((1, d), lambda i: (0, 0)),
                  pl.BlockSpec((ROUTER_ROWS, d), lambda i: (0, 0)),
                  pl.BlockSpec((ROUTER_ROWS, 1), lambda i: (0, 0))],
        out_specs=[pl.BlockSpec((d // (2 * LANES), tm, LANES), lambda i: (0, i, 0)),
                   pl.BlockSpec((SUBLANES, tm), lambda i: (0, i)),
                   pl.BlockSpec((SUBLANES, tm), lambda i: (0, i)),
                   pl.BlockSpec((N_EXPERTS, LANES), lambda i: (0, 0))],
        out_shape=[jax.ShapeDtypeStruct((d // (2 * LANES), n, LANES), jnp.uint32),
                   jax.ShapeDtypeStruct((SUBLANES, n), jnp.int32),
                   jax.ShapeDtypeStruct((SUBLANES, n), F32),
                   jax.ShapeDtypeStruct((N_EXPERTS, LANES), jnp.int32)],
        scratch_shapes=[pltpu.VMEM((N_EXPERTS, LANES), F32)],
        compiler_params=pltpu.CompilerParams(dimension_semantics=("arbitrary",),
                                             vmem_limit_bytes=VMEM_LIMIT),
        name="router",
    )(h1, g_ffn, wr, br)


def _dispatch_body(fill_lo_ref, fill_hi_ref, dest_ref, c_ref, xs_ref, zero_ref, sem, zsem):
    td = c_ref.shape[1]

    @pl.when(pl.program_id(0) == 0)
    def _():
        zero_ref[...] = jnp.zeros_like(zero_ref)

        def per_expert(e, carry):
            lo, hi = fill_lo_ref[e], fill_hi_ref[e]

            def start(r, c2):
                _row_copy(zero_ref, 0, xs_ref, r, zsem).start()
                return c2

            def wait(r, c2):
                _row_copy(zero_ref, 0, xs_ref, 0, zsem).wait()
                return c2

            lax.fori_loop(lo, hi, start, 0)
            lax.fori_loop(lo, hi, wait, 0)
            return carry

        lax.fori_loop(0, N_EXPERTS, per_expert, 0)

        def block_copy(j):
            return _rows_copy(zero_ref, xs_ref, j * MOE_BLOCK, MOE_BLOCK, zsem)

        first_unused = fill_hi_ref[N_EXPERTS - 1] // MOE_BLOCK
        n_blocks = xs_ref.shape[1] // MOE_BLOCK

        def start_block(j, carry):
            block_copy(j).start()
            return carry

        def wait_block(j, carry):
            block_copy(0).wait()
            return carry

        lax.fori_loop(first_unused, n_blocks, start_block, 0)
        lax.fori_loop(first_unused, n_blocks, wait_block, 0)

    for t in range(td):
        for k in range(TOP_K):
            _row_copy(c_ref, t, xs_ref, dest_ref[k, t], sem).start(priority=k % 2)
    for k in range(TOP_K):
        _rows_copy(c_ref, xs_ref, 0, td, sem).wait()


def _dispatch(fill_lo, fill_hi, dest, c, n_slots):
    nc, n, _ = c.shape
    td = _pick(n, (512, 256))
    grid_spec = pltpu.PrefetchScalarGridSpec(
        num_scalar_prefetch=2,
        grid=(n // td,),
        in_specs=[pl.BlockSpec((TOP_K, td), lambda i, lo, hi: (0, i), memory_space=pltpu.SMEM),
                  pl.BlockSpec((nc, td, LANES), lambda i, lo, hi: (0, i, 0))],
        out_specs=pl.BlockSpec(memory_space=pl.ANY),
        scratch_shapes=[pltpu.VMEM((nc, MOE_BLOCK, LANES), c.dtype),
                        pltpu.SemaphoreType.DMA(()), pltpu.SemaphoreType.DMA(())],
    )
    return pl.pallas_call(
        _dispatch_body,
        grid_spec=grid_spec,
        out_shape=jax.ShapeDtypeStruct((nc, n_slots, LANES), c.dtype),
        compiler_params=pltpu.CompilerParams(dimension_semantics=("arbitrary",),
                                             vmem_limit_bytes=VMEM_LIMIT),
        name="dispatch",
    )(fill_lo, fill_hi, dest, c)


def _expert_body(be_ref, na_ref, nxt_ref, xs_ref, wg_hbm, wu_hbm, wd_hbm, yb_ref,
                 wg_ref, wu_ref, wd_ref, sg_ref, su_ref, sd_ref, wsem):
    j = pl.program_id(0)
    active = j < na_ref[0]
    e = be_ref[j]
    first = (j == 0) | (be_ref[jnp.maximum(j - 1, 0)] != e)

    def weight_copies(expert):
        return (pltpu.make_async_copy(wg_hbm.at[expert], sg_ref, wsem.at[0]),
                pltpu.make_async_copy(wu_hbm.at[expert], su_ref, wsem.at[1]),
                pltpu.make_async_copy(wd_hbm.at[expert], sd_ref, wsem.at[2]))

    @pl.when(j == 0)
    def _():
        for cp in weight_copies(e):
            cp.start()

    @pl.when(active & first)
    def _():
        for cp, stage, dst in zip(weight_copies(e), (sg_ref, su_ref, sd_ref), (wg_ref, wu_ref, wd_ref)):
            cp.wait()

            def convert(r, carry, stage=stage, dst=dst):
                rows = pl.ds(pl.multiple_of(r * LANES, LANES), LANES)
                dst[rows, :] = stage[rows, :].astype(BF16)
                return carry

            lax.fori_loop(0, stage.shape[0] // LANES, convert, 0)

        @pl.when(nxt_ref[e] < N_EXPERTS)
        def _():
            for cp in weight_copies(nxt_ref[e]):
                cp.start()

    @pl.when(active)
    def _():
        x = _unpack_bf16_pairs(_load_chunks(xs_ref))
        a = _dot(x, wg_ref[...])
        b = _dot(x, wu_ref[...])
        hid = (a * _sigmoid(a)) * b
        _store_chunks(yb_ref, _dot(hid.astype(BF16), wd_ref[...]))

    @pl.when(jnp.logical_not(active))
    def _():
        yb_ref[...] = jnp.zeros_like(yb_ref)


def _experts(block_exp, n_active, next_exp, xs, w_gate, w_up, w_down):
    nc_in, n_slots, _ = xs.shape
    d, de = w_gate.shape[1:]
    nc = d // LANES
    nb = n_slots // MOE_BLOCK
    assert d % LANES == 0 and de % LANES == 0
    blk = lambda j, be, na, nxt: (0, jnp.minimum(j, na[0] - 1), 0)
    grid_spec = pltpu.PrefetchScalarGridSpec(
        num_scalar_prefetch=3,
        grid=(nb,),
        in_specs=[pl.BlockSpec((nc_in, MOE_BLOCK, LANES), blk),
                  pl.BlockSpec(memory_space=pl.ANY),
                  pl.BlockSpec(memory_space=pl.ANY),
                  pl.BlockSpec(memory_space=pl.ANY)],
        out_specs=pl.BlockSpec((nc, MOE_BLOCK, LANES), lambda j, be, na, nxt: (0, j, 0)),
        scratch_shapes=[pltpu.VMEM((d, de), BF16), pltpu.VMEM((d, de), BF16), pltpu.VMEM((de, d), BF16),
                        pltpu.VMEM((d, de), F32), pltpu.VMEM((d, de), F32), pltpu.VMEM((de, d), F32),
                        pltpu.SemaphoreType.DMA((3,))],
    )
    return pl.pallas_call(
        _expert_body,
        grid_spec=grid_spec,
        out_shape=jax.ShapeDtypeStruct((nc, n_slots, LANES), F32),
        compiler_params=pltpu.CompilerParams(dimension_semantics=("arbitrary",),
                                             vmem_limit_bytes=VMEM_LIMIT),
        name="experts",
    )(block_exp, n_active, next_exp, xs, w_gate, w_up, w_down)


def _combine_body(dcur_ref, dnxt_ref, gate_ref, h1_ref, pp_ref, ps_ref, yb_ref, gple_ref, wpg_hbm, wpp_hbm, gfin_ref,
                  yp_ref, ys_ref, buf_ref, sem, wpg_ref, wpp_ref, stage_ref, wsem,
                  *, n_prompt_tiles, final_norm):
    i = pl.program_id(0)

    @pl.when(i == 0)
    def _():
        _load_as_bf16(wpg_hbm, wpg_ref, stage_ref, wsem)
        _load_as_bf16(wpp_hbm, wpp_ref, stage_ref, wsem)

    n_tiles = pl.num_programs(0)
    tc = h1_ref.shape[0]
    slot = lax.rem(i, 2)

    @pl.when(i == 0)
    def _():
        def body(t, carry):
            for k in range(TOP_K):
                _row_copy(yb_ref, dcur_ref[k, t], buf_ref.at[0, k], t, sem.at[0]).start()
            return carry

        lax.fori_loop(0, tc, body, 0)

    def wait_slot(s):
        for k in range(TOP_K):
            pltpu.make_async_copy(yb_ref.at[:, pl.ds(0, tc), :], buf_ref.at[s, k], sem.at[s]).wait()

    wait_slot(slot)
    gt = gate_ref[...].T
    h2 = h1_ref[...] + (gt[:, 0:1] * _load_chunks(buf_ref.at[slot, 0])
                        + gt[:, 1:2] * _load_chunks(buf_ref.at[slot, 1]))
    for t in range(tc):
        for k in range(TOP_K):
            _row_copy(yb_ref, dnxt_ref[k, t], buf_ref.at[1 - slot, k], t, sem.at[1 - slot]).start(priority=k % 2)
    a = _rms(h2, gple_ref[...]).astype(BF16)
    p = jnp.where(i < n_prompt_tiles, pp_ref[...], ps_ref[...]).astype(BF16)
    h3 = h2 + _sigmoid(_dot(a, wpg_ref[...])) * _dot(p, wpp_ref[...])
    if final_norm:
        h3 = _rms(h3, gfin_ref[...])

    @pl.when(i < n_prompt_tiles)
    def _():
        yp_ref[...] = h3

    @pl.when(i >= n_prompt_tiles)
    def _():
        ys_ref[...] = h3

    @pl.when(i == n_tiles - 1)
    def _():
        wait_slot(1 - slot)


def _combine(dest, gates, h1, p_p, p_s, yb, g_ple, w_pg, w_pp, g_final, *, n_prompt, final_norm):
    n, d = h1.shape
    ple = p_p.shape[1]
    tc = 256
    assert n_prompt % tc == 0 and (n - n_prompt) % tc == 0
    nt, npt = n // tc, n_prompt // tc
    nst = nt - npt
    const2 = lambda i: (0, 0)
    return pl.pallas_call(
        functools.partial(_combine_body, n_prompt_tiles=npt, final_norm=final_norm),
        grid=(nt,),
        in_specs=[pl.BlockSpec((TOP_K, tc), lambda i: (0, i), memory_space=pltpu.SMEM),
                  pl.BlockSpec((TOP_K, tc), lambda i: (0, jnp.minimum(i + 1, nt - 1)), memory_space=pltpu.SMEM),
                  pl.BlockSpec((SUBLANES, tc), lambda i: (0, i)),
                  pl.BlockSpec((tc, d), lambda i: (i, 0)),
                  pl.BlockSpec((tc, ple), lambda i: (jnp.minimum(i, npt - 1), 0)),
                  pl.BlockSpec((tc, ple), lambda i: (jnp.clip(i - npt, 0, nst - 1), 0)),
                  pl.BlockSpec(memory_space=pl.ANY),
                  pl.BlockSpec((1, d), const2),
                  pl.BlockSpec(memory_space=pl.ANY),
                  pl.BlockSpec(memory_space=pl.ANY),
                  pl.BlockSpec((1, d), const2)],
        out_specs=[pl.BlockSpec((tc, d), lambda i: (jnp.minimum(i, npt - 1), 0)),
                   pl.BlockSpec((tc, d), lambda i: (jnp.clip(i - npt, 0, nst - 1), 0))],
        out_shape=[jax.ShapeDtypeStruct((n_prompt, d), F32),
                   jax.ShapeDtypeStruct((n - n_prompt, d), F32)],
        scratch_shapes=[pltpu.VMEM((2, TOP_K, d // LANES, tc, LANES), F32), pltpu.SemaphoreType.DMA((2,)),
                        pltpu.VMEM((d, d), BF16), pltpu.VMEM((ple, d), BF16),
                        pltpu.VMEM((WEIGHT_STAGE_ROWS, d), F32), pltpu.SemaphoreType.DMA((2,))],
        compiler_params=pltpu.CompilerParams(dimension_semantics=("arbitrary",),
                                             vmem_limit_bytes=VMEM_LIMIT),
        name="combine",
    )(dest, dest, gates, h1, p_p, p_s, yb, g_ple, w_pg, w_pp, g_final)


def _layer(hp, hs, p_p, p_s, s0_sample, lb, batch, seq, dec_batch, dec_seq,
           g_mix, w_in, g_head, w_pa, ln_v_g, ln_v_b, w_s, b_s, w_pb, w_o,
           g_ffn, w_gr, b_gr, w_er, b_er, w_gate, w_up, w_down, g_ple, w_pg, w_pp, g_final, final_norm):
    n_prompt, d = hp.shape
    n = n_prompt + hs.shape[0]
    n_heads = d // HEAD_DIM
    row = lambda a: a.reshape(1, -1).astype(F32)

    z = _in_proj(hp, hs, row(g_mix), w_in)

    lb_row, gh_row = row(lb), row(g_head)
    s0_prompt = jnp.zeros((batch, n_heads, HEAD_DIM, HEAD_DIM), F32)
    og_p, st_p = _hgrn(z, s0_prompt, lb_row, gh_row, row_base=0, batch=batch, seq=seq,
                       d_model=d, name="hgrn_prompt")
    og_s, st_s = _hgrn(z, s0_sample.astype(F32), lb_row, gh_row, row_base=n_prompt, batch=dec_batch,
                       seq=dec_seq, d_model=d, name="hgrn_sample")

    start = PAST_LEN % MLP_CHUNK
    assert start + dec_seq <= MLP_CHUNK
    rep = MLP_CHUNK // dec_seq
    ws_s = jnp.tile(w_s[:, start:start + dec_seq, start:start + dec_seq], (1, rep, rep))
    bs_s = jnp.tile(b_s[:, start:start + dec_seq], (1, rep))
    h1, vn_p, vn_s = _mix(z, og_p, og_s, hp, hs, row(ln_v_g), row(ln_v_b), w_s, ws_s, b_s[..., None],
                          bs_s[..., None], w_pa, w_pb, w_o,
                          batch=batch, dec_seq=dec_seq)

    pad_rows = ROUTER_ROWS - N_EXPERTS - N_GROUPS
    wr = jnp.concatenate([w_er.T, w_gr.T, jnp.zeros((pad_rows, d), F32)], axis=0)
    br = jnp.concatenate([b_er, b_gr, jnp.zeros((pad_rows,), F32)]).reshape(ROUTER_ROWS, 1).astype(F32)
    c, idx, gates, cnt = _router(h1, row(g_ffn), wr, br)

    counts = cnt[:, 0]
    padded = (counts + MOE_BLOCK - 1) // MOE_BLOCK * MOE_BLOCK
    pad_end = jnp.cumsum(padded).astype(jnp.int32)
    pad_start = pad_end - padded
    n_blocks = -(-(n * TOP_K) // MOE_BLOCK) + N_EXPERTS
    block_first = jnp.arange(n_blocks, dtype=jnp.int32) * MOE_BLOCK
    block_exp = jnp.minimum(jnp.sum(pad_end[None, :] <= block_first[:, None], axis=1), N_EXPERTS - 1).astype(jnp.int32)
    n_active = jnp.maximum(pad_end[-1:] // MOE_BLOCK, 1)
    hit = idx[0:TOP_K, :, None] == jnp.arange(N_EXPERTS, dtype=jnp.int32)
    dest = jnp.sum(jnp.where(hit, pad_start, 0), axis=-1) + idx[TOP_K:2 * TOP_K]

    xs = _dispatch(pad_start + counts, pad_end, dest, c, n_blocks * MOE_BLOCK)
    used = jnp.where(counts > 0, jnp.arange(N_EXPERTS, dtype=jnp.int32), N_EXPERTS)
    next_exp = jnp.concatenate([lax.cummin(used, axis=0, reverse=True)[1:],
                                jnp.full((1,), N_EXPERTS, jnp.int32)]).astype(jnp.int32)
    yb = _experts(block_exp, n_active, next_exp, xs, w_gate, w_up, w_down)
    yp, ys = _combine(dest, gates, h1, p_p, p_s, yb, row(g_ple), w_pg, w_pp,
                      row(g_final), n_prompt=n_prompt, final_norm=final_norm)
    return yp, ys, st_p, st_s, vn_p, vn_s


def kernel(x_prompt, x_sample, p_prompt, p_sample, state_hgrn, g_mix, w_in, lb_logits, g_head, w_pa, ln_v_g, ln_v_b, w_s, b_s, w_pb, w_o, g_ffn, w_gr, b_gr, w_er, b_er, w_gate, w_up, w_down, g_ple, w_pg, w_pp, g_final):
    batch, seq, d = x_prompt.shape
    dec_batch, dec_seq, _ = x_sample.shape
    depth = g_mix.shape[0]
    n_prompt, n_sample = batch * seq, dec_batch * dec_seq
    width = w_pb.shape[1]
    lbs = jnp.cumsum(jax.nn.softmax(lb_logits.astype(F32), axis=0), axis=0)
    hp, hs = x_prompt.reshape(n_prompt, d), x_sample.reshape(n_sample, d)
    keep = min((seq - 1) % MLP_CHUNK + 1, seq)
    keep_s = min((PAST_LEN % MLP_CHUNK + dec_seq - 1) % MLP_CHUNK + 1, dec_seq)
    sp, ss, vp, vs = [], [], [], []
    for i in range(depth):
        p_p, p_s = p_prompt[i].reshape(n_prompt, -1), p_sample[i].reshape(n_sample, -1)
        hp, hs, st_p, st_s, vn_p, vn_s = _layer(
            hp, hs, p_p, p_s, state_hgrn[i], lbs[i], batch, seq, dec_batch, dec_seq,
            g_mix[i], w_in[i], g_head[i], w_pa[i], ln_v_g[i], ln_v_b[i], w_s[i], b_s[i], w_pb[i], w_o[i],
            g_ffn[i], w_gr[i], b_gr[i], w_er[i], b_er[i], w_gate[i], w_up[i], w_down[i],
            g_ple[i], w_pg[i], w_pp[i], g_final, i == depth - 1)
        sp.append(st_p.astype(x_prompt.dtype))
        ss.append(st_s.astype(state_hgrn.dtype))
        vn_last = vn_p.reshape(batch + 1, -1, width)[:batch]
        vp.append(vn_last[:, vn_last.shape[1] - keep:])
        vs.append(vn_s[:n_sample].reshape(dec_batch, dec_seq, width)[:, dec_seq - keep_s:])
    y_prompt = hp.reshape(batch, seq, d)
    y_sample = hs.reshape(dec_batch, dec_seq, d)
    return (y_prompt, y_sample, jnp.stack(sp), jnp.stack(ss), jnp.stack(vp), jnp.stack(vs))
```

```python
import functools
import math

import jax
import jax.numpy as jnp
from jax import lax
from jax.experimental import pallas as pl
from jax.experimental.pallas import tpu as pltpu

F32 = jnp.float32
BF16 = jnp.bfloat16

EPS = 1e-6
HEAD_DIM = 128
REC_CHUNK = 64
REC_HALF = 32
REC_CHUNKS_PER_STEP = 4
MLP_CHUNK = 128
MLP_GROUPS = 4
N_GROUPS = 4
E_PER_GROUP = 8
N_EXPERTS = N_GROUPS * E_PER_GROUP
TOP_K = 2
MOE_BLOCK = 256
PAST_LEN = 2048
LANES = 128
SUBLANES = 8
ROUTER_ROWS = 48
VMEM_LIMIT = 56 * 1024 * 1024
LOG2_E = 1.4426950408889634
WEIGHT_STAGE_ROWS = 256


def _pick(n, candidates):
    for c in candidates:
        if n % c == 0:
            return c
    raise ValueError(f"no tile in {candidates} divides {n}")


def _dot(a, b):
    return jnp.dot(a, b, preferred_element_type=F32)


def _dot_nt(a, b, precision=None):
    return lax.dot_general(a, b, (((1,), (1,)), ((), ())), precision=precision, preferred_element_type=F32)


def _dot_tn(a, b):
    return lax.dot_general(a, b, (((0,), (0,)), ((), ())), preferred_element_type=F32)


def _rms(x, g):
    return x * lax.rsqrt(jnp.mean(x * x, axis=-1, keepdims=True) + EPS) * g


def _gelu(x):
    c = math.sqrt(2.0 / math.pi)
    return x * (0.5 * (1.0 + jnp.tanh(c * (x + 0.044715 * (x * x * x)))))


def _sigmoid(x):
    return 0.5 * jnp.tanh(0.5 * x) + 0.5


def _store_chunks(ref, x):
    for c in range(ref.shape[0]):
        ref[c] = x[:, c * LANES:(c + 1) * LANES].astype(ref.dtype)


def _load_chunks(ref):
    return jnp.concatenate([ref[c] for c in range(ref.shape[0])], axis=1)


HIGH_HALF = 0xFFFF0000


def _pack_bf16_pairs(x):
    half = x.shape[1] // 2
    lo = jnp.right_shift(pltpu.bitcast(x[:, :half], jnp.uint32), jnp.uint32(16))
    hi = jnp.bitwise_and(pltpu.bitcast(x[:, half:], jnp.uint32), jnp.uint32(HIGH_HALF))
    return jnp.bitwise_or(lo, hi)


def _unpack_bf16_pairs(p):
    lo = pltpu.bitcast(jnp.left_shift(p, jnp.uint32(16)), F32)
    hi = pltpu.bitcast(jnp.bitwise_and(p, jnp.uint32(HIGH_HALF)), F32)
    return jnp.concatenate([lo, hi], axis=1).astype(BF16)


def _row_copy(src_ref, src_row, dst_ref, dst_row, sem):
    return pltpu.make_async_copy(src_ref.at[:, pl.ds(src_row, 1), :], dst_ref.at[:, pl.ds(dst_row, 1), :], sem)


def _rows_copy(src_ref, dst_ref, dst_row, n_rows, sem):
    return pltpu.make_async_copy(src_ref, dst_ref.at[:, pl.ds(dst_row, n_rows), :], sem)


def _load_as_bf16(w_hbm, w_vmem, stage_ref, sem):
    half = stage_ref.shape[0] // 2
    rows, cols = min(half, w_hbm.shape[0]), min(stage_ref.shape[1], w_hbm.shape[1])
    assert w_hbm.shape[0] % rows == 0 and w_hbm.shape[1] % cols == 0
    chunks = [(r, c) for r in range(0, w_hbm.shape[0], rows) for c in range(0, w_hbm.shape[1], cols)]

    def slot(i):
        return stage_ref.at[pl.ds((i % 2) * half, rows), pl.ds(0, cols)]

    def copy(i):
        r, c = chunks[i]
        return pltpu.make_async_copy(w_hbm.at[pl.ds(r, rows), pl.ds(c, cols)], slot(i), sem.at[i % 2])

    copy(0).start()
    for i, (r, c) in enumerate(chunks):
        if i + 1 < len(chunks):
            copy(i + 1).start()
        copy(i).wait()
        w_vmem[pl.ds(r, rows), pl.ds(c, cols)] = slot(i)[...].astype(BF16)


def _in_proj_body(xp_ref, xs_ref, g_ref, w_ref, z_ref, xn_ref, *, n_prompt_tiles):
    first = pl.program_id(1) == 0
    is_p = pl.program_id(0) < n_prompt_tiles

    @pl.when(first & is_p)
    def _():
        xn_ref[...] = _rms(xp_ref[...], g_ref[...]).astype(BF16)

    @pl.when(first & jnp.logical_not(is_p))
    def _():
        xn_ref[...] = _rms(xs_ref[...], g_ref[...]).astype(BF16)

    z_ref[...] = _dot(xn_ref[...], w_ref[...].astype(BF16)).astype(z_ref.dtype)


def _in_proj(xp, xs, g, w):
    (n_p, d), n_s = xp.shape, xs.shape[0]
    width = w.shape[1]
    tm = _pick(math.gcd(n_p, n_s), (1024, 512, 256))
    tn = _pick(width, (1024, 512, 256, 128))
    npt, nst = n_p // tm, n_s // tm
    return pl.pallas_call(
        functools.partial(_in_proj_body, n_prompt_tiles=npt),
        grid=(npt + nst, width // tn),
        in_specs=[pl.BlockSpec((tm, d), lambda i, j: (jnp.minimum(i, npt - 1), 0)),
                  pl.BlockSpec((tm, d), lambda i, j: (jnp.clip(i - npt, 0, nst - 1), 0),
                               pipeline_mode=pl.Buffered(1)),
                  pl.BlockSpec((1, d), lambda i, j: (0, 0)),
                  pl.BlockSpec((d, tn), lambda i, j: (0, j))],
        out_specs=pl.BlockSpec((tm, tn), lambda i, j: (i, j)),
        out_shape=jax.ShapeDtypeStruct((n_p + n_s, width), BF16),
        scratch_shapes=[pltpu.VMEM((tm, d), BF16)],
        compiler_params=pltpu.CompilerParams(dimension_semantics=("parallel", "arbitrary"),
                                             vmem_limit_bytes=VMEM_LIMIT),
        name="in_proj",
    )(xp, xs, g, w)


def _cumsum_rows(x, tril3):
    hi = x.astype(BF16)
    r1 = x - hi.astype(F32)
    mid = r1.astype(BF16)
    lo = (r1 - mid.astype(F32)).astype(BF16)
    return _dot(tril3, jnp.concatenate([hi, mid, lo], axis=0))


def _hgrn_scratch(chunk, heads, streams):
    half = min(REC_HALF, chunk)
    wblk = heads * HEAD_DIM
    per_chunk = ([pltpu.VMEM((chunk, wblk), BF16),
                  pltpu.VMEM((chunk, wblk), BF16),
                  pltpu.VMEM((chunk, wblk), BF16),
                  pltpu.VMEM((SUBLANES, wblk), F32),
                  pltpu.VMEM((heads, chunk, chunk), BF16),
                  pltpu.VMEM((chunk, wblk), F32)]
                 + [pltpu.VMEM((half * (g + 1), wblk), BF16) for g in range(chunk // half)])
    return [pltpu.VMEM((streams * heads, HEAD_DIM, HEAD_DIM), F32)] + per_chunk * REC_CHUNKS_PER_STEP


def _hgrn_passes(q_ref, f_ref, i_ref, zo_ref, lb_ref, gh_ref, og_ref, scratch, rows, og_rows, chunk, heads,
                 reset=None):
    st_ref, qs_ref, qe_ref, kd_ref, dec_ref, a_ref, o_ref = scratch[:7]
    key_refs = scratch[7:]
    half = min(REC_HALF, chunk)
    n_half = chunk // half
    tril = (lax.broadcasted_iota(jnp.int32, (chunk, chunk), 0)
            >= lax.broadcasted_iota(jnp.int32, (chunk, chunk), 1))
    tril = jnp.where(tril, 1.0, 0.0).astype(BF16)
    tril3 = jnp.concatenate([tril, tril, tril], axis=1)
    masks = []
    for g in range(n_half):
        r = lax.broadcasted_iota(jnp.int32, (half, half * (g + 1)), 0)
        c = lax.broadcasted_iota(jnp.int32, (half, half * (g + 1)), 1)
        masks.append(r + g * half >= c)
    head_slices = [slice(h * HEAD_DIM, (h + 1) * HEAD_DIM) for h in range(heads)]

    def operands():
        for hs in head_slices:
            q = q_ref[rows, hs].astype(F32)
            lb = lb_ref[:, hs]
            c1 = 0.5 * (1.0 - lb)
            f = (lb + c1) + c1 * jnp.tanh(0.5 * f_ref[rows, hs].astype(F32))
            kk = 1.0 - f
            b = _cumsum_rows(jnp.log(f), tril3) * LOG2_E
            b_last = b[chunk - 1:chunk, :]
            dec_ref[0:1, hs] = jnp.exp2(b_last)
            mids, ks = [], []
            for g in range(n_half):
                rg = slice(g * half, (g + 1) * half)
                mid = b[g * half + half // 2 - 1:g * half + half // 2, :]
                qs_g = q[rg] * jnp.exp2(b[rg] - mid)
                ks_g = kk[rg] * jnp.exp2(mid - b[rg])
                mids.append(mid)
                ks.append(ks_g)
                qs_ref[rg, hs] = qs_g.astype(BF16)
                qe_ref[rg, hs] = (qs_g * jnp.exp2(mid)).astype(BF16)
                kd_ref[rg, hs] = (ks_g * jnp.exp2(b_last - mid)).astype(BF16)
                for gp in range(g):
                    key_refs[g][gp * half:(gp + 1) * half, hs] = (
                        ks[gp] * jnp.exp2(mid - mids[gp])).astype(BF16)
                key_refs[g][rg, hs] = ks_g.astype(BF16)

    def scores():
        for h, hs in enumerate(head_slices):
            for g in range(n_half):
                rg = slice(g * half, (g + 1) * half)
                a = _dot_nt(qs_ref[rg, hs], key_refs[g][:, hs])
                a_ref[h, rg, 0:half * (g + 1)] = jnp.where(masks[g], a, 0.0).astype(BF16)

    def outputs_and_state():
        for h, hs in enumerate(head_slices):
            v = i_ref[rows, hs]
            st = st_ref[h]
            if reset is not None:
                st = jnp.where(reset, 0.0, st)
            o_inter = _dot_nt(qe_ref[:, hs], st.astype(BF16))
            for g in range(n_half):
                rg = slice(g * half, (g + 1) * half)
                o_ref[rg, hs] = o_inter[rg] + _dot(a_ref[h, rg, 0:half * (g + 1)], v[0:half * (g + 1)])
            st_ref[h] = st * dec_ref[0:1, hs] + _dot_tn(v, kd_ref[:, hs])

    def normalise():
        for hs in head_slices:
            o = o_ref[:, hs]
            o_n = o * lax.rsqrt(jnp.mean(o * o, axis=-1, keepdims=True) + EPS) * (0.5 * gh_ref[:, hs])
            gate2 = jnp.tanh(0.5 * zo_ref[rows, hs].astype(F32)) + 1.0
            og_ref[og_rows, hs] = (o_n * gate2).astype(og_ref.dtype)

    return [operands, scores, outputs_and_state, normalise]


def _hgrn_body(q_ref, f_ref, i_ref, zo_ref, s0_ref, lb_ref, gh_ref, og_ref, sout_ref, *scratch,
               chunk, n_chunks, heads, streams):
    ti = pl.program_id(2)
    st_ref = scratch[0]

    @pl.when(ti == 0)
    def _():
        for k in range(streams):
            for h in range(heads):
                st_ref[k * heads + h] = s0_ref[k, h].T

    n_set = (len(scratch) - 1) // REC_CHUNKS_PER_STEP
    staging = [tuple(scratch[1 + k * n_set:1 + (k + 1) * n_set]) for k in range(REC_CHUNKS_PER_STEP)]

    def run_alternated(passes):
        for stage in zip(*passes):
            for run_pass in stage:
                run_pass()

    def passes_for(state_ref, rows, k):
        return _hgrn_passes(q_ref, f_ref, i_ref, zo_ref, lb_ref, gh_ref, og_ref, (state_ref,) + staging[k],
                            rows, rows, chunk, heads)

    if streams > 1:
        assert n_chunks == 1 and streams <= REC_CHUNKS_PER_STEP
        run_alternated([passes_for(st_ref.at[pl.ds(k * heads, heads)], slice(k * chunk, (k + 1) * chunk), k)
                        for k in range(streams)])
    else:
        per_step = math.gcd(n_chunks, REC_CHUNKS_PER_STEP)

        def chunk_step(ci, carry):
            run_alternated([passes_for(st_ref, pl.ds(pl.multiple_of((ci * per_step + k) * chunk, chunk), chunk), k)
                            for k in range(per_step)])
            return carry

        lax.fori_loop(0, n_chunks // per_step, chunk_step, 0)

    @pl.when(ti == pl.num_programs(2) - 1)
    def _():
        for k in range(streams):
            for h in range(heads):
                sout_ref[k, h] = st_ref[k * heads + h].T


def _hgrn(z, s0, lb, g_head, *, row_base, batch, seq, d_model, name):
    n_heads = d_model // HEAD_DIM
    heads = min(16, n_heads)
    chunk = min(REC_CHUNK, seq)
    half = min(REC_HALF, chunk)
    assert seq % chunk == 0 and chunk % half == 0
    rt = max(_pick(seq, (512, 256, 128, 64, 32)), chunk)
    tiles = seq // rt
    streams = math.gcd(batch, REC_CHUNKS_PER_STEP) if (tiles == 1 and rt == chunk) else 1
    blk_rows = streams * rt
    assert row_base % blk_rows == 0
    wblk = heads * HEAD_DIM
    cpb = d_model // wblk

    def zspec(section):
        return pl.BlockSpec((blk_rows, wblk),
                            lambda b, hg, i: (row_base // blk_rows + b * tiles + i, section * cpb + hg))

    return pl.pallas_call(
        functools.partial(_hgrn_body, chunk=chunk, n_chunks=rt // chunk, heads=heads, streams=streams),
        grid=(batch // streams, n_heads // heads, tiles),
        in_specs=[zspec(0), zspec(1), zspec(2), zspec(3),
                  pl.BlockSpec((streams, heads, HEAD_DIM, HEAD_DIM), lambda b, hg, i: (b, hg, 0, 0)),
                  pl.BlockSpec((1, wblk), lambda b, hg, i: (0, hg)),
                  pl.BlockSpec((1, wblk), lambda b, hg, i: (0, hg))],
        out_specs=[pl.BlockSpec((blk_rows, wblk), lambda b, hg, i: (b * tiles + i, hg)),
                   pl.BlockSpec((streams, heads, HEAD_DIM, HEAD_DIM), lambda b, hg, i: (b, hg, 0, 0))],
        out_shape=[jax.ShapeDtypeStruct((batch * seq, d_model), BF16),
                   jax.ShapeDtypeStruct((batch, n_heads, HEAD_DIM, HEAD_DIM), F32)],
        scratch_shapes=_hgrn_scratch(chunk, heads, streams),
        compiler_params=pltpu.CompilerParams(dimension_semantics=("parallel", "parallel", "arbitrary"),
                                             vmem_limit_bytes=VMEM_LIMIT),
        name=name,
    )(z, z, z, z, s0, lb, g_head)


def _mix_body(zu_ref, zv_ref, zga_ref, zgb_ref, ogp_ref, ogs_ref, xp_ref, xs_ref, lng_ref, lnb_ref,
              wsp_ref, wss_ref, bsp_ref, bss_ref, wpa_hbm, wpb_hbm, wo_hbm,
              h1_ref, vnp_ref, vns_ref, sg_ref, wpa_ref, wpb_ref, wo_ref, stage_ref, wsem, ya_ref,
              *, n_prompt_tiles, dec_seq):
    @pl.when(pl.program_id(0) == 0)
    def _():
        _load_as_bf16(wpa_hbm, wpa_ref, stage_ref, wsem)
        _load_as_bf16(wpb_hbm, wpb_ref, stage_ref, wsem)
        _load_as_bf16(wo_hbm, wo_ref, stage_ref, wsem)

    is_p = pl.program_id(0) < n_prompt_tiles
    tm, width = zu_ref.shape
    gd = width // MLP_GROUPS
    og = jnp.where(is_p, ogp_ref[...].astype(F32), ogs_ref[...].astype(F32)).astype(BF16)
    n_chunks = tm // MLP_CHUNK
    d = h1_ref.shape[1]
    col = d // (2 * n_chunks)

    def ya_slice(j):
        cs = slice(j * col, (j + 1) * col)
        ya_ref[:, cs] = _dot(og, wpa_ref[:, cs])

    r = lax.broadcasted_iota(jnp.int32, (MLP_CHUNK, MLP_CHUNK), 0)
    c = lax.broadcasted_iota(jnp.int32, (MLP_CHUNK, MLP_CHUNK), 1)
    causal = r >= c
    same_stream = (r // dec_seq) == (c // dec_seq)
    w_mix, bias = [], []
    for g in range(MLP_GROUPS):
        w_p = jnp.where(causal, wsp_ref[g], 0.0)
        w_s = jnp.where(causal & same_stream, wss_ref[g], 0.0)
        w_mix.append(jnp.where(is_p, w_p, w_s).astype(BF16))
        bias.append(jnp.where(is_p, bsp_ref[g], bss_ref[g]))
    for cc in range(n_chunks):
        rows = slice(cc * MLP_CHUNK, (cc + 1) * MLP_CHUNK)
        ya_slice(2 * cc)
        u = _gelu(zu_ref[rows, :].astype(F32))
        gv = _gelu(zv_ref[rows, :].astype(F32))
        xc = gv - jnp.mean(gv, axis=-1, keepdims=True)
        vn = xc * lax.rsqrt(jnp.mean(xc * xc, axis=-1, keepdims=True) + EPS) * lng_ref[...] + lnb_ref[...]
        vnp_ref[rows, :] = vn
        vns_ref[rows, :] = vn
        ya_slice(2 * cc + 1)
        vnb = vn.astype(BF16)
        for g in range(MLP_GROUPS):
            cols = slice(g * gd, (g + 1) * gd)
            s = _dot(w_mix[g], vnb[:, cols]) + bias[g]
            sg_ref[rows, cols] = (u[:, cols] * s).astype(BF16)
    y_b = _dot(sg_ref[...], wpb_ref[...])
    m = _sigmoid(zga_ref[...].astype(F32)) * ya_ref[...] + _sigmoid(zgb_ref[...].astype(F32)) * y_b
    x = jnp.where(is_p, xp_ref[...], xs_ref[...])
    h1_ref[...] = x + _dot(m.astype(BF16), wo_ref[...])


def _mix(z, og_p, og_s, xp, xs, ln_g, ln_b, ws_p, ws_s, bs_p, bs_s, w_pa, w_pb, w_o, *, batch, dec_seq):
    (n_p, d), n_s = xp.shape, xs.shape[0]
    n = n_p + n_s
    width = w_pb.shape[0]
    tm = 256
    assert n_s % tm == 0 and (n_p // batch) % tm == 0 and tm % MLP_CHUNK == 0 and MLP_CHUNK % dec_seq == 0
    npt, nst = n_p // tm, n_s // tm
    tpb = npt // batch
    u_blk = 4 * d // width
    ga_blk = (4 * d + 2 * width) // d
    const2 = lambda i: (0, 0)
    const3 = lambda i: (0, 0, 0)
    p_map = lambda i: (jnp.minimum(i, npt - 1), 0)
    s_map = lambda i: (jnp.clip(i - npt, 0, nst - 1), 0)
    return pl.pallas_call(
        functools.partial(_mix_body, n_prompt_tiles=npt, dec_seq=dec_seq),
        grid=(n // tm,),
        in_specs=[pl.BlockSpec((tm, width), lambda i: (i, u_blk)),
                  pl.BlockSpec((tm, width), lambda i: (i, u_blk + 1)),
                  pl.BlockSpec((tm, d), lambda i: (i, ga_blk)),
                  pl.BlockSpec((tm, d), lambda i: (i, ga_blk + 1)),
                  pl.BlockSpec((tm, d), p_map),
                  pl.BlockSpec((tm, d), s_map),
                  pl.BlockSpec((tm, d), p_map),
                  pl.BlockSpec((tm, d), s_map),
                  pl.BlockSpec((1, width), const2),
                  pl.BlockSpec((1, width), const2),
                  pl.BlockSpec((MLP_GROUPS, MLP_CHUNK, MLP_CHUNK), const3),
                  pl.BlockSpec((MLP_GROUPS, MLP_CHUNK, MLP_CHUNK), const3),
                  pl.BlockSpec((MLP_GROUPS, MLP_CHUNK, 1), const3),
                  pl.BlockSpec((MLP_GROUPS, MLP_CHUNK, 1), const3),
                  pl.BlockSpec(memory_space=pl.ANY),
                  pl.BlockSpec(memory_space=pl.ANY),
                  pl.BlockSpec(memory_space=pl.ANY)],
        out_specs=[pl.BlockSpec((tm, d), lambda i: (i, 0)),
                   pl.BlockSpec((tm, width), lambda i: (jnp.where(i < npt, i // tpb, batch), 0)),
                   pl.BlockSpec((tm, width), lambda i: (jnp.where(i < npt, nst, i - npt), 0))],
        out_shape=[jax.ShapeDtypeStruct((n, d), F32),
                   jax.ShapeDtypeStruct(((batch + 1) * tm, width), F32),
                   jax.ShapeDtypeStruct(((nst + 1) * tm, width), F32)],
        scratch_shapes=[pltpu.VMEM((tm, width), BF16),
                        pltpu.VMEM((d, d), BF16), pltpu.VMEM((width, d), BF16), pltpu.VMEM((d, d), BF16),
                        pltpu.VMEM((WEIGHT_STAGE_ROWS, d), F32), pltpu.SemaphoreType.DMA((2,)),
                        pltpu.VMEM((tm, d), F32)],
        compiler_params=pltpu.CompilerParams(dimension_semantics=("arbitrary",),
                                             vmem_limit_bytes=VMEM_LIMIT),
        name="mix",
    )(z, z, z, z, og_p, og_s, xp, xs, ln_g, ln_b, ws_p, ws_s, bs_p, bs_s, w_pa, w_pb, w_o)


def _router_body(h_ref, g_ref, wr_ref, br_ref, c_ref, idx_ref, gate_ref, cnt_ref, carry_ref):
    @pl.when(pl.program_id(0) == 0)
    def _():
        carry_ref[...] = jnp.zeros_like(carry_ref)

    tm = h_ref.shape[0]
    c = _rms(h_ref[...], g_ref[...])
    wr = wr_ref[...]
    wr_hi = wr.astype(BF16)
    wr_lo = (wr - wr_hi.astype(F32)).astype(BF16)
    c_hi = c.astype(BF16)
    c_lo = (c - c_hi.astype(F32)).astype(BF16)
    _store_chunks(c_ref, _pack_bf16_pairs(c_hi.astype(F32)))
    both = _dot_nt(jnp.concatenate([wr_hi, wr_lo], axis=0), c_hi)
    lt = both[0:ROUTER_ROWS] + both[ROUTER_ROWS:2 * ROUTER_ROWS] + _dot_nt(wr_hi, c_lo) + br_ref[...]
    le = lt[0:N_EXPERTS]
    lg = lt[N_EXPERTS:N_EXPERTS + N_GROUPS]
    gmax = jnp.max(lg, axis=0, keepdims=True)
    p_sel = 1.0 / jnp.sum(jnp.exp(lg - gmax), axis=0, keepdims=True)
    best = lg[0:1]
    gi = jnp.zeros((1, tm), jnp.int32)
    for g in range(1, N_GROUPS):
        better = lg[g:g + 1] > best
        gi = jnp.where(better, g, gi)
        best = jnp.where(better, lg[g:g + 1], best)
    leg = jnp.zeros((E_PER_GROUP, tm), F32)
    for g in range(N_GROUPS):
        leg = jnp.where(gi == g, le[g * E_PER_GROUP:(g + 1) * E_PER_GROUP], leg)
    sub = lax.broadcasted_iota(jnp.int32, (E_PER_GROUP, tm), 0).astype(F32)
    v1 = jnp.max(leg, axis=0, keepdims=True)
    i1 = jnp.min(jnp.where(leg == v1, sub, float(E_PER_GROUP)), axis=0, keepdims=True)
    rest = jnp.where(sub == i1, -jnp.inf, leg)
    v2 = jnp.max(rest, axis=0, keepdims=True)
    i2 = jnp.min(jnp.where(rest == v2, sub, float(E_PER_GROUP)), axis=0, keepdims=True)
    e2 = jnp.exp(v2 - v1)
    den = 1.0 + e2
    gate0 = p_sel * (1.0 / den)
    gate1 = p_sel * (e2 / den)
    ex0 = gi * E_PER_GROUP + i1.astype(jnp.int32)
    ex1 = gi * E_PER_GROUP + i2.astype(jnp.int32)
    eid = lax.broadcasted_iota(jnp.int32, (N_EXPERTS, tm), 0)
    oh0 = eid == ex0
    oh1 = eid == ex1
    oh = jnp.where(oh0 | oh1, 1.0, 0.0)
    upper = jnp.where(lax.broadcasted_iota(jnp.int32, (tm, tm), 0) < lax.broadcasted_iota(jnp.int32, (tm, tm), 1),
                      1.0, 0.0).astype(BF16)
    before = _dot(oh.astype(BF16), upper) + carry_ref[:, 0:1]
    rank0 = jnp.sum(jnp.where(oh0, before, 0.0), axis=0, keepdims=True)
    rank1 = jnp.sum(jnp.where(oh1, before, 0.0), axis=0, keepdims=True)
    carry = carry_ref[...] + jnp.sum(oh, axis=1, keepdims=True)
    carry_ref[...] = carry
    cnt_ref[...] = carry.astype(jnp.int32)
    idx_ref[...] = jnp.zeros_like(idx_ref)
    idx_ref[0:1, :] = ex0
    idx_ref[1:2, :] = ex1
    idx_ref[2:3, :] = rank0.astype(jnp.int32)
    idx_ref[3:4, :] = rank1.astype(jnp.int32)
    gate_ref[...] = jnp.zeros_like(gate_ref)
    gate_ref[0:1, :] = gate0
    gate_ref[1:2, :] = gate1


def _router(h1, g_ffn, wr, br):
    n, d = h1.shape
    tm = _pick(n, (512, 256))
    return pl.pallas_call(
        _router_body,
        grid=(n // tm,),
        in_specs=[pl.BlockSpec((tm, d), lambda i: (i, 0)),
                  pl.BlockSpec((1, d), lambda i: (0, 0)),
                  pl.BlockSpec((ROUTER_ROWS, d), lambda i: (0, 0)),
                  pl.BlockSpec((ROUTER_ROWS, 1), lambda i: (0, 0))],
        out_specs=[pl.BlockSpec((d // (2 * LANES), tm, LANES), lambda i: (0, i, 0)),
                   pl.BlockSpec((SUBLANES, tm), lambda i: (0, i)),
                   pl.BlockSpec((SUBLANES, tm), lambda i: (0, i)),
                   pl.BlockSpec((N_EXPERTS, LANES), lambda i: (0, 0))],
        out_shape=[jax.ShapeDtypeStruct((d // (2 * LANES), n, LANES), jnp.uint32),
                   jax.ShapeDtypeStruct((SUBLANES, n), jnp.int32),
                   jax.ShapeDtypeStruct((SUBLANES, n), F32),
                   jax.ShapeDtypeStruct((N_EXPERTS, LANES), jnp.int32)],
        scratch_shapes=[pltpu.VMEM((N_EXPERTS, LANES), F32)],
        compiler_params=pltpu.CompilerParams(dimension_semantics=("arbitrary",),
                                             vmem_limit_bytes=VMEM_LIMIT),
        name="router",
    )(h1, g_ffn, wr, br)


def _dispatch_body(fill_lo_ref, fill_hi_ref, dest_ref, c_ref, xs_ref, zero_ref, sem, zsem):
    td = c_ref.shape[1]

    @pl.when(pl.program_id(0) == 0)
    def _():
        zero_ref[...] = jnp.zeros_like(zero_ref)

        def per_expert(e, carry):
            lo, hi = fill_lo_ref[e], fill_hi_ref[e]

            def start(r, c2):
                _row_copy(zero_ref, 0, xs_ref, r, zsem).start()
                return c2

            def wait(r, c2):
                _row_copy(zero_ref, 0, xs_ref, 0, zsem).wait()
                return c2

            lax.fori_loop(lo, hi, start, 0)
            lax.fori_loop(lo, hi, wait, 0)
            return carry

        lax.fori_loop(0, N_EXPERTS, per_expert, 0)

        def block_copy(j):
            return _rows_copy(zero_ref, xs_ref, j * MOE_BLOCK, MOE_BLOCK, zsem)

        first_unused = fill_hi_ref[N_EXPERTS - 1] // MOE_BLOCK
        n_blocks = xs_ref.shape[1] // MOE_BLOCK

        def start_block(j, carry):
            block_copy(j).start()
            return carry

        def wait_block(j, carry):
            block_copy(0).wait()
            return carry

        lax.fori_loop(first_unused, n_blocks, start_block, 0)
        lax.fori_loop(first_unused, n_blocks, wait_block, 0)

    for t in range(td):
        for k in range(TOP_K):
            _row_copy(c_ref, t, xs_ref, dest_ref[k, t], sem).start(priority=k % 2)
    for k in range(TOP_K):
        _rows_copy(c_ref, xs_ref, 0, td, sem).wait()


def _dispatch(fill_lo, fill_hi, dest, c, n_slots):
    nc, n, _ = c.shape
    td = _pick(n, (512, 256))
    grid_spec = pltpu.PrefetchScalarGridSpec(
        num_scalar_prefetch=2,
        grid=(n // td,),
        in_specs=[pl.BlockSpec((TOP_K, td), lambda i, lo, hi: (0, i), memory_space=pltpu.SMEM),
                  pl.BlockSpec((nc, td, LANES), lambda i, lo, hi: (0, i, 0))],
        out_specs=pl.BlockSpec(memory_space=pl.ANY),
        scratch_shapes=[pltpu.VMEM((nc, MOE_BLOCK, LANES), c.dtype),
                        pltpu.SemaphoreType.DMA(()), pltpu.SemaphoreType.DMA(())],
    )
    return pl.pallas_call(
        _dispatch_body,
        grid_spec=grid_spec,
        out_shape=jax.ShapeDtypeStruct((nc, n_slots, LANES), c.dtype),
        compiler_params=pltpu.CompilerParams(dimension_semantics=("arbitrary",),
                                             vmem_limit_bytes=VMEM_LIMIT),
        name="dispatch",
    )(fill_lo, fill_hi, dest, c)


def _expert_body(be_ref, na_ref, nxt_ref, xs_ref, wg_hbm, wu_hbm, wd_hbm, yb_ref,
                 wg_ref, wu_ref, wd_ref, sg_ref, su_ref, sd_ref, wsem):
    j = pl.program_id(0)
    active = j < na_ref[0]
    e = be_ref[j]
    first = (j == 0) | (be_ref[jnp.maximum(j - 1, 0)] != e)

    def weight_copies(expert):
        return (pltpu.make_async_copy(wg_hbm.at[expert], sg_ref, wsem.at[0]),
                pltpu.make_async_copy(wu_hbm.at[expert], su_ref, wsem.at[1]),
                pltpu.make_async_copy(wd_hbm.at[expert], sd_ref, wsem.at[2]))

    @pl.when(j == 0)
    def _():
        for cp in weight_copies(e):
            cp.start()

    @pl.when(active & first)
    def _():
        for cp, stage, dst in zip(weight_copies(e), (sg_ref, su_ref, sd_ref), (wg_ref, wu_ref, wd_ref)):
            cp.wait()

            def convert(r, carry, stage=stage, dst=dst):
                rows = pl.ds(pl.multiple_of(r * LANES, LANES), LANES)
                dst[rows, :] = stage[rows, :].astype(BF16)
                return carry

            lax.fori_loop(0, stage.shape[0] // LANES, convert, 0)

        @pl.when(nxt_ref[e] < N_EXPERTS)
        def _():
            for cp in weight_copies(nxt_ref[e]):
                cp.start()

    @pl.when(active)
    def _():
        x = _unpack_bf16_pairs(_load_chunks(xs_ref))
        a = _dot(x, wg_ref[...])
        b = _dot(x, wu_ref[...])
        hid = (a * _sigmoid(a)) * b
        _store_chunks(yb_ref, _dot(hid.astype(BF16), wd_ref[...]))

    @pl.when(jnp.logical_not(active))
    def _():
        yb_ref[...] = jnp.zeros_like(yb_ref)


def _experts(block_exp, n_active, next_exp, xs, w_gate, w_up, w_down):
    nc_in, n_slots, _ = xs.shape
    d, de = w_gate.shape[1:]
    nc = d // LANES
    nb = n_slots // MOE_BLOCK
    assert d % LANES == 0 and de % LANES == 0
    blk = lambda j, be, na, nxt: (0, jnp.minimum(j, na[0] - 1), 0)
    grid_spec = pltpu.PrefetchScalarGridSpec(
        num_scalar_prefetch=3,
        grid=(nb,),
        in_specs=[pl.BlockSpec((nc_in, MOE_BLOCK, LANES), blk),
                  pl.BlockSpec(memory_space=pl.ANY),
                  pl.BlockSpec(memory_space=pl.ANY),
                  pl.BlockSpec(memory_space=pl.ANY)],
        out_specs=pl.BlockSpec((nc, MOE_BLOCK, LANES), lambda j, be, na, nxt: (0, j, 0)),
        scratch_shapes=[pltpu.VMEM((d, de), BF16), pltpu.VMEM((d, de), BF16), pltpu.VMEM((de, d), BF16),
                        pltpu.VMEM((d, de), F32), pltpu.VMEM((d, de), F32), pltpu.VMEM((de, d), F32),
                        pltpu.SemaphoreType.DMA((3,))],
    )
    return pl.pallas_call(
        _expert_body,
        grid_spec=grid_spec,
        out_shape=jax.ShapeDtypeStruct((nc, n_slots, LANES), F32),
        compiler_params=pltpu.CompilerParams(dimension_semantics=("arbitrary",),
                                             vmem_limit_bytes=VMEM_LIMIT),
        name="experts",
    )(block_exp, n_active, next_exp, xs, w_gate, w_up, w_down)


def _combine_body(dcur_ref, dnxt_ref, dnx2_ref, gate_ref, h1_ref, pp_ref, ps_ref, yb_ref, gple_ref, wpg_hbm, wpp_hbm,
                  gfin_ref, yp_ref, ys_ref, buf_ref, sem, wpg_ref, wpp_ref, stage_ref, wsem,
                  *, n_prompt_tiles, final_norm):
    i = pl.program_id(0)

    @pl.when(i == 0)
    def _():
        _load_as_bf16(wpg_hbm, wpg_ref, stage_ref, wsem)
        _load_as_bf16(wpp_hbm, wpp_ref, stage_ref, wsem)

    n_tiles = pl.num_programs(0)
    tc = h1_ref.shape[0]
    slot = lax.rem(i, 3)
    ahead = lax.rem(i + 2, 3)

    @pl.when(i == 0)
    def _():
        def body(t, carry):
            for k in range(TOP_K):
                _row_copy(yb_ref, dcur_ref[k, t], buf_ref.at[0, k], t, sem.at[0]).start()
                _row_copy(yb_ref, dnxt_ref[k, t], buf_ref.at[1, k], t, sem.at[1]).start()
            return carry

        lax.fori_loop(0, tc, body, 0)

    def wait_slot(s):
        for k in range(TOP_K):
            pltpu.make_async_copy(yb_ref.at[:, pl.ds(0, tc), :], buf_ref.at[s, k], sem.at[s]).wait()

    wait_slot(slot)
    gt = gate_ref[...].T
    h2 = h1_ref[...] + (gt[:, 0:1] * _load_chunks(buf_ref.at[slot, 0])
                        + gt[:, 1:2] * _load_chunks(buf_ref.at[slot, 1]))
    for t in range(tc):
        for k in range(TOP_K):
            _row_copy(yb_ref, dnx2_ref[k, t], buf_ref.at[ahead, k], t, sem.at[ahead]).start(priority=k % 2)
    a = _rms(h2, gple_ref[...]).astype(BF16)
    p = jnp.where(i < n_prompt_tiles, pp_ref[...], ps_ref[...]).astype(BF16)
    h3 = h2 + _sigmoid(_dot(a, wpg_ref[...])) * _dot(p, wpp_ref[...])
    if final_norm:
        h3 = _rms(h3, gfin_ref[...])

    @pl.when(i < n_prompt_tiles)
    def _():
        yp_ref[...] = h3

    @pl.when(i >= n_prompt_tiles)
    def _():
        ys_ref[...] = h3

    @pl.when(i == n_tiles - 1)
    def _():
        wait_slot(lax.rem(i + 1, 3))
        wait_slot(ahead)


def _combine(dest, gates, h1, p_p, p_s, yb, g_ple, w_pg, w_pp, g_final, *, n_prompt, final_norm):
    n, d = h1.shape
    ple = p_p.shape[1]
    tc = 256
    assert n_prompt % tc == 0 and (n - n_prompt) % tc == 0
    nt, npt = n // tc, n_prompt // tc
    nst = nt - npt
    assert nt >= 3
    const2 = lambda i: (0, 0)
    return pl.pallas_call(
        functools.partial(_combine_body, n_prompt_tiles=npt, final_norm=final_norm),
        grid=(nt,),
        in_specs=[pl.BlockSpec((TOP_K, tc), lambda i: (0, i), memory_space=pltpu.SMEM),
                  pl.BlockSpec((TOP_K, tc), lambda i: (0, jnp.minimum(i + 1, nt - 1)), memory_space=pltpu.SMEM),
                  pl.BlockSpec((TOP_K, tc), lambda i: (0, jnp.minimum(i + 2, nt - 1)), memory_space=pltpu.SMEM),
                  pl.BlockSpec((SUBLANES, tc), lambda i: (0, i)),
                  pl.BlockSpec((tc, d), lambda i: (i, 0)),
                  pl.BlockSpec((tc, ple), lambda i: (jnp.minimum(i, npt - 1), 0)),
                  pl.BlockSpec((tc, ple), lambda i: (jnp.clip(i - npt, 0, nst - 1), 0)),
                  pl.BlockSpec(memory_space=pl.ANY),
                  pl.BlockSpec((1, d), const2),
                  pl.BlockSpec(memory_space=pl.ANY),
                  pl.BlockSpec(memory_space=pl.ANY),
                  pl.BlockSpec((1, d), const2)],
        out_specs=[pl.BlockSpec((tc, d), lambda i: (jnp.minimum(i, npt - 1), 0)),
                   pl.BlockSpec((tc, d), lambda i: (jnp.clip(i - npt, 0, nst - 1), 0))],
        out_shape=[jax.ShapeDtypeStruct((n_prompt, d), F32),
                   jax.ShapeDtypeStruct((n - n_prompt, d), F32)],
        scratch_shapes=[pltpu.VMEM((3, TOP_K, d // LANES, tc, LANES), F32), pltpu.SemaphoreType.DMA((3,)),
                        pltpu.VMEM((d, d), BF16), pltpu.VMEM((ple, d), BF16),
                        pltpu.VMEM((WEIGHT_STAGE_ROWS, d), F32), pltpu.SemaphoreType.DMA((2,))],
        compiler_params=pltpu.CompilerParams(dimension_semantics=("arbitrary",),
                                             vmem_limit_bytes=VMEM_LIMIT),
        name="combine",
    )(dest, dest, dest, gates, h1, p_p, p_s, yb, g_ple, w_pg, w_pp, g_final)


def _layer(hp, hs, p_p, p_s, s0_sample, lb, batch, seq, dec_batch, dec_seq,
           g_mix, w_in, g_head, w_pa, ln_v_g, ln_v_b, w_s, b_s, w_pb, w_o,
           g_ffn, w_gr, b_gr, w_er, b_er, w_gate, w_up, w_down, g_ple, w_pg, w_pp, g_final, final_norm):
    n_prompt, d = hp.shape
    n = n_prompt + hs.shape[0]
    n_heads = d // HEAD_DIM
    row = lambda a: a.reshape(1, -1).astype(F32)

    z = _in_proj(hp, hs, row(g_mix), w_in)

    lb_row, gh_row = row(lb), row(g_head)
    s0_prompt = jnp.zeros((batch, n_heads, HEAD_DIM, HEAD_DIM), F32)
    og_p, st_p = _hgrn(z, s0_prompt, lb_row, gh_row, row_base=0, batch=batch, seq=seq,
                       d_model=d, name="hgrn_prompt")
    og_s, st_s = _hgrn(z, s0_sample.astype(F32), lb_row, gh_row, row_base=n_prompt, batch=dec_batch,
                       seq=dec_seq, d_model=d, name="hgrn_sample")

    start = PAST_LEN % MLP_CHUNK
    assert start + dec_seq <= MLP_CHUNK
    rep = MLP_CHUNK // dec_seq
    ws_s = jnp.tile(w_s[:, start:start + dec_seq, start:start + dec_seq], (1, rep, rep))
    bs_s = jnp.tile(b_s[:, start:start + dec_seq], (1, rep))
    h1, vn_p, vn_s = _mix(z, og_p, og_s, hp, hs, row(ln_v_g), row(ln_v_b), w_s, ws_s, b_s[..., None],
                          bs_s[..., None], w_pa, w_pb, w_o,
                          batch=batch, dec_seq=dec_seq)

    pad_rows = ROUTER_ROWS - N_EXPERTS - N_GROUPS
    wr = jnp.concatenate([w_er.T, w_gr.T, jnp.zeros((pad_rows, d), F32)], axis=0)
    br = jnp.concatenate([b_er, b_gr, jnp.zeros((pad_rows,), F32)]).reshape(ROUTER_ROWS, 1).astype(F32)
    c, idx, gates, cnt = _router(h1, row(g_ffn), wr, br)

    counts = cnt[:, 0]
    padded = (counts + MOE_BLOCK - 1) // MOE_BLOCK * MOE_BLOCK
    pad_end = jnp.cumsum(padded).astype(jnp.int32)
    pad_start = pad_end - padded
    n_blocks = -(-(n * TOP_K) // MOE_BLOCK) + N_EXPERTS
    block_first = jnp.arange(n_blocks, dtype=jnp.int32) * MOE_BLOCK
    block_exp = jnp.minimum(jnp.sum(pad_end[None, :] <= block_first[:, None], axis=1), N_EXPERTS - 1).astype(jnp.int32)
    n_active = jnp.maximum(pad_end[-1:] // MOE_BLOCK, 1)
    hit = idx[0:TOP_K, :, None] == jnp.arange(N_EXPERTS, dtype=jnp.int32)
    dest = jnp.sum(jnp.where(hit, pad_start, 0), axis=-1) + idx[TOP_K:2 * TOP_K]

    xs = _dispatch(pad_start + counts, pad_end, dest, c, n_blocks * MOE_BLOCK)
    used = jnp.where(counts > 0, jnp.arange(N_EXPERTS, dtype=jnp.int32), N_EXPERTS)
    next_exp = jnp.concatenate([lax.cummin(used, axis=0, reverse=True)[1:],
                                jnp.full((1,), N_EXPERTS, jnp.int32)]).astype(jnp.int32)
    yb = _experts(block_exp, n_active, next_exp, xs, w_gate, w_up, w_down)
    yp, ys = _combine(dest, gates, h1, p_p, p_s, yb, row(g_ple), w_pg, w_pp,
                      row(g_final), n_prompt=n_prompt, final_norm=final_norm)
    return yp, ys, st_p, st_s, vn_p, vn_s


def kernel(x_prompt, x_sample, p_prompt, p_sample, state_hgrn, g_mix, w_in, lb_logits, g_head, w_pa, ln_v_g, ln_v_b, w_s, b_s, w_pb, w_o, g_ffn, w_gr, b_gr, w_er, b_er, w_gate, w_up, w_down, g_ple, w_pg, w_pp, g_final):
    batch, seq, d = x_prompt.shape
    dec_batch, dec_seq, _ = x_sample.shape
    depth = g_mix.shape[0]
    n_prompt, n_sample = batch * seq, dec_batch * dec_seq
    width = w_pb.shape[1]
    lbs = jnp.cumsum(jax.nn.softmax(lb_logits.astype(F32), axis=0), axis=0)
    hp, hs = x_prompt.reshape(n_prompt, d), x_sample.reshape(n_sample, d)
    keep = min((seq - 1) % MLP_CHUNK + 1, seq)
    keep_s = min((PAST_LEN % MLP_CHUNK + dec_seq - 1) % MLP_CHUNK + 1, dec_seq)
    sp, ss, vp, vs = [], [], [], []
    for i in range(depth):
        p_p, p_s = p_prompt[i].reshape(n_prompt, -1), p_sample[i].reshape(n_sample, -1)
        hp, hs, st_p, st_s, vn_p, vn_s = _layer(
            hp, hs, p_p, p_s, state_hgrn[i], lbs[i], batch, seq, dec_batch, dec_seq,
            g_mix[i], w_in[i], g_head[i], w_pa[i], ln_v_g[i], ln_v_b[i], w_s[i], b_s[i], w_pb[i], w_o[i],
            g_ffn[i], w_gr[i], b_gr[i], w_er[i], b_er[i], w_gate[i], w_up[i], w_down[i],
            g_ple[i], w_pg[i], w_pp[i], g_final, i == depth - 1)
        sp.append(st_p.astype(x_prompt.dtype))
        ss.append(st_s.astype(state_hgrn.dtype))
        vn_last = vn_p.reshape(batch + 1, -1, width)[:batch]
        vp.append(vn_last[:, vn_last.shape[1] - keep:])
        vs.append(vn_s[:n_sample].reshape(dec_batch, dec_seq, width)[:, dec_seq - keep_s:])
    y_prompt = hp.reshape(batch, seq, d)
    y_sample = hs.reshape(dec_batch, dec_seq, d)
    return (y_prompt, y_sample, jnp.stack(sp), jnp.stack(ss), jnp.stack(vp), jnp.stack(vs))
```

```python
import functools
import math

import jax
import jax.numpy as jnp
from jax import lax
from jax.experimental import pallas as pl
from jax.experimental.pallas import tpu as pltpu

F32 = jnp.float32
BF16 = jnp.bfloat16

EPS = 1e-6
HEAD_DIM = 128
REC_CHUNK = 64
REC_HALF = 32
REC_CHUNKS_PER_STEP = 4
MLP_CHUNK = 128
MLP_GROUPS = 4
N_GROUPS = 4
E_PER_GROUP = 8
N_EXPERTS = N_GROUPS * E_PER_GROUP
TOP_K = 2
MOE_BLOCK = 256
PAST_LEN = 2048
LANES = 128
SUBLANES = 8
ROUTER_ROWS = 48
VMEM_LIMIT = 56 * 1024 * 1024
LOG2_E = 1.4426950408889634
WEIGHT_STAGE_ROWS = 256


def _pick(n, candidates):
    for c in candidates:
        if n % c == 0:
            return c
    raise ValueError(f"no tile in {candidates} divides {n}")


def _dot(a, b):
    return jnp.dot(a, b, preferred_element_type=F32)


def _dot_nt(a, b, precision=None):
    return lax.dot_general(a, b, (((1,), (1,)), ((), ())), precision=precision, preferred_element_type=F32)


def _dot_tn(a, b):
    return lax.dot_general(a, b, (((0,), (0,)), ((), ())), preferred_element_type=F32)


def _rms(x, g):
    return x * lax.rsqrt(jnp.mean(x * x, axis=-1, keepdims=True) + EPS) * g


def _gelu(x):
    c = math.sqrt(2.0 / math.pi)
    return x * (0.5 * (1.0 + jnp.tanh(c * (x + 0.044715 * (x * x * x)))))


def _sigmoid(x):
    return 0.5 * jnp.tanh(0.5 * x) + 0.5


def _store_chunks(ref, x):
    for c in range(ref.shape[0]):
        ref[c] = x[:, c * LANES:(c + 1) * LANES].astype(ref.dtype)


def _load_chunks(ref):
    return jnp.concatenate([ref[c] for c in range(ref.shape[0])], axis=1)


HIGH_HALF = 0xFFFF0000


def _pack_bf16_pairs(x):
    half = x.shape[1] // 2
    lo = jnp.right_shift(pltpu.bitcast(x[:, :half], jnp.uint32), jnp.uint32(16))
    hi = jnp.bitwise_and(pltpu.bitcast(x[:, half:], jnp.uint32), jnp.uint32(HIGH_HALF))
    return jnp.bitwise_or(lo, hi)


def _unpack_bf16_pairs(p):
    lo = pltpu.bitcast(jnp.left_shift(p, jnp.uint32(16)), F32)
    hi = pltpu.bitcast(jnp.bitwise_and(p, jnp.uint32(HIGH_HALF)), F32)
    return jnp.concatenate([lo, hi], axis=1).astype(BF16)


def _row_copy(src_ref, src_row, dst_ref, dst_row, sem):
    return pltpu.make_async_copy(src_ref.at[:, pl.ds(src_row, 1), :], dst_ref.at[:, pl.ds(dst_row, 1), :], sem)


def _rows_copy(src_ref, dst_ref, dst_row, n_rows, sem):
    return pltpu.make_async_copy(src_ref, dst_ref.at[:, pl.ds(dst_row, n_rows), :], sem)


def _load_as_bf16(w_hbm, w_vmem, stage_ref, sem):
    half = stage_ref.shape[0] // 2
    rows, cols = min(half, w_hbm.shape[0]), min(stage_ref.shape[1], w_hbm.shape[1])
    assert w_hbm.shape[0] % rows == 0 and w_hbm.shape[1] % cols == 0
    chunks = [(r, c) for r in range(0, w_hbm.shape[0], rows) for c in range(0, w_hbm.shape[1], cols)]

    def slot(i):
        return stage_ref.at[pl.ds((i % 2) * half, rows), pl.ds(0, cols)]

    def copy(i):
        r, c = chunks[i]
        return pltpu.make_async_copy(w_hbm.at[pl.ds(r, rows), pl.ds(c, cols)], slot(i), sem.at[i % 2])

    copy(0).start()
    for i, (r, c) in enumerate(chunks):
        if i + 1 < len(chunks):
            copy(i + 1).start()
        copy(i).wait()
        w_vmem[pl.ds(r, rows), pl.ds(c, cols)] = slot(i)[...].astype(BF16)


def _in_proj_body(xp_ref, xs_ref, g_ref, w_ref, z_ref, xn_ref, *, n_prompt_tiles):
    first = pl.program_id(1) == 0
    is_p = pl.program_id(0) < n_prompt_tiles

    @pl.when(first & is_p)
    def _():
        xn_ref[...] = _rms(xp_ref[...], g_ref[...]).astype(BF16)

    @pl.when(first & jnp.logical_not(is_p))
    def _():
        xn_ref[...] = _rms(xs_ref[...], g_ref[...]).astype(BF16)

    z_ref[...] = _dot(xn_ref[...], w_ref[...].astype(BF16)).astype(z_ref.dtype)


def _in_proj(xp, xs, g, w):
    (n_p, d), n_s = xp.shape, xs.shape[0]
    width = w.shape[1]
    tm = _pick(math.gcd(n_p, n_s), (1024, 512, 256))
    tn = _pick(width, (1024, 512, 256, 128))
    npt, nst = n_p // tm, n_s // tm
    return pl.pallas_call(
        functools.partial(_in_proj_body, n_prompt_tiles=npt),
        grid=(npt + nst, width // tn),
        in_specs=[pl.BlockSpec((tm, d), lambda i, j: (jnp.minimum(i, npt - 1), 0)),
                  pl.BlockSpec((tm, d), lambda i, j: (jnp.clip(i - npt, 0, nst - 1), 0),
                               pipeline_mode=pl.Buffered(1)),
                  pl.BlockSpec((1, d), lambda i, j: (0, 0)),
                  pl.BlockSpec((d, tn), lambda i, j: (0, j))],
        out_specs=pl.BlockSpec((tm, tn), lambda i, j: (i, j)),
        out_shape=jax.ShapeDtypeStruct((n_p + n_s, width), BF16),
        scratch_shapes=[pltpu.VMEM((tm, d), BF16)],
        compiler_params=pltpu.CompilerParams(dimension_semantics=("parallel", "arbitrary"),
                                             vmem_limit_bytes=VMEM_LIMIT),
        name="in_proj",
    )(xp, xs, g, w)


def _cumsum_rows(x, tril3):
    hi = x.astype(BF16)
    r1 = x - hi.astype(F32)
    mid = r1.astype(BF16)
    lo = (r1 - mid.astype(F32)).astype(BF16)
    return _dot(tril3, jnp.concatenate([hi, mid, lo], axis=0))


def _hgrn_scratch(chunk, heads, streams):
    half = min(REC_HALF, chunk)
    wblk = heads * HEAD_DIM
    per_chunk = ([pltpu.VMEM((chunk, wblk), BF16),
                  pltpu.VMEM((chunk, wblk), BF16),
                  pltpu.VMEM((chunk, wblk), BF16),
                  pltpu.VMEM((SUBLANES, wblk), F32),
                  pltpu.VMEM((heads, chunk, chunk), BF16),
                  pltpu.VMEM((chunk, wblk), F32)]
                 + [pltpu.VMEM((half * (g + 1), wblk), BF16) for g in range(chunk // half)])
    return [pltpu.VMEM((streams * heads, HEAD_DIM, HEAD_DIM), F32)] + per_chunk * REC_CHUNKS_PER_STEP


def _hgrn_passes(q_ref, f_ref, i_ref, zo_ref, lb_ref, gh_ref, og_ref, scratch, rows, og_rows, chunk, heads,
                 reset=None):
    st_ref, qs_ref, qe_ref, kd_ref, dec_ref, a_ref, o_ref = scratch[:7]
    key_refs = scratch[7:]
    half = min(REC_HALF, chunk)
    n_half = chunk // half
    tril = (lax.broadcasted_iota(jnp.int32, (chunk, chunk), 0)
            >= lax.broadcasted_iota(jnp.int32, (chunk, chunk), 1))
    tril = jnp.where(tril, 1.0, 0.0).astype(BF16)
    tril3 = jnp.concatenate([tril, tril, tril], axis=1)
    masks = []
    for g in range(n_half):
        r = lax.broadcasted_iota(jnp.int32, (half, half * (g + 1)), 0)
        c = lax.broadcasted_iota(jnp.int32, (half, half * (g + 1)), 1)
        masks.append(r + g * half >= c)
    head_slices = [slice(h * HEAD_DIM, (h + 1) * HEAD_DIM) for h in range(heads)]

    def operands():
        for hs in head_slices:
            q = q_ref[rows, hs].astype(F32)
            lb = lb_ref[:, hs]
            c1 = 0.5 * (1.0 - lb)
            f = (lb + c1) + c1 * jnp.tanh(0.5 * f_ref[rows, hs].astype(F32))
            kk = 1.0 - f
            b = _cumsum_rows(jnp.log(f), tril3) * LOG2_E
            b_last = b[chunk - 1:chunk, :]
            dec_ref[0:1, hs] = jnp.exp2(b_last)
            mids, ks = [], []
            for g in range(n_half):
                rg = slice(g * half, (g + 1) * half)
                mid = b[g * half + half // 2 - 1:g * half + half // 2, :]
                qs_g = q[rg] * jnp.exp2(b[rg] - mid)
                ks_g = kk[rg] * jnp.exp2(mid - b[rg])
                mids.append(mid)
                ks.append(ks_g)
                qs_ref[rg, hs] = qs_g.astype(BF16)
                qe_ref[rg, hs] = (qs_g * jnp.exp2(mid)).astype(BF16)
                kd_ref[rg, hs] = (ks_g * jnp.exp2(b_last - mid)).astype(BF16)
                for gp in range(g):
                    key_refs[g][gp * half:(gp + 1) * half, hs] = (
                        ks[gp] * jnp.exp2(mid - mids[gp])).astype(BF16)
                key_refs[g][rg, hs] = ks_g.astype(BF16)

    def scores():
        for h, hs in enumerate(head_slices):
            for g in range(n_half):
                rg = slice(g * half, (g + 1) * half)
                a = _dot_nt(qs_ref[rg, hs], key_refs[g][:, hs])
                a_ref[h, rg, 0:half * (g + 1)] = jnp.where(masks[g], a, 0.0).astype(BF16)

    def outputs_and_state():
        for h, hs in enumerate(head_slices):
            v = i_ref[rows, hs]
            st = st_ref[h]
            if reset is not None:
                st = jnp.where(reset, 0.0, st)
            o_inter = _dot_nt(qe_ref[:, hs], st.astype(BF16))
            for g in range(n_half):
                rg = slice(g * half, (g + 1) * half)
                o_ref[rg, hs] = o_inter[rg] + _dot(a_ref[h, rg, 0:half * (g + 1)], v[0:half * (g + 1)])
            st_ref[h] = st * dec_ref[0:1, hs] + _dot_tn(v, kd_ref[:, hs])

    def normalise():
        for hs in head_slices:
            o = o_ref[:, hs]
            o_n = o * lax.rsqrt(jnp.mean(o * o, axis=-1, keepdims=True) + EPS) * (0.5 * gh_ref[:, hs])
            gate2 = jnp.tanh(0.5 * zo_ref[rows, hs].astype(F32)) + 1.0
            og_ref[og_rows, hs] = (o_n * gate2).astype(og_ref.dtype)

    return [operands, scores, outputs_and_state, normalise]


def _hgrn_body(q_ref, f_ref, i_ref, zo_ref, s0_ref, lb_ref, gh_ref, og_ref, sout_ref, *scratch,
               chunk, n_chunks, heads, streams):
    ti = pl.program_id(2)
    st_ref = scratch[0]

    @pl.when(ti == 0)
    def _():
        for k in range(streams):
            for h in range(heads):
                st_ref[k * heads + h] = s0_ref[k, h].T

    n_set = (len(scratch) - 1) // REC_CHUNKS_PER_STEP
    staging = [tuple(scratch[1 + k * n_set:1 + (k + 1) * n_set]) for k in range(REC_CHUNKS_PER_STEP)]

    def run_alternated(passes):
        for stage in zip(*passes):
            for run_pass in stage:
                run_pass()

    def passes_for(state_ref, rows, k):
        return _hgrn_passes(q_ref, f_ref, i_ref, zo_ref, lb_ref, gh_ref, og_ref, (state_ref,) + staging[k],
                            rows, rows, chunk, heads)

    if streams > 1:
        assert n_chunks == 1 and streams <= REC_CHUNKS_PER_STEP
        run_alternated([passes_for(st_ref.at[pl.ds(k * heads, heads)], slice(k * chunk, (k + 1) * chunk), k)
                        for k in range(streams)])
    else:
        per_step = math.gcd(n_chunks, REC_CHUNKS_PER_STEP)

        def chunk_step(ci, carry):
            run_alternated([passes_for(st_ref, pl.ds(pl.multiple_of((ci * per_step + k) * chunk, chunk), chunk), k)
                            for k in range(per_step)])
            return carry

        lax.fori_loop(0, n_chunks // per_step, chunk_step, 0)

    @pl.when(ti == pl.num_programs(2) - 1)
    def _():
        for k in range(streams):
            for h in range(heads):
                sout_ref[k, h] = st_ref[k * heads + h].T


def _hgrn(z, s0, lb, g_head, *, row_base, batch, seq, d_model, name):
    n_heads = d_model // HEAD_DIM
    heads = min(16, n_heads)
    chunk = min(REC_CHUNK, seq)
    half = min(REC_HALF, chunk)
    assert seq % chunk == 0 and chunk % half == 0
    rt = max(_pick(seq, (512, 256, 128, 64, 32)), chunk)
    tiles = seq // rt
    streams = math.gcd(batch, REC_CHUNKS_PER_STEP) if (tiles == 1 and rt == chunk) else 1
    blk_rows = streams * rt
    assert row_base % blk_rows == 0
    wblk = heads * HEAD_DIM
    cpb = d_model // wblk

    def zspec(section):
        return pl.BlockSpec((blk_rows, wblk),
                            lambda b, hg, i: (row_base // blk_rows + b * tiles + i, section * cpb + hg))

    return pl.pallas_call(
        functools.partial(_hgrn_body, chunk=chunk, n_chunks=rt // chunk, heads=heads, streams=streams),
        grid=(batch // streams, n_heads // heads, tiles),
        in_specs=[zspec(0), zspec(1), zspec(2), zspec(3),
                  pl.BlockSpec((streams, heads, HEAD_DIM, HEAD_DIM), lambda b, hg, i: (b, hg, 0, 0)),
                  pl.BlockSpec((1, wblk), lambda b, hg, i: (0, hg)),
                  pl.BlockSpec((1, wblk), lambda b, hg, i: (0, hg))],
        out_specs=[pl.BlockSpec((blk_rows, wblk), lambda b, hg, i: (b * tiles + i, hg)),
                   pl.BlockSpec((streams, heads, HEAD_DIM, HEAD_DIM), lambda b, hg, i: (b, hg, 0, 0))],
        out_shape=[jax.ShapeDtypeStruct((batch * seq, d_model), BF16),
                   jax.ShapeDtypeStruct((batch, n_heads, HEAD_DIM, HEAD_DIM), F32)],
        scratch_shapes=_hgrn_scratch(chunk, heads, streams),
        compiler_params=pltpu.CompilerParams(dimension_semantics=("parallel", "parallel", "arbitrary"),
                                             vmem_limit_bytes=VMEM_LIMIT),
        name=name,
    )(z, z, z, z, s0, lb, g_head)


def _mix_body(zu_ref, zv_ref, zga_ref, zgb_ref, ogp_ref, ogs_ref, xp_ref, xs_ref, lng_ref, lnb_ref,
              wsp_ref, wss_ref, bsp_ref, bss_ref, wpa_hbm, wpb_hbm, wo_hbm,
              h1_ref, vnp_ref, vns_ref, sg_ref, wpa_ref, wpb_ref, wo_ref, stage_ref, wsem, ya_ref,
              *, n_prompt_tiles, dec_seq):
    @pl.when(pl.program_id(0) == 0)
    def _():
        _load_as_bf16(wpa_hbm, wpa_ref, stage_ref, wsem)
        _load_as_bf16(wpb_hbm, wpb_ref, stage_ref, wsem)
        _load_as_bf16(wo_hbm, wo_ref, stage_ref, wsem)

    is_p = pl.program_id(0) < n_prompt_tiles
    tm, width = zu_ref.shape
    gd = width // MLP_GROUPS
    og = jnp.where(is_p, ogp_ref[...].astype(F32), ogs_ref[...].astype(F32)).astype(BF16)
    n_chunks = tm // MLP_CHUNK
    d = h1_ref.shape[1]
    col = d // (2 * n_chunks)

    def ya_slice(j):
        cs = slice(j * col, (j + 1) * col)
        ya_ref[:, cs] = _dot(og, wpa_ref[:, cs])

    r = lax.broadcasted_iota(jnp.int32, (MLP_CHUNK, MLP_CHUNK), 0)
    c = lax.broadcasted_iota(jnp.int32, (MLP_CHUNK, MLP_CHUNK), 1)
    causal = r >= c
    same_stream = (r // dec_seq) == (c // dec_seq)
    w_mix, bias = [], []
    for g in range(MLP_GROUPS):
        w_p = jnp.where(causal, wsp_ref[g], 0.0)
        w_s = jnp.where(causal & same_stream, wss_ref[g], 0.0)
        w_mix.append(jnp.where(is_p, w_p, w_s).astype(BF16))
        bias.append(jnp.where(is_p, bsp_ref[g], bss_ref[g]))
    for cc in range(n_chunks):
        rows = slice(cc * MLP_CHUNK, (cc + 1) * MLP_CHUNK)
        ya_slice(2 * cc)
        u = _gelu(zu_ref[rows, :].astype(F32))
        gv = _gelu(zv_ref[rows, :].astype(F32))
        xc = gv - jnp.mean(gv, axis=-1, keepdims=True)
        vn = xc * lax.rsqrt(jnp.mean(xc * xc, axis=-1, keepdims=True) + EPS) * lng_ref[...] + lnb_ref[...]
        vnp_ref[rows, :] = vn
        vns_ref[rows, :] = vn
        ya_slice(2 * cc + 1)
        vnb = vn.astype(BF16)
        for g in range(MLP_GROUPS):
            cols = slice(g * gd, (g + 1) * gd)
            s = _dot(w_mix[g], vnb[:, cols]) + bias[g]
            sg_ref[rows, cols] = (u[:, cols] * s).astype(BF16)
    y_b = _dot(sg_ref[...], wpb_ref[...])
    m = _sigmoid(zga_ref[...].astype(F32)) * ya_ref[...] + _sigmoid(zgb_ref[...].astype(F32)) * y_b
    x = jnp.where(is_p, xp_ref[...], xs_ref[...])
    h1_ref[...] = x + _dot(m.astype(BF16), wo_ref[...])


def _mix(z, og_p, og_s, xp, xs, ln_g, ln_b, ws_p, ws_s, bs_p, bs_s, w_pa, w_pb, w_o, *, batch, dec_seq):
    (n_p, d), n_s = xp.shape, xs.shape[0]
    n = n_p + n_s
    width = w_pb.shape[0]
    tm = 256
    assert n_s % tm == 0 and (n_p // batch) % tm == 0 and tm % MLP_CHUNK == 0 and MLP_CHUNK % dec_seq == 0
    npt, nst = n_p // tm, n_s // tm
    tpb = npt // batch
    u_blk = 4 * d // width
    ga_blk = (4 * d + 2 * width) // d
    const2 = lambda i: (0, 0)
    const3 = lambda i: (0, 0, 0)
    p_map = lambda i: (jnp.minimum(i, npt - 1), 0)
    s_map = lambda i: (jnp.clip(i - npt, 0, nst - 1), 0)
    return pl.pallas_call(
        functools.partial(_mix_body, n_prompt_tiles=npt, dec_seq=dec_seq),
        grid=(n // tm,),
        in_specs=[pl.BlockSpec((tm, width), lambda i: (i, u_blk)),
                  pl.BlockSpec((tm, width), lambda i: (i, u_blk + 1)),
                  pl.BlockSpec((tm, d), lambda i: (i, ga_blk)),
                  pl.BlockSpec((tm, d), lambda i: (i, ga_blk + 1)),
                  pl.BlockSpec((tm, d), p_map),
                  pl.BlockSpec((tm, d), s_map),
                  pl.BlockSpec((tm, d), p_map),
                  pl.BlockSpec((tm, d), s_map),
                  pl.BlockSpec((1, width), const2),
                  pl.BlockSpec((1, width), const2),
                  pl.BlockSpec((MLP_GROUPS, MLP_CHUNK, MLP_CHUNK), const3),
                  pl.BlockSpec((MLP_GROUPS, MLP_CHUNK, MLP_CHUNK), const3),
                  pl.BlockSpec((MLP_GROUPS, MLP_CHUNK, 1), const3),
                  pl.BlockSpec((MLP_GROUPS, MLP_CHUNK, 1), const3),
                  pl.BlockSpec(memory_space=pl.ANY),
                  pl.BlockSpec(memory_space=pl.ANY),
                  pl.BlockSpec(memory_space=pl.ANY)],
        out_specs=[pl.BlockSpec((tm, d), lambda i: (i, 0)),
                   pl.BlockSpec((tm, width), lambda i: (jnp.where(i < npt, i // tpb, batch), 0)),
                   pl.BlockSpec((tm, width), lambda i: (jnp.where(i < npt, nst, i - npt), 0))],
        out_shape=[jax.ShapeDtypeStruct((n, d), F32),
                   jax.ShapeDtypeStruct(((batch + 1) * tm, width), F32),
                   jax.ShapeDtypeStruct(((nst + 1) * tm, width), F32)],
        scratch_shapes=[pltpu.VMEM((tm, width), BF16),
                        pltpu.VMEM((d, d), BF16), pltpu.VMEM((width, d), BF16), pltpu.VMEM((d, d), BF16),
                        pltpu.VMEM((WEIGHT_STAGE_ROWS, d), F32), pltpu.SemaphoreType.DMA((2,)),
                        pltpu.VMEM((tm, d), F32)],
        compiler_params=pltpu.CompilerParams(dimension_semantics=("arbitrary",),
                                             vmem_limit_bytes=VMEM_LIMIT),
        name="mix",
    )(z, z, z, z, og_p, og_s, xp, xs, ln_g, ln_b, ws_p, ws_s, bs_p, bs_s, w_pa, w_pb, w_o)


def _router_body(h_ref, g_ref, wr_ref, br_ref, c_ref, idx_ref, gate_ref, cnt_ref, carry_ref):
    @pl.when(pl.program_id(0) == 0)
    def _():
        carry_ref[...] = jnp.zeros_like(carry_ref)

    tm = h_ref.shape[0]
    c = _rms(h_ref[...], g_ref[...])
    wr = wr_ref[...]
    wr_hi = wr.astype(BF16)
    wr_lo = (wr - wr_hi.astype(F32)).astype(BF16)
    c_hi = c.astype(BF16)
    c_lo = (c - c_hi.astype(F32)).astype(BF16)
    _store_chunks(c_ref, _pack_bf16_pairs(c_hi.astype(F32)))
    both = _dot_nt(jnp.concatenate([wr_hi, wr_lo], axis=0), c_hi)
    lt = both[0:ROUTER_ROWS] + both[ROUTER_ROWS:2 * ROUTER_ROWS] + _dot_nt(wr_hi, c_lo) + br_ref[...]
    le = lt[0:N_EXPERTS]
    lg = lt[N_EXPERTS:N_EXPERTS + N_GROUPS]
    gmax = jnp.max(lg, axis=0, keepdims=True)
    p_sel = 1.0 / jnp.sum(jnp.exp(lg - gmax), axis=0, keepdims=True)
    best = lg[0:1]
    gi = jnp.zeros((1, tm), jnp.int32)
    for g in range(1, N_GROUPS):
        better = lg[g:g + 1] > best
        gi = jnp.where(better, g, gi)
        best = jnp.where(better, lg[g:g + 1], best)
    leg = jnp.zeros((E_PER_GROUP, tm), F32)
    for g in range(N_GROUPS):
        leg = jnp.where(gi == g, le[g * E_PER_GROUP:(g + 1) * E_PER_GROUP], leg)
    sub = lax.broadcasted_iota(jnp.int32, (E_PER_GROUP, tm), 0).astype(F32)
    v1 = jnp.max(leg, axis=0, keepdims=True)
    i1 = jnp.min(jnp.where(leg == v1, sub, float(E_PER_GROUP)), axis=0, keepdims=True)
    rest = jnp.where(sub == i1, -jnp.inf, leg)
    v2 = jnp.max(rest, axis=0, keepdims=True)
    i2 = jnp.min(jnp.where(rest == v2, sub, float(E_PER_GROUP)), axis=0, keepdims=True)
    e2 = jnp.exp(v2 - v1)
    den = 1.0 + e2
    gate0 = p_sel * (1.0 / den)
    gate1 = p_sel * (e2 / den)
    ex0 = gi * E_PER_GROUP + i1.astype(jnp.int32)
    ex1 = gi * E_PER_GROUP + i2.astype(jnp.int32)
    eid = lax.broadcasted_iota(jnp.int32, (N_EXPERTS, tm), 0)
    oh0 = eid == ex0
    oh1 = eid == ex1
    oh = jnp.where(oh0 | oh1, 1.0, 0.0)
    upper = jnp.where(lax.broadcasted_iota(jnp.int32, (tm, tm), 0) < lax.broadcasted_iota(jnp.int32, (tm, tm), 1),
                      1.0, 0.0).astype(BF16)
    before = _dot(oh.astype(BF16), upper) + carry_ref[:, 0:1]
    rank0 = jnp.sum(jnp.where(oh0, before, 0.0), axis=0, keepdims=True)
    rank1 = jnp.sum(jnp.where(oh1, before, 0.0), axis=0, keepdims=True)
    carry = carry_ref[...] + jnp.sum(oh, axis=1, keepdims=True)
    carry_ref[...] = carry
    cnt_ref[...] = carry.astype(jnp.int32)
    idx_ref[...] = jnp.zeros_like(idx_ref)
    idx_ref[0:1, :] = ex0
    idx_ref[1:2, :] = ex1
    idx_ref[2:3, :] = rank0.astype(jnp.int32)
    idx_ref[3:4, :] = rank1.astype(jnp.int32)
    gate_ref[...] = jnp.zeros_like(gate_ref)
    gate_ref[0:1, :] = gate0
    gate_ref[1:2, :] = gate1


def _router(h1, g_ffn, wr, br):
    n, d = h1.shape
    tm = _pick(n, (512, 256))
    return pl.pallas_call(
        _router_body,
        grid=(n // tm,),
        in_specs=[pl.BlockSpec((tm, d), lambda i: (i, 0)),
                  pl.BlockSpec((1, d), lambda i: (0, 0)),
                  pl.BlockSpec((ROUTER_ROWS, d), lambda i: (0, 0)),
                  pl.BlockSpec((ROUTER_ROWS, 1), lambda i: (0, 0))],
        out_specs=[pl.BlockSpec((d // (2 * LANES), tm, LANES), lambda i: (0, i, 0)),
                   pl.BlockSpec((SUBLANES, tm), lambda i: (0, i)),
                   pl.BlockSpec((SUBLANES, tm), lambda i: (0, i)),
                   pl.BlockSpec((N_EXPERTS, LANES), lambda i: (0, 0))],
        out_shape=[jax.ShapeDtypeStruct((d // (2 * LANES), n, LANES), jnp.uint32),
                   jax.ShapeDtypeStruct((SUBLANES, n), jnp.int32),
                   jax.ShapeDtypeStruct((SUBLANES, n), F32),
                   jax.ShapeDtypeStruct((N_EXPERTS, LANES), jnp.int32)],
        scratch_shapes=[pltpu.VMEM((N_EXPERTS, LANES), F32)],
        compiler_params=pltpu.CompilerParams(dimension_semantics=("arbitrary",),
                                             vmem_limit_bytes=VMEM_LIMIT),
        name="router",
    )(h1, g_ffn, wr, br)


def _dispatch_body(fill_lo_ref, fill_hi_ref, dest_ref, c_ref, xs_ref, zero_ref, sem, zsem):
    td = c_ref.shape[1]

    @pl.when(pl.program_id(0) == 0)
    def _():
        zero_ref[...] = jnp.zeros_like(zero_ref)

        sizes = [1 << b for b in reversed(range(MOE_BLOCK.bit_length() - 1))]

        def zero_rows(n_rows):
            return zero_ref.at[:, pl.ds(0, n_rows), :]

        def per_expert(e, carry):
            lo, hi = fill_lo_ref[e], fill_hi_ref[e]
            n_fill = hi - lo
            pos = lo
            for size in sizes:
                take = jnp.bitwise_and(n_fill, size) != 0

                @pl.when(take)
                def _(pos=pos, size=size):
                    _rows_copy(zero_rows(size), xs_ref, pos, size, zsem).start()

                pos = pos + jnp.where(take, size, 0)
            for size in sizes:
                @pl.when(jnp.bitwise_and(n_fill, size) != 0)
                def _(size=size):
                    _rows_copy(zero_rows(size), xs_ref, 0, size, zsem).wait()

            return carry

        lax.fori_loop(0, N_EXPERTS, per_expert, 0)

        def block_copy(j):
            return _rows_copy(zero_ref, xs_ref, j * MOE_BLOCK, MOE_BLOCK, zsem)

        first_unused = fill_hi_ref[N_EXPERTS - 1] // MOE_BLOCK
        n_blocks = xs_ref.shape[1] // MOE_BLOCK

        def start_block(j, carry):
            block_copy(j).start()
            return carry

        def wait_block(j, carry):
            block_copy(0).wait()
            return carry

        lax.fori_loop(first_unused, n_blocks, start_block, 0)
        lax.fori_loop(first_unused, n_blocks, wait_block, 0)

    for t in range(td):
        for k in range(TOP_K):
            _row_copy(c_ref, t, xs_ref, dest_ref[k, t], sem).start(priority=k % 2)
    for k in range(TOP_K):
        _rows_copy(c_ref, xs_ref, 0, td, sem).wait()


def _dispatch(fill_lo, fill_hi, dest, c, n_slots):
    nc, n, _ = c.shape
    td = _pick(n, (512, 256))
    grid_spec = pltpu.PrefetchScalarGridSpec(
        num_scalar_prefetch=2,
        grid=(n // td,),
        in_specs=[pl.BlockSpec((TOP_K, td), lambda i, lo, hi: (0, i), memory_space=pltpu.SMEM),
                  pl.BlockSpec((nc, td, LANES), lambda i, lo, hi: (0, i, 0))],
        out_specs=pl.BlockSpec(memory_space=pl.ANY),
        scratch_shapes=[pltpu.VMEM((nc, MOE_BLOCK, LANES), c.dtype),
                        pltpu.SemaphoreType.DMA(()), pltpu.SemaphoreType.DMA(())],
    )
    return pl.pallas_call(
        _dispatch_body,
        grid_spec=grid_spec,
        out_shape=jax.ShapeDtypeStruct((nc, n_slots, LANES), c.dtype),
        compiler_params=pltpu.CompilerParams(dimension_semantics=("arbitrary",),
                                             vmem_limit_bytes=VMEM_LIMIT),
        name="dispatch",
    )(fill_lo, fill_hi, dest, c)


def _expert_body(be_ref, na_ref, nxt_ref, xs_ref, wg_hbm, wu_hbm, wd_hbm, yb_ref,
                 wg_ref, wu_ref, wd_ref, sg_ref, su_ref, sd_ref, wsem):
    j = pl.program_id(0)
    active = j < na_ref[0]
    e = be_ref[j]
    first = (j == 0) | (be_ref[jnp.maximum(j - 1, 0)] != e)

    def weight_copies(expert):
        return (pltpu.make_async_copy(wg_hbm.at[expert], sg_ref, wsem.at[0]),
                pltpu.make_async_copy(wu_hbm.at[expert], su_ref, wsem.at[1]),
                pltpu.make_async_copy(wd_hbm.at[expert], sd_ref, wsem.at[2]))

    @pl.when(j == 0)
    def _():
        for cp in weight_copies(e):
            cp.start()

    @pl.when(active & first)
    def _():
        for cp, stage, dst in zip(weight_copies(e), (sg_ref, su_ref, sd_ref), (wg_ref, wu_ref, wd_ref)):
            cp.wait()

            def convert(r, carry, stage=stage, dst=dst):
                rows = pl.ds(pl.multiple_of(r * LANES, LANES), LANES)
                dst[rows, :] = stage[rows, :].astype(BF16)
                return carry

            lax.fori_loop(0, stage.shape[0] // LANES, convert, 0)

        @pl.when(nxt_ref[e] < N_EXPERTS)
        def _():
            for cp in weight_copies(nxt_ref[e]):
                cp.start()

    @pl.when(active)
    def _():
        x = _unpack_bf16_pairs(_load_chunks(xs_ref))
        a = _dot(x, wg_ref[...])
        b = _dot(x, wu_ref[...])
        hid = (a * _sigmoid(a)) * b
        _store_chunks(yb_ref, _dot(hid.astype(BF16), wd_ref[...]))

    @pl.when(jnp.logical_not(active))
    def _():
        yb_ref[...] = jnp.zeros_like(yb_ref)


def _experts(block_exp, n_active, next_exp, xs, w_gate, w_up, w_down):
    nc_in, n_slots, _ = xs.shape
    d, de = w_gate.shape[1:]
    nc = d // LANES
    nb = n_slots // MOE_BLOCK
    assert d % LANES == 0 and de % LANES == 0
    blk = lambda j, be, na, nxt: (0, jnp.minimum(j, na[0] - 1), 0)
    grid_spec = pltpu.PrefetchScalarGridSpec(
        num_scalar_prefetch=3,
        grid=(nb,),
        in_specs=[pl.BlockSpec((nc_in, MOE_BLOCK, LANES), blk),
                  pl.BlockSpec(memory_space=pl.ANY),
                  pl.BlockSpec(memory_space=pl.ANY),
                  pl.BlockSpec(memory_space=pl.ANY)],
        out_specs=pl.BlockSpec((nc, MOE_BLOCK, LANES), lambda j, be, na, nxt: (0, j, 0)),
        scratch_shapes=[pltpu.VMEM((d, de), BF16), pltpu.VMEM((d, de), BF16), pltpu.VMEM((de, d), BF16),
                        pltpu.VMEM((d, de), F32), pltpu.VMEM((d, de), F32), pltpu.VMEM((de, d), F32),
                        pltpu.SemaphoreType.DMA((3,))],
    )
    return pl.pallas_call(
        _expert_body,
        grid_spec=grid_spec,
        out_shape=jax.ShapeDtypeStruct((nc, n_slots, LANES), F32),
        compiler_params=pltpu.CompilerParams(dimension_semantics=("arbitrary",),
                                             vmem_limit_bytes=VMEM_LIMIT),
        name="experts",
    )(block_exp, n_active, next_exp, xs, w_gate, w_up, w_down)


def _combine_body(dcur_ref, dnxt_ref, dnx2_ref, gate_ref, h1_ref, pp_ref, ps_ref, yb_ref, gple_ref, wpg_hbm, wpp_hbm,
                  gfin_ref, yp_ref, ys_ref, buf_ref, sem, wpg_ref, wpp_ref, stage_ref, wsem,
                  *, n_prompt_tiles, final_norm):
    i = pl.program_id(0)

    @pl.when(i == 0)
    def _():
        _load_as_bf16(wpg_hbm, wpg_ref, stage_ref, wsem)
        _load_as_bf16(wpp_hbm, wpp_ref, stage_ref, wsem)

    n_tiles = pl.num_programs(0)
    tc = h1_ref.shape[0]
    slot = lax.rem(i, 3)
    ahead = lax.rem(i + 2, 3)

    @pl.when(i == 0)
    def _():
        def body(t, carry):
            for k in range(TOP_K):
                _row_copy(yb_ref, dcur_ref[k, t], buf_ref.at[0, k], t, sem.at[0]).start()
                _row_copy(yb_ref, dnxt_ref[k, t], buf_ref.at[1, k], t, sem.at[1]).start()
            return carry

        lax.fori_loop(0, tc, body, 0)

    def wait_slot(s):
        for k in range(TOP_K):
            pltpu.make_async_copy(yb_ref.at[:, pl.ds(0, tc), :], buf_ref.at[s, k], sem.at[s]).wait()

    wait_slot(slot)
    gt = gate_ref[...].T
    h2 = h1_ref[...] + (gt[:, 0:1] * _load_chunks(buf_ref.at[slot, 0])
                        + gt[:, 1:2] * _load_chunks(buf_ref.at[slot, 1]))
    for t in range(tc):
        for k in range(TOP_K):
            _row_copy(yb_ref, dnx2_ref[k, t], buf_ref.at[ahead, k], t, sem.at[ahead]).start(priority=k % 2)
    a = _rms(h2, gple_ref[...]).astype(BF16)
    p = jnp.where(i < n_prompt_tiles, pp_ref[...], ps_ref[...]).astype(BF16)
    h3 = h2 + _sigmoid(_dot(a, wpg_ref[...])) * _dot(p, wpp_ref[...])
    if final_norm:
        h3 = _rms(h3, gfin_ref[...])

    @pl.when(i < n_prompt_tiles)
    def _():
        yp_ref[...] = h3

    @pl.when(i >= n_prompt_tiles)
    def _():
        ys_ref[...] = h3

    @pl.when(i == n_tiles - 1)
    def _():
        wait_slot(lax.rem(i + 1, 3))
        wait_slot(ahead)


def _combine(dest, gates, h1, p_p, p_s, yb, g_ple, w_pg, w_pp, g_final, *, n_prompt, final_norm):
    n, d = h1.shape
    ple = p_p.shape[1]
    tc = 256
    assert n_prompt % tc == 0 and (n - n_prompt) % tc == 0
    nt, npt = n // tc, n_prompt // tc
    nst = nt - npt
    assert nt >= 3
    const2 = lambda i: (0, 0)
    return pl.pallas_call(
        functools.partial(_combine_body, n_prompt_tiles=npt, final_norm=final_norm),
        grid=(nt,),
        in_specs=[pl.BlockSpec((TOP_K, tc), lambda i: (0, i), memory_space=pltpu.SMEM),
                  pl.BlockSpec((TOP_K, tc), lambda i: (0, jnp.minimum(i + 1, nt - 1)), memory_space=pltpu.SMEM),
                  pl.BlockSpec((TOP_K, tc), lambda i: (0, jnp.minimum(i + 2, nt - 1)), memory_space=pltpu.SMEM),
                  pl.BlockSpec((SUBLANES, tc), lambda i: (0, i)),
                  pl.BlockSpec((tc, d), lambda i: (i, 0)),
                  pl.BlockSpec((tc, ple), lambda i: (jnp.minimum(i, npt - 1), 0)),
                  pl.BlockSpec((tc, ple), lambda i: (jnp.clip(i - npt, 0, nst - 1), 0)),
                  pl.BlockSpec(memory_space=pl.ANY),
                  pl.BlockSpec((1, d), const2),
                  pl.BlockSpec(memory_space=pl.ANY),
                  pl.BlockSpec(memory_space=pl.ANY),
                  pl.BlockSpec((1, d), const2)],
        out_specs=[pl.BlockSpec((tc, d), lambda i: (jnp.minimum(i, npt - 1), 0)),
                   pl.BlockSpec((tc, d), lambda i: (jnp.clip(i - npt, 0, nst - 1), 0))],
        out_shape=[jax.ShapeDtypeStruct((n_prompt, d), F32),
                   jax.ShapeDtypeStruct((n - n_prompt, d), F32)],
        scratch_shapes=[pltpu.VMEM((3, TOP_K, d // LANES, tc, LANES), F32), pltpu.SemaphoreType.DMA((3,)),
                        pltpu.VMEM((d, d), BF16), pltpu.VMEM((ple, d), BF16),
                        pltpu.VMEM((WEIGHT_STAGE_ROWS, d), F32), pltpu.SemaphoreType.DMA((2,))],
        compiler_params=pltpu.CompilerParams(dimension_semantics=("arbitrary",),
                                             vmem_limit_bytes=VMEM_LIMIT),
        name="combine",
    )(dest, dest, dest, gates, h1, p_p, p_s, yb, g_ple, w_pg, w_pp, g_final)


def _layer(hp, hs, p_p, p_s, s0_sample, lb, batch, seq, dec_batch, dec_seq,
           g_mix, w_in, g_head, w_pa, ln_v_g, ln_v_b, w_s, b_s, w_pb, w_o,
           g_ffn, w_gr, b_gr, w_er, b_er, w_gate, w_up, w_down, g_ple, w_pg, w_pp, g_final, final_norm):
    n_prompt, d = hp.shape
    n = n_prompt + hs.shape[0]
    n_heads = d // HEAD_DIM
    row = lambda a: a.reshape(1, -1).astype(F32)

    z = _in_proj(hp, hs, row(g_mix), w_in)

    lb_row, gh_row = row(lb), row(g_head)
    s0_prompt = jnp.zeros((batch, n_heads, HEAD_DIM, HEAD_DIM), F32)
    og_p, st_p = _hgrn(z, s0_prompt, lb_row, gh_row, row_base=0, batch=batch, seq=seq,
                       d_model=d, name="hgrn_prompt")
    og_s, st_s = _hgrn(z, s0_sample.astype(F32), lb_row, gh_row, row_base=n_prompt, batch=dec_batch,
                       seq=dec_seq, d_model=d, name="hgrn_sample")

    start = PAST_LEN % MLP_CHUNK
    assert start + dec_seq <= MLP_CHUNK
    rep = MLP_CHUNK // dec_seq
    ws_s = jnp.tile(w_s[:, start:start + dec_seq, start:start + dec_seq], (1, rep, rep))
    bs_s = jnp.tile(b_s[:, start:start + dec_seq], (1, rep))
    h1, vn_p, vn_s = _mix(z, og_p, og_s, hp, hs, row(ln_v_g), row(ln_v_b), w_s, ws_s, b_s[..., None],
                          bs_s[..., None], w_pa, w_pb, w_o,
                          batch=batch, dec_seq=dec_seq)

    pad_rows = ROUTER_ROWS - N_EXPERTS - N_GROUPS
    wr = jnp.concatenate([w_er.T, w_gr.T, jnp.zeros((pad_rows, d), F32)], axis=0)
    br = jnp.concatenate([b_er, b_gr, jnp.zeros((pad_rows,), F32)]).reshape(ROUTER_ROWS, 1).astype(F32)
    c, idx, gates, cnt = _router(h1, row(g_ffn), wr, br)

    counts = cnt[:, 0]
    padded = (counts + MOE_BLOCK - 1) // MOE_BLOCK * MOE_BLOCK
    pad_end = jnp.cumsum(padded).astype(jnp.int32)
    pad_start = pad_end - padded
    n_blocks = -(-(n * TOP_K) // MOE_BLOCK) + N_EXPERTS
    block_first = jnp.arange(n_blocks, dtype=jnp.int32) * MOE_BLOCK
    block_exp = jnp.minimum(jnp.sum(pad_end[None, :] <= block_first[:, None], axis=1), N_EXPERTS - 1).astype(jnp.int32)
    n_active = jnp.maximum(pad_end[-1:] // MOE_BLOCK, 1)
    hit = idx[0:TOP_K, :, None] == jnp.arange(N_EXPERTS, dtype=jnp.int32)
    dest = jnp.sum(jnp.where(hit, pad_start, 0), axis=-1) + idx[TOP_K:2 * TOP_K]

    xs = _dispatch(pad_start + counts, pad_end, dest, c, n_blocks * MOE_BLOCK)
    used = jnp.where(counts > 0, jnp.arange(N_EXPERTS, dtype=jnp.int32), N_EXPERTS)
    next_exp = jnp.concatenate([lax.cummin(used, axis=0, reverse=True)[1:],
                                jnp.full((1,), N_EXPERTS, jnp.int32)]).astype(jnp.int32)
    yb = _experts(block_exp, n_active, next_exp, xs, w_gate, w_up, w_down)
    yp, ys = _combine(dest, gates, h1, p_p, p_s, yb, row(g_ple), w_pg, w_pp,
                      row(g_final), n_prompt=n_prompt, final_norm=final_norm)
    return yp, ys, st_p, st_s, vn_p, vn_s


def kernel(x_prompt, x_sample, p_prompt, p_sample, state_hgrn, g_mix, w_in, lb_logits, g_head, w_pa, ln_v_g, ln_v_b, w_s, b_s, w_pb, w_o, g_ffn, w_gr, b_gr, w_er, b_er, w_gate, w_up, w_down, g_ple, w_pg, w_pp, g_final):
    batch, seq, d = x_prompt.shape
    dec_batch, dec_seq, _ = x_sample.shape
    depth = g_mix.shape[0]
    n_prompt, n_sample = batch * seq, dec_batch * dec_seq
    width = w_pb.shape[1]
    lbs = jnp.cumsum(jax.nn.softmax(lb_logits.astype(F32), axis=0), axis=0)
    hp, hs = x_prompt.reshape(n_prompt, d), x_sample.reshape(n_sample, d)
    keep = min((seq - 1) % MLP_CHUNK + 1, seq)
    keep_s = min((PAST_LEN % MLP_CHUNK + dec_seq - 1) % MLP_CHUNK + 1, dec_seq)
    sp, ss, vp, vs = [], [], [], []
    for i in range(depth):
        p_p, p_s = p_prompt[i].reshape(n_prompt, -1), p_sample[i].reshape(n_sample, -1)
        hp, hs, st_p, st_s, vn_p, vn_s = _layer(
            hp, hs, p_p, p_s, state_hgrn[i], lbs[i], batch, seq, dec_batch, dec_seq,
            g_mix[i], w_in[i], g_head[i], w_pa[i], ln_v_g[i], ln_v_b[i], w_s[i], b_s[i], w_pb[i], w_o[i],
            g_ffn[i], w_gr[i], b_gr[i], w_er[i], b_er[i], w_gate[i], w_up[i], w_down[i],
            g_ple[i], w_pg[i], w_pp[i], g_final, i == depth - 1)
        sp.append(st_p.astype(x_prompt.dtype))
        ss.append(st_s.astype(state_hgrn.dtype))
        vn_last = vn_p.reshape(batch + 1, -1, width)[:batch]
        vp.append(vn_last[:, vn_last.shape[1] - keep:])
        vs.append(vn_s[:n_sample].reshape(dec_batch, dec_seq, width)[:, dec_seq - keep_s:])
    y_prompt = hp.reshape(batch, seq, d)
    y_sample = hs.reshape(dec_batch, dec_seq, d)
    return (y_prompt, y_sample, jnp.stack(sp), jnp.stack(ss), jnp.stack(vp), jnp.stack(vs))
```
